```python
import jax, jax.numpy as jnp
from jax import lax
import numpy as np

D_MODEL = 2048
BATCH = 4
SEQ = 2048
DEPTH = 1
DEC_BATCH = 128
DEC_SEQ = 1
PAST_LEN = 8192
PAGE_SIZE = 128

V_DIM = 128
N_HEADS = (D_MODEL // 2) // V_DIM
NOPE_DIM = 128
ROPE_DIM = 64
Q_LORA = 512
KV_LORA = 512
ROPE_BASE = 10000.0
Q_BLOCK = 128
SM_SCALE = (NOPE_DIM + ROPE_DIM) ** -0.5
CONV_CH = D_MODEL - N_HEADS * V_DIM
CONV_WIDTH = 31
MIX_WIDTH = N_HEADS * V_DIM + CONV_CH
IN_COLS = Q_LORA + KV_LORA + ROPE_DIM + 2 * CONV_CH
N_EXPERT_GROUPS = 8
EXPERTS_PER_GROUP = 8
N_EXPERTS = N_EXPERT_GROUPS * EXPERTS_PER_GROUP
TOP_K = 2
D_EXPERT = 1408
MOE_BLOCK = 64
EPS = 1e-6
NEG_INF = -1e30

kernel_name = 'hybrid_mla_conformer_hmoe_step'


def rmsnorm(x, g):
    xf = x.astype(jnp.float32)
    y = xf * lax.rsqrt(jnp.mean(xf * xf, axis=-1, keepdims=True) + EPS)
    return (y * g.astype(jnp.float32)).astype(x.dtype)


def layernorm(x, g, b):
    xf = x.astype(jnp.float32)
    xc = xf - jnp.mean(xf, axis=-1, keepdims=True)
    y = xc * lax.rsqrt(jnp.mean(xc * xc, axis=-1, keepdims=True) + EPS)
    return (y * g.astype(jnp.float32) + b.astype(jnp.float32)).astype(x.dtype)


def rope(x, pos):
    half = ROPE_DIM // 2
    inv_freq = ROPE_BASE ** (-jnp.arange(half, dtype=jnp.float32) / half)
    ang = pos.astype(jnp.float32)[:, None] * inv_freq[None, :]
    cos = jnp.cos(ang)[:, None, :]
    sin = jnp.sin(ang)[:, None, :]
    xf = x.astype(jnp.float32)
    x1, x2 = xf[..., :half], xf[..., half:]
    return jnp.concatenate([x1 * cos - x2 * sin, x1 * sin + x2 * cos], axis=-1).astype(x.dtype)


def mixer_inputs(x, lw):
    h = rmsnorm(x, lw['attn_norm_g'])
    z = jnp.einsum('bsd,dc->bsc', h, lw['w_in'])
    zq, zkv, zkr, zconv = jnp.split(z, [Q_LORA, Q_LORA + KV_LORA, Q_LORA + KV_LORA + ROPE_DIM], axis=-1)
    q = jnp.einsum('bsr,rhe->bshe', rmsnorm(zq, lw['q_norm_g']), lw['w_q_up'])
    c = rmsnorm(zkv, lw['kv_norm_g'])
    a, gate = jnp.split(zconv, 2, axis=-1)
    u = a * jax.nn.sigmoid(gate)
    return q[..., :NOPE_DIM], q[..., NOPE_DIM:], c, zkr, u


def mla_prompt(q_nope, q_rope, c, k_rope, w_kv_up):
    B, S = c.shape[:2]
    k_nope = jnp.einsum('bsc,chn->bshn', c, w_kv_up[..., :NOPE_DIM])
    v = jnp.einsum('bsc,chv->bshv', c, w_kv_up[..., NOPE_DIM:])
    n_blk = S // Q_BLOCK
    qn = q_nope.reshape(B, n_blk, Q_BLOCK, N_HEADS, NOPE_DIM).transpose(1, 0, 2, 3, 4)
    qr = q_rope.reshape(B, n_blk, Q_BLOCK, N_HEADS, ROPE_DIM).transpose(1, 0, 2, 3, 4)
    key_pos = jnp.arange(S)

    def attend_block(args):
        qn_b, qr_b, i = args
        s = jnp.einsum('bqhn,bkhn->bhqk', qn_b, k_nope) + jnp.einsum('bqhr,bkr->bhqk', qr_b, k_rope)
        s = s.astype(jnp.float32) * SM_SCALE
        q_pos = i * Q_BLOCK + jnp.arange(Q_BLOCK)
        s = jnp.where(key_pos[None, :] <= q_pos[:, None], s, NEG_INF)
        p = jax.nn.softmax(s, axis=-1).astype(v.dtype)
        return jnp.einsum('bhqk,bkhv->bqhv', p, v)

    o = lax.map(attend_block, (qn, qr, jnp.arange(n_blk)))
    return o.transpose(1, 0, 2, 3, 4).reshape(B, S, N_HEADS * V_DIM)


def mla_sample(q_nope, q_rope, c_new, kr_new, cache_c, cache_kr, page_table, w_kv_up):
    Bd, T = c_new.shape[:2]
    past = page_table.shape[1] * cache_c.shape[1]
    q_lat = jnp.einsum('bthn,chn->bthc', q_nope, w_kv_up[..., :NOPE_DIM])
    key_pos = jnp.arange(past + T)
    q_pos = past + jnp.arange(T)
    mask = key_pos[None, :] <= q_pos[:, None]

    def one_seq(args):
        ql, qr, pt, cn, krn = args
        c_all = jnp.concatenate([cache_c[pt].reshape(past, KV_LORA), cn.astype(cache_c.dtype)], axis=0)
        kr_all = jnp.concatenate([cache_kr[pt].reshape(past, ROPE_DIM), krn.astype(cache_kr.dtype)], axis=0)
        s = jnp.einsum('thc,sc->hts', ql, c_all) + jnp.einsum('thr,sr->hts', qr, kr_all)
        s = jnp.where(mask[None], s.astype(jnp.float32) * SM_SCALE, NEG_INF)
        p = jax.nn.softmax(s, axis=-1).astype(c_all.dtype)
        return jnp.einsum('hts,sc->thc', p, c_all)

    o_lat = lax.map(one_seq, (q_lat, q_rope, page_table, c_new, kr_new))
    o = jnp.einsum('bthc,chv->bthv', o_lat.astype(c_new.dtype), w_kv_up[..., NOPE_DIM:])
    return o.reshape(Bd, T, N_HEADS * V_DIM)


def conv_module_tail(u_pad, lw):
    y = lax.conv_general_dilated(u_pad, lw['conv_w'][:, None, :].astype(u_pad.dtype), window_strides=(1,),
                                 padding='VALID', dimension_numbers=('NWC', 'WIO', 'NWC'),
                                 feature_group_count=CONV_CH) + lw['conv_b']
    return jax.nn.silu(layernorm(y, lw['conv_ln_g'], lw['conv_ln_b']))


def route(h, lw):
    n_tok = h.shape[0]
    pg = jax.nn.softmax((h @ lw['w_router_group']).astype(jnp.float32) + lw['b_router_group'].astype(jnp.float32), axis=-1)
    g_p, g_idx = lax.top_k(pg, 1)
    le = (h @ lw['w_router_expert']).astype(jnp.float32) + lw['b_router_expert'].astype(jnp.float32)
    le = le.reshape(n_tok, N_EXPERT_GROUPS, EXPERTS_PER_GROUP)[jnp.arange(n_tok), g_idx[:, 0]]
    e_p, e_loc = lax.top_k(jax.nn.softmax(le, axis=-1), TOP_K)
    wts = g_p * e_p / jnp.sum(e_p, axis=-1, keepdims=True)
    return (g_idx * EXPERTS_PER_GROUP + e_loc).astype(jnp.int32), wts


def moe_ffn(h, e_idx, wts, w_gate, w_up, w_down):
    n_tok, d = h.shape
    n_assign = n_tok * TOP_K
    flat_e = e_idx.reshape(n_assign)
    flat_tok = jnp.repeat(jnp.arange(n_tok, dtype=jnp.int32), TOP_K)
    flat_w = wts.reshape(n_assign).astype(h.dtype)
    order = jnp.argsort(flat_e)
    sorted_e = flat_e[order]
    counts = jnp.zeros((N_EXPERTS,), jnp.int32).at[flat_e].add(1)
    padded = (counts + MOE_BLOCK - 1) // MOE_BLOCK * MOE_BLOCK
    pad_end = jnp.cumsum(padded)
    pad_start = pad_end - padded
    start = jnp.cumsum(counts) - counts
    dest = pad_start[sorted_e] + jnp.arange(n_assign, dtype=jnp.int32) - start[sorted_e]
    n_blocks = -(-n_assign // MOE_BLOCK) + N_EXPERTS
    n_rows = n_blocks * MOE_BLOCK
    row_tok = jnp.full((n_rows,), n_tok, jnp.int32).at[dest].set(flat_tok[order])
    row_w = jnp.zeros((n_rows,), h.dtype).at[dest].set(flat_w[order])
    block_e = jnp.minimum(jnp.searchsorted(pad_end, jnp.arange(n_blocks, dtype=jnp.int32) * MOE_BLOCK, side='right'),
                          N_EXPERTS - 1)
    h_ext = jnp.concatenate([h, jnp.zeros((1, d), h.dtype)], axis=0)
    xb = h_ext[row_tok].reshape(n_blocks, MOE_BLOCK, d)

    def expert_block(args):
        xblk, e = args
        return (jax.nn.silu(xblk @ w_gate[e]) * (xblk @ w_up[e])) @ w_down[e]

    yb = lax.map(expert_block, (xb, block_e)).reshape(n_rows, d) * row_w[:, None]
    return jnp.zeros((n_tok + 1, d), h.dtype).at[row_tok].add(yb)[:n_tok]


def moe_block(x, lw):
    B, S, d = x.shape
    h = rmsnorm(x, lw['ffn_norm_g']).reshape(B * S, d)
    e_idx, wts = route(h, lw)
    return moe_ffn(h, e_idx, wts, lw['w_exp_gate'], lw['w_exp_up'], lw['w_exp_down']).reshape(B, S, d)


def hybrid_layer(x, pos, conv_prev, attend, lw):
    qn, qr, c, kr, u = mixer_inputs(x, lw)
    qr = rope(qr, pos)
    kr = rope(kr[:, :, None, :], pos)[:, :, 0, :]
    o_att = attend(qn, qr, c, kr)
    u_pad = jnp.concatenate([conv_prev.astype(u.dtype), u], axis=1)
    o_conv = conv_module_tail(u_pad, lw)
    x = x + jnp.einsum('bsm,md->bsd', jnp.concatenate([o_att, o_conv], axis=-1), lw['w_out'])
    x = x + moe_block(x, lw)
    return x, c, kr, u_pad[:, u_pad.shape[1] - (CONV_WIDTH - 1):]


def setup_inputs(seed: int = 0) -> dict:
    key = jax.random.key(seed)
    ks = jax.random.split(key, 26)
    n_pages = PAST_LEN // PAGE_SIZE
    n_used = DEC_BATCH * n_pages
    n_pool = n_used + (n_used + 3) // 4
    L = DEPTH

    def nrm(k, shape, scale):
        return jax.random.normal(k, shape, jnp.float32) * scale

    def gain(k, shape):
        return 1.0 + nrm(k, shape, 0.02)

    page_table = jax.random.permutation(ks[5], n_pool)[:n_used].reshape(DEC_BATCH, n_pages).astype(jnp.int32)
    return {
        'x_prompt': nrm(ks[0], (BATCH, SEQ, D_MODEL), 1.0),
        'x_sample': nrm(ks[1], (DEC_BATCH, DEC_SEQ, D_MODEL), 1.0),
        'cache_kv_latent': nrm(ks[2], (L, n_pool, PAGE_SIZE, KV_LORA), 1.0),
        'cache_k_rope': nrm(ks[3], (L, n_pool, PAGE_SIZE, ROPE_DIM), 1.0),
        'state_conv': nrm(ks[4], (L, DEC_BATCH, CONV_WIDTH - 1, CONV_CH), 0.5),
        'page_table': page_table,
        'attn_norm_g': gain(ks[6], (L, D_MODEL)),
        'w_in': nrm(ks[7], (L, D_MODEL, IN_COLS), D_MODEL ** -0.5),
        'q_norm_g': gain(ks[8], (L, Q_LORA)),
        'w_q_up': nrm(ks[9], (L, Q_LORA, N_HEADS, NOPE_DIM + ROPE_DIM), Q_LORA ** -0.5),
        'kv_norm_g': gain(ks[10], (L, KV_LORA)),
        'w_kv_up': nrm(ks[11], (L, KV_LORA, N_HEADS, NOPE_DIM + V_DIM), KV_LORA ** -0.5),
        'conv_w': nrm(ks[12], (L, CONV_WIDTH, CONV_CH), CONV_WIDTH ** -0.5),
        'conv_b': nrm(ks[13], (L, CONV_CH), 0.02),
        'conv_ln_g': gain(ks[14], (L, CONV_CH)),
        'conv_ln_b': nrm(ks[15], (L, CONV_CH), 0.02),
        'w_out': nrm(ks[16], (L, MIX_WIDTH, D_MODEL), MIX_WIDTH ** -0.5),
        'ffn_norm_g': gain(ks[17], (L, D_MODEL)),
        'w_router_group': nrm(ks[18], (L, D_MODEL, N_EXPERT_GROUPS), D_MODEL ** -0.5),
        'b_router_group': nrm(ks[19], (L, N_EXPERT_GROUPS), 0.01),
        'w_router_expert': nrm(ks[20], (L, D_MODEL, N_EXPERTS), D_MODEL ** -0.5),
        'b_router_expert': nrm(ks[21], (L, N_EXPERTS), 0.01),
        'w_exp_gate': nrm(ks[22], (L, N_EXPERTS, D_MODEL, D_EXPERT), D_MODEL ** -0.5),
        'w_exp_up': nrm(ks[23], (L, N_EXPERTS, D_MODEL, D_EXPERT), D_MODEL ** -0.5),
        'w_exp_down': nrm(ks[24], (L, N_EXPERTS, D_EXPERT, D_MODEL), D_EXPERT ** -0.5),
        'final_norm_g': gain(ks[25], (D_MODEL,)),
    }


def reference(x_prompt, x_sample, cache_kv_latent, cache_k_rope, state_conv, page_table, attn_norm_g, w_in,
              q_norm_g, w_q_up, kv_norm_g, w_kv_up, conv_w, conv_b, conv_ln_g, conv_ln_b, w_out, ffn_norm_g,
              w_router_group, b_router_group, w_router_expert, b_router_expert, w_exp_gate, w_exp_up, w_exp_down,
              final_norm_g):
    B, S = x_prompt.shape[:2]
    T = x_sample.shape[1]
    past = page_table.shape[1] * cache_kv_latent.shape[2]
    pos_p = jnp.arange(S)
    pos_s = past + jnp.arange(T)
    xp, xs = x_prompt, x_sample
    lat_p, kr_p, cv_p, lat_s, kr_s, cv_s = [], [], [], [], [], []
    for l in range(DEPTH):
        lw = {
            'attn_norm_g': attn_norm_g[l], 'w_in': w_in[l], 'q_norm_g': q_norm_g[l], 'w_q_up': w_q_up[l],
            'kv_norm_g': kv_norm_g[l], 'conv_w': conv_w[l], 'conv_b': conv_b[l], 'conv_ln_g': conv_ln_g[l],
            'conv_ln_b': conv_ln_b[l], 'w_out': w_out[l], 'ffn_norm_g': ffn_norm_g[l],
            'w_router_group': w_router_group[l], 'b_router_group': b_router_group[l],
            'w_router_expert': w_router_expert[l], 'b_router_expert': b_router_expert[l],
            'w_exp_gate': w_exp_gate[l], 'w_exp_up': w_exp_up[l], 'w_exp_down': w_exp_down[l],
        }
        w_kv_up_l = w_kv_up[l]
        cache_c_l = cache_kv_latent[l]
        cache_kr_l = cache_k_rope[l]

        def attend_prompt(qn, qr, c, kr):
            return mla_prompt(qn, qr, c, kr, w_kv_up_l)

        def attend_sample(qn, qr, c, kr):
            return mla_sample(qn, qr, c, kr, cache_c_l, cache_kr_l, page_table, w_kv_up_l)

        zero_prev = jnp.zeros((B, CONV_WIDTH - 1, CONV_CH), xp.dtype)
        xp, c_new, kr_new, conv_new = hybrid_layer(xp, pos_p, zero_prev, attend_prompt, lw)
        lat_p.append(c_new)
        kr_p.append(kr_new)
        cv_p.append(conv_new)
        xs, c_new, kr_new, conv_new = hybrid_layer(xs, pos_s, state_conv[l], attend_sample, lw)
        lat_s.append(c_new)
        kr_s.append(kr_new)
        cv_s.append(conv_new)
    y_prompt = rmsnorm(xp, final_norm_g)
    y_sample = rmsnorm(xs, final_norm_g)
    return (y_prompt, y_sample, jnp.stack(lat_p), jnp.stack(kr_p), jnp.stack(cv_p),
            jnp.stack(lat_s), jnp.stack(kr_s), jnp.stack(cv_s))
```

```python
import functools

import jax
import jax.numpy as jnp
from jax import lax
from jax.experimental import pallas as pl
from jax.experimental.pallas import tpu as pltpu

F32 = jnp.float32
BF16 = jnp.bfloat16

EPS = 1e-6
NEG_INF = -1e30
ROPE_BASE = 10000.0

LANES = 128
VMEM_LIMIT = 56 * 1024 * 1024

NOPE_DIM = 128
ROPE_DIM = 64
TOP_K = 2
HEAD_PAD = 256
QS_HEAD = 640

MOE_ROWS = 256
TOK_TILE = 416
CMB_TILE = 128


def _cparams(sem, vmem=VMEM_LIMIT):
    return pltpu.CompilerParams(dimension_semantics=sem, vmem_limit_bytes=vmem)


def _rms(x, g):
    return x * lax.rsqrt(jnp.mean(x * x, axis=-1, keepdims=True) + EPS) * g


def _rope128(v, c, a, b):
    return v * c + pltpu.roll(v, 96, 1) * a + pltpu.roll(v, 32, 1) * b


def _sigmoid(x):
    return 1.0 / (1.0 + jnp.exp(-x))


def _nt_dot(a, b):
    return lax.dot_general(a, b, (((1,), (1,)), ((), ())), preferred_element_type=F32)


def _in_proj_kernel(*refs, n_heads, q_lora, kv_lora, conv_ch, prompt):
    (x_ref, g_ref, w_ref, qg_ref, wq_ref, kvg_ref, wkv_ref, rc_ref, ra_ref, rb_ref) = refs[:10]
    outs = refs[10:]
    if prompt:
        q_ref, c_ref, kr_ref, u_ref, kv_ref, krb_ref = outs
    else:
        q_ref, c_ref, kr_ref, u_ref = outs
    o_kv = q_lora
    o_a = q_lora + kv_lora
    o_g = o_a + conv_ch
    o_kr = o_g + conv_ch

    h = _rms(x_ref[...], g_ref[...]).astype(BF16)
    rc, ra, rb = rc_ref[...], ra_ref[...], rb_ref[...]

    zq = jnp.dot(h, w_ref[:, 0:q_lora], preferred_element_type=F32)
    qn = _rms(zq, qg_ref[...]).astype(BF16)
    qf = jnp.dot(qn, wq_ref[...], preferred_element_type=F32)

    zkv = jnp.dot(h, w_ref[:, o_kv:o_a], preferred_element_type=F32)
    c = _rms(zkv, kvg_ref[...])
    c_ref[...] = c

    zkr = jnp.dot(h, w_ref[:, o_kr:o_kr + LANES], preferred_element_type=F32)
    kr = _rope128(zkr, rc, ra, rb)
    kr_ref[...] = kr[:, :ROPE_DIM]

    a = jnp.dot(h, w_ref[:, o_a:o_g], preferred_element_type=F32)
    gate = jnp.dot(h, w_ref[:, o_g:o_kr], preferred_element_type=F32)
    u_ref[...] = a * _sigmoid(gate)

    if prompt:
        krb_ref[...] = kr.astype(BF16)
        kv_ref[...] = jnp.dot(c.astype(BF16), wkv_ref[...], preferred_element_type=F32).astype(BF16)
        for hd in range(n_heads):
            o = hd * HEAD_PAD
            q_ref[:, o:o + NOPE_DIM] = qf[:, o:o + NOPE_DIM].astype(BF16)
            q_ref[:, o + NOPE_DIM:o + HEAD_PAD] = _rope128(
                qf[:, o + NOPE_DIM:o + HEAD_PAD], rc, ra, rb).astype(BF16)
    else:
        for hd in range(n_heads):
            o = hd * HEAD_PAD
            qnope = qf[:, o:o + NOPE_DIM].astype(BF16)
            wkn = wkv_ref[:, o:o + NOPE_DIM]
            qo = hd * QS_HEAD
            q_ref[:, qo:qo + kv_lora] = _nt_dot(qnope, wkn).astype(BF16)
            q_ref[:, qo + kv_lora:qo + QS_HEAD] = _rope128(
                qf[:, o + NOPE_DIM:o + HEAD_PAD], rc, ra, rb).astype(BF16)


def _in_proj(x, attn_g, w_in_p, q_g, wq_p, kv_g, wkv_p, rc, ra, rb, *, tm, prompt, n_heads, q_lora, kv_lora,
             conv_ch):
    n, d = x.shape
    n_tab = rc.shape[0] // tm
    const = lambda i: (0, 0)
    row = lambda i: (i, 0)
    tab = lambda i: (i % n_tab, 0)
    single = pl.Buffered(1)
    in_specs = [
        pl.BlockSpec((tm, d), row),
        pl.BlockSpec((1, d), const),
        pl.BlockSpec(w_in_p.shape, const, pipeline_mode=single),
        pl.BlockSpec((1, q_lora), const),
        pl.BlockSpec(wq_p.shape, const, pipeline_mode=single),
        pl.BlockSpec((1, kv_lora), const),
        pl.BlockSpec(wkv_p.shape, const, pipeline_mode=single),
        pl.BlockSpec((tm, LANES), tab),
        pl.BlockSpec((tm, LANES), tab),
        pl.BlockSpec((tm, LANES), tab),
    ]
    q_cols = n_heads * (HEAD_PAD if prompt else QS_HEAD)
    out_shape = [
        jax.ShapeDtypeStruct((n, q_cols), BF16),
        jax.ShapeDtypeStruct((n, kv_lora), F32),
        jax.ShapeDtypeStruct((n, ROPE_DIM), F32),
        jax.ShapeDtypeStruct((n, conv_ch), F32),
    ]
    out_specs = [
        pl.BlockSpec((tm, q_cols), row),
        pl.BlockSpec((tm, kv_lora), row),
        pl.BlockSpec((tm, ROPE_DIM), row),
        pl.BlockSpec((tm, conv_ch), row),
    ]
    if prompt:
        out_shape += [jax.ShapeDtypeStruct((n, wkv_p.shape[1]), BF16), jax.ShapeDtypeStruct((n, LANES), BF16)]
        out_specs += [pl.BlockSpec((tm, wkv_p.shape[1]), row), pl.BlockSpec((tm, LANES), row)]
    kern = functools.partial(_in_proj_kernel, n_heads=n_heads, q_lora=q_lora, kv_lora=kv_lora, conv_ch=conv_ch,
                             prompt=prompt)
    return pl.pallas_call(
        kern, grid=(n // tm,), in_specs=in_specs, out_specs=out_specs, out_shape=out_shape,
        compiler_params=_cparams(("arbitrary",)),
        name="in_proj_prompt" if prompt else "in_proj_sample",
    )(x, attn_g, w_in_p, q_g, wq_p, kv_g, wkv_p, rc, ra, rb)


def _attn_prompt_kernel(q_ref, kv_ref, krb_ref, o_ref, kcat_ref, *, tq, tk, scale):
    i = pl.program_id(2)

    @pl.when(i == 0)
    def _():
        kcat_ref[:, :NOPE_DIM] = kv_ref[:, :NOPE_DIM]
        kcat_ref[:, NOPE_DIM:] = krb_ref[...]

    q = q_ref[...]
    v_dim = o_ref.shape[1]

    def scores(j):
        k = kcat_ref[pl.ds(pl.multiple_of(j * tk, tk), tk), :]
        return _nt_dot(q, k) * scale

    def update(j, s, carry):
        m, l, acc = carry
        m_new = jnp.maximum(m, jnp.max(s, axis=-1, keepdims=True))
        alpha = jnp.exp(m - m_new)
        p = jnp.exp(s - m_new)
        v = kv_ref[pl.ds(pl.multiple_of(j * tk, tk), tk), NOPE_DIM:]
        acc = acc * alpha + jnp.dot(p.astype(BF16), v, preferred_element_type=F32)
        return m_new, l * alpha + jnp.sum(p, axis=-1, keepdims=True), acc

    def body(j, carry):
        return update(j, scores(j), carry)

    init = (jnp.full((tq, 1), NEG_INF, F32), jnp.zeros((tq, 1), F32), jnp.zeros((tq, v_dim), F32))
    carry = lax.fori_loop(0, i, body, init)
    rowi = lax.broadcasted_iota(jnp.int32, (tq, tk), 0)
    coli = lax.broadcasted_iota(jnp.int32, (tq, tk), 1)
    s = jnp.where(coli <= rowi, scores(i), NEG_INF)
    m, l, acc = update(i, s, carry)
    o_ref[...] = (acc / l).astype(o_ref.dtype)


def _attn_prompt(q, kv, krb, *, batch, seq, n_heads, v_dim, scale, tq=256):
    n = q.shape[0]
    nq = seq // tq
    kern = functools.partial(_attn_prompt_kernel, tq=tq, tk=tq, scale=scale)
    return pl.pallas_call(
        kern, grid=(batch, n_heads, nq),
        in_specs=[
            pl.BlockSpec((tq, HEAD_PAD), lambda b, h, i: (b * nq + i, h)),
            pl.BlockSpec((seq, HEAD_PAD), lambda b, h, i: (b, h)),
            pl.BlockSpec((seq, LANES), lambda b, h, i: (b, 0)),
        ],
        out_specs=pl.BlockSpec((tq, v_dim), lambda b, h, i: (b * nq + i, h)),
        out_shape=jax.ShapeDtypeStruct((n, n_heads * v_dim), BF16),
        scratch_shapes=[pltpu.VMEM((seq, HEAD_PAD), BF16)],
        compiler_params=_cparams(("arbitrary", "arbitrary", "arbitrary")),
        name="attn_prompt",
    )(q, kv, krb)


def _ln_silu(y, g, b):
    yc = y - jnp.mean(y, axis=-1, keepdims=True)
    z = yc * lax.rsqrt(jnp.mean(yc * yc, axis=-1, keepdims=True) + EPS) * g + b
    return z * _sigmoid(z)


def _conv_prompt_kernel(halo_ref, u_ref, w_ref, b_ref, g_ref, beta_ref, o_ref, win_ref, y_ref, *, tt, halo,
                        width, rc):
    i = pl.program_id(1)

    @pl.when(i == 0)
    def _():
        win_ref[0:halo, :] = jnp.zeros((halo, win_ref.shape[1]), F32)

    @pl.when(i > 0)
    def _():
        win_ref[0:halo, :] = halo_ref[...]

    win_ref[halo:, :] = u_ref[...]
    base = halo - (width - 1)
    ch = win_ref.shape[1]
    for lc in range(ch // LANES):
        ls = slice(lc * LANES, (lc + 1) * LANES)
        wl = w_ref[:, ls]
        for r in range(tt // rc):
            acc = jnp.zeros((rc, LANES), F32)
            for k in range(width):
                s0 = r * rc + base + k
                acc = acc + win_ref[s0:s0 + rc, ls] * wl[k:k + 1, :]
            y_ref[r * rc:(r + 1) * rc, ls] = acc
    o_ref[...] = _ln_silu(y_ref[...] + b_ref[...], g_ref[...], beta_ref[...]).astype(o_ref.dtype)


def _conv_prompt(u, conv_w, conv_b, ln_g, ln_b, *, batch, seq, tt=256, halo=32, rc=64):
    n, ch = u.shape
    width = conv_w.shape[0]
    nt = seq // tt
    hb = tt // halo
    const = lambda b, i: (0, 0)
    kern = functools.partial(_conv_prompt_kernel, tt=tt, halo=halo, width=width, rc=rc)
    return pl.pallas_call(
        kern, grid=(batch, nt),
        in_specs=[
            pl.BlockSpec((halo, ch), lambda b, i: (jnp.maximum((b * nt + i) * hb - 1, 0), 0)),
            pl.BlockSpec((tt, ch), lambda b, i: (b * nt + i, 0)),
            pl.BlockSpec((width, ch), const),
            pl.BlockSpec((1, ch), const),
            pl.BlockSpec((1, ch), const),
            pl.BlockSpec((1, ch), const),
        ],
        out_specs=pl.BlockSpec((tt, ch), lambda b, i: (b * nt + i, 0)),
        out_shape=jax.ShapeDtypeStruct((n, ch), BF16),
        scratch_shapes=[pltpu.VMEM((halo + tt, ch), F32), pltpu.VMEM((tt, ch), F32)],
        compiler_params=_cparams(("arbitrary", "arbitrary")),
        name="conv_prompt",
    )(u, u, conv_w, conv_b, ln_g, ln_b)


def _conv_sample_kernel(st_ref, u_ref, w_ref, b_ref, g_ref, beta_ref, o_ref, *, width):
    w = w_ref[...]
    y = jnp.sum(st_ref[...] * w[None, :width - 1, :], axis=1) + u_ref[...] * w[width - 1:width, :] + b_ref[...]
    o_ref[...] = _ln_silu(y, g_ref[...], beta_ref[...]).astype(o_ref.dtype)


def _conv_sample(state, u, conv_w, conv_b, ln_g, ln_b, *, tb=32):
    nb, hist, ch = state.shape
    width = conv_w.shape[0]
    const = lambda i: (0, 0)
    return pl.pallas_call(
        functools.partial(_conv_sample_kernel, width=width), grid=(nb // tb,),
        in_specs=[
            pl.BlockSpec((tb, hist, ch), lambda i: (i, 0, 0)),
            pl.BlockSpec((tb, ch), lambda i: (i, 0)),
            pl.BlockSpec((width, ch), const),
            pl.BlockSpec((1, ch), const),
            pl.BlockSpec((1, ch), const),
            pl.BlockSpec((1, ch), const),
        ],
        out_specs=pl.BlockSpec((tb, ch), lambda i: (i, 0)),
        out_shape=jax.ShapeDtypeStruct((nb, ch), BF16),
        compiler_params=_cparams(("arbitrary",)),
        name="conv_sample",
    )(state, u, conv_w, conv_b, ln_g, ln_b)


def _attn_sample_kernel(pt_ref, qs_ref, cn_ref, krn_ref, cache_c, cache_kr, o_ref, cbuf, rbuf, sem, m_ref, l_ref,
                        acc_ref, *, pages, n_chunks, page, kv_lora, scale):
    b = pl.program_id(0)
    ch = pl.program_id(1)
    g = b * n_chunks + ch
    n_steps = pl.num_programs(0) * n_chunks
    slot = g % 2

    def copies(step, sl):
        out = []
        for p in range(pages):
            pid = pt_ref[step * pages + p]
            rows = pl.ds(p * page, page)
            out.append(pltpu.make_async_copy(cache_c.at[pid], cbuf.at[sl, rows, :], sem.at[0, sl]))
            out.append(pltpu.make_async_copy(cache_kr.at[pid], rbuf.at[sl, rows, :], sem.at[1, sl]))
        return out

    @pl.when(g == 0)
    def _():
        for cp in copies(0, 0):
            cp.start()

    @pl.when(g + 1 < n_steps)
    def _():
        for cp in copies(g + 1, 1 - slot):
            cp.start()

    @pl.when(ch == 0)
    def _():
        m_ref[...] = jnp.full(m_ref.shape, NEG_INF, F32)
        l_ref[...] = jnp.zeros(l_ref.shape, F32)
        acc_ref[...] = jnp.zeros(acc_ref.shape, F32)

    for cp in copies(g, slot):
        cp.wait()

    q = qs_ref[0]
    ql = q[:, :kv_lora]
    qr = q[:, kv_lora:kv_lora + ROPE_DIM]
    cb = cbuf[slot].astype(BF16)
    rb = rbuf[slot].astype(BF16)
    s = (_nt_dot(ql, cb) + _nt_dot(qr, rb)) * scale
    m_old = m_ref[...]
    m_new = jnp.maximum(m_old, jnp.max(s, axis=-1, keepdims=True))
    alpha = jnp.exp(m_old - m_new)
    p = jnp.exp(s - m_new)
    l_ref[...] = l_ref[...] * alpha + jnp.sum(p, axis=-1, keepdims=True)
    acc_ref[...] = acc_ref[...] * alpha + jnp.dot(p.astype(BF16), cb, preferred_element_type=F32)
    m_ref[...] = m_new

    @pl.when(ch == n_chunks - 1)
    def _():
        cn = cn_ref[0].astype(BF16).astype(F32)
        krn = krn_ref[0].astype(BF16).astype(F32)
        s_new = (jnp.sum(ql.astype(F32) * cn, axis=-1, keepdims=True)
                 + jnp.sum(qr.astype(F32) * krn, axis=-1, keepdims=True)) * scale
        m_o = m_ref[...]
        m_n = jnp.maximum(m_o, s_new)
        al = jnp.exp(m_o - m_n)
        p_new = jnp.exp(s_new - m_n)
        l_fin = l_ref[...] * al + p_new
        acc = acc_ref[...] * al + p_new.astype(BF16).astype(F32) * cn
        o_ref[0] = acc / l_fin


def _attn_sample(page_table, qs3, c_new3, kr_new3, cache_c, cache_kr, *, scale, pages=16):
    nb, n_heads, qw = qs3.shape
    n_pages = page_table.shape[1]
    page, kv_lora = cache_c.shape[1:]
    rope = cache_kr.shape[2]
    n_chunks = n_pages // pages
    keys = pages * page
    kern = functools.partial(_attn_sample_kernel, pages=pages, n_chunks=n_chunks, page=page, kv_lora=kv_lora,
                             scale=scale)
    grid_spec = pltpu.PrefetchScalarGridSpec(
        num_scalar_prefetch=1, grid=(nb, n_chunks),
        in_specs=[
            pl.BlockSpec((1, n_heads, qw), lambda b, c, pt: (b, 0, 0)),
            pl.BlockSpec((1, 1, kv_lora), lambda b, c, pt: (b, 0, 0)),
            pl.BlockSpec((1, 1, rope), lambda b, c, pt: (b, 0, 0)),
            pl.BlockSpec(memory_space=pl.ANY),
            pl.BlockSpec(memory_space=pl.ANY),
        ],
        out_specs=pl.BlockSpec((1, n_heads, kv_lora), lambda b, c, pt: (b, 0, 0)),
        scratch_shapes=[
            pltpu.VMEM((2, keys, kv_lora), F32),
            pltpu.VMEM((2, keys, rope), F32),
            pltpu.SemaphoreType.DMA((2, 2)),
            pltpu.VMEM((n_heads, 1), F32),
            pltpu.VMEM((n_heads, 1), F32),
            pltpu.VMEM((n_heads, kv_lora), F32),
        ],
    )
    return pl.pallas_call(
        kern, grid_spec=grid_spec,
        out_shape=jax.ShapeDtypeStruct((nb, n_heads, kv_lora), F32),
        compiler_params=_cparams(("arbitrary", "arbitrary")),
        name="attn_sample",
    )(page_table.reshape(-1), qs3, c_new3, kr_new3, cache_c, cache_kr)


def _v_up_kernel(ol_ref, wkv_ref, o_ref, *, n_heads, kv_lora, v_dim):
    for hd in range(n_heads):
        ol = ol_ref[:, hd * kv_lora:(hd + 1) * kv_lora].astype(BF16)
        wv = wkv_ref[:, hd * HEAD_PAD + NOPE_DIM:(hd + 1) * HEAD_PAD]
        o_ref[:, hd * v_dim:(hd + 1) * v_dim] = jnp.dot(ol, wv, preferred_element_type=F32).astype(o_ref.dtype)


def _v_up(o_lat2, wkv_p, *, n_heads, kv_lora, v_dim):
    nb = o_lat2.shape[0]
    return pl.pallas_call(
        functools.partial(_v_up_kernel, n_heads=n_heads, kv_lora=kv_lora, v_dim=v_dim),
        out_shape=jax.ShapeDtypeStruct((nb, n_heads * v_dim), BF16),
        compiler_params=_cparams(None),
        name="v_up_sample",
    )(o_lat2, wkv_p)


def _out_proj_kernel(*refs, half, aliased):
    if aliased:
        a_ref, b_ref, w_ref, x_ref, _, o_ref = refs
    else:
        a_ref, b_ref, w_ref, x_ref, o_ref = refs
    o_ref[...] = (x_ref[...] + jnp.dot(a_ref[...], w_ref[0:half, :], preferred_element_type=F32)
                  + jnp.dot(b_ref[...], w_ref[half:, :], preferred_element_type=F32))


def _out_proj(a, b, w, x, *, tm, n_total, row_off, prev=None):
    n, half = a.shape
    d = w.shape[1]
    off = row_off // tm
    const = lambda i: (0, 0)
    row = lambda i: (i, 0)
    in_specs = [
        pl.BlockSpec((tm, half), row),
        pl.BlockSpec((tm, half), row),
        pl.BlockSpec(w.shape, const, pipeline_mode=pl.Buffered(1)),
        pl.BlockSpec((tm, d), row),
    ]
    args = [a, b, w, x]
    aliases = {}
    if prev is not None:
        in_specs.append(pl.BlockSpec(memory_space=pl.ANY))
        args.append(prev)
        aliases = {4: 0}
    return pl.pallas_call(
        functools.partial(_out_proj_kernel, half=half, aliased=prev is not None), grid=(n // tm,),
        in_specs=in_specs,
        out_specs=pl.BlockSpec((tm, d), lambda i: (i + off, 0)),
        out_shape=jax.ShapeDtypeStruct((n_total, d), F32),
        input_output_aliases=aliases,
        compiler_params=_cparams(("arbitrary",)),
        name="out_proj",
    )(*args)


def _router_kernel(x_ref, g_ref, whi_ref, wlo_ref, bias_ref, info_ref, cnt_ref, run_ref, *, n_exp, n_grp, per_grp):
    i = pl.program_id(0)

    @pl.when(i == 0)
    def _():
        run_ref[...] = jnp.zeros(run_ref.shape, F32)

    tm = x_ref.shape[0]
    h = _rms(x_ref[...], g_ref[...])
    h_hi = h.astype(BF16)
    h_lo = (h - h_hi.astype(F32)).astype(BF16)
    whi = whi_ref[...]
    logits = (jnp.dot(h_hi, whi, preferred_element_type=F32) + jnp.dot(h_lo, whi, preferred_element_type=F32)
              + jnp.dot(h_hi, wlo_ref[...], preferred_element_type=F32) + bias_ref[...])
    lane_i = lax.broadcasted_iota(jnp.int32, (tm, LANES), 1)
    lane = lane_i.astype(F32)
    big = float(LANES)

    def first_max(vals):
        mx = jnp.max(vals, axis=-1, keepdims=True)
        idx = jnp.min(jnp.where(vals == mx, lane, big), axis=-1, keepdims=True)
        return mx, idx

    is_grp = (lane_i >= n_exp) & (lane_i < n_exp + n_grp)
    lg = jnp.where(is_grp, logits, NEG_INF)
    g_max, g_lane = first_max(lg)
    g_p = 1.0 / jnp.sum(jnp.where(is_grp, jnp.exp(lg - g_max), 0.0), axis=-1, keepdims=True)
    g_idx = g_lane - float(n_exp)
    lo = g_idx * float(per_grp)
    in_grp = (lane >= lo) & (lane < lo + float(per_grp))
    le = jnp.where(in_grp, logits, NEG_INF)
    m1, i1 = first_max(le)
    m2, i2 = first_max(jnp.where(lane == i1, NEG_INF, le))
    e2 = jnp.exp(m2 - m1)
    w1 = g_p / (1.0 + e2)
    w2 = g_p * e2 / (1.0 + e2)

    oh1 = lane == i1
    oh2 = lane == i2
    onehot = jnp.where(oh1 | oh2, 1.0, 0.0)
    r_i = lax.broadcasted_iota(jnp.int32, (tm, tm), 0)
    c_i = lax.broadcasted_iota(jnp.int32, (tm, tm), 1)
    lower = jnp.where(c_i < r_i, 1.0, 0.0).astype(BF16)
    before = run_ref[...] + jnp.dot(lower, onehot.astype(BF16), preferred_element_type=F32)
    rank1 = jnp.sum(jnp.where(oh1, before, 0.0), axis=-1, keepdims=True)
    rank2 = jnp.sum(jnp.where(oh2, before, 0.0), axis=-1, keepdims=True)
    run_ref[...] = run_ref[...] + jnp.sum(onehot, axis=0, keepdims=True)
    cnt_ref[...] = run_ref[...]

    info = jnp.where(lane_i == 0, i1, jnp.where(lane_i == 1, i2, jnp.where(lane_i == 2, rank1, jnp.where(
        lane_i == 3, rank2, jnp.where(lane_i == 4, w1, jnp.where(lane_i == 5, w2, 0.0))))))
    info_ref[...] = info


def _router(x1, ffn_g, w_hi, w_lo, bias, *, tm, n_exp, n_grp, per_grp):
    n, d = x1.shape
    const = lambda i: (0, 0)
    return pl.pallas_call(
        functools.partial(_router_kernel, n_exp=n_exp, n_grp=n_grp, per_grp=per_grp), grid=(n // tm,),
        in_specs=[
            pl.BlockSpec((tm, d), lambda i: (i, 0)),
            pl.BlockSpec((1, d), const),
            pl.BlockSpec((d, LANES), const),
            pl.BlockSpec((d, LANES), const),
            pl.BlockSpec((1, LANES), const),
        ],
        out_specs=[pl.BlockSpec((tm, LANES), lambda i: (i, 0)), pl.BlockSpec((1, LANES), const)],
        out_shape=[jax.ShapeDtypeStruct((n, LANES), F32), jax.ShapeDtypeStruct((1, LANES), F32)],
        scratch_shapes=[pltpu.VMEM((1, LANES), F32)],
        compiler_params=_cparams(("arbitrary",)),
        name="router",
    )(x1, ffn_g, w_hi, w_lo, bias)


def _dispatch_kernel(dest_ref, x_ref, g_ref, xs_in, xs_ref, hbuf, sem, *, tm):
    del xs_in
    i = pl.program_id(0)
    n = pl.num_programs(0)
    slot = i % 2

    def row_copy(sl, r, d):
        return pltpu.make_async_copy(hbuf.at[sl, pl.ds(r, 1), :], xs_ref.at[pl.ds(d, 1), :], sem.at[sl])

    def wait_all(sl):
        def body(r, c):
            for _ in range(TOP_K):
                row_copy(sl, 0, 0).wait()
            return c
        lax.fori_loop(0, tm, body, 0)

    hbuf[slot] = _rms(x_ref[...], g_ref[...])

    def issue(r, c):
        a = (i * tm + r) * TOP_K
        for k in range(TOP_K):
            row_copy(slot, r, dest_ref[a + k]).start()
        return c
    lax.fori_loop(0, tm, issue, 0)

    @pl.when(i > 0)
    def _():
        wait_all(1 - slot)

    @pl.when(i == n - 1)
    def _():
        wait_all(slot)


def _dispatch(dest, x1, ffn_g, xs_zero, *, tm):
    n, d = x1.shape
    grid_spec = pltpu.PrefetchScalarGridSpec(
        num_scalar_prefetch=1, grid=(n // tm,),
        in_specs=[
            pl.BlockSpec((tm, d), lambda i, ds: (i, 0)),
            pl.BlockSpec((1, d), lambda i, ds: (0, 0)),
            pl.BlockSpec(memory_space=pl.ANY),
        ],
        out_specs=pl.BlockSpec(memory_space=pl.ANY),
        scratch_shapes=[pltpu.VMEM((2, tm, d), F32), pltpu.SemaphoreType.DMA((2,))],
    )
    return pl.pallas_call(
        functools.partial(_dispatch_kernel, tm=tm), grid_spec=grid_spec,
        out_shape=jax.ShapeDtypeStruct(xs_zero.shape, F32),
        input_output_aliases={3: 0},
        compiler_params=_cparams(("arbitrary",)),
        name="moe_dispatch",
    )(dest, x1, ffn_g, xs_zero)


def _expert_up_kernel(be_ref, nb_ref, x_ref, wg_ref, wu_ref, o_ref, *, kc):
    del be_ref

    @pl.when(pl.program_id(0) < nb_ref[0])
    def _():
        d = x_ref.shape[1]
        g = None
        u = None
        for k0 in range(0, d, kc):
            xk = x_ref[:, k0:k0 + kc].astype(BF16)
            gk = jnp.dot(xk, wg_ref[0, k0:k0 + kc, :].astype(BF16), preferred_element_type=F32)
            uk = jnp.dot(xk, wu_ref[0, k0:k0 + kc, :].astype(BF16), preferred_element_type=F32)
            g = gk if g is None else g + gk
            u = uk if u is None else u + uk
        o_ref[...] = (g * _sigmoid(g) * u).astype(o_ref.dtype)


def _expert_down_kernel(be_ref, nb_ref, h_ref, wd_ref, o_ref, *, kc):
    del be_ref

    @pl.when(pl.program_id(0) < nb_ref[0])
    def _():
        f = h_ref.shape[1]
        y = None
        for k0 in range(0, f, kc):
            yk = jnp.dot(h_ref[:, k0:k0 + kc], wd_ref[0, k0:k0 + kc, :].astype(BF16), preferred_element_type=F32)
            y = yk if y is None else y + yk
        o_ref[...] = y


def _experts(block_e, n_blk, xs, w_gate, w_up, w_down, *, tb):
    rows, d = xs.shape
    n_exp, _, f = w_gate.shape
    n_blocks = rows // tb
    blk = lambda j, be, nb: (jnp.minimum(j, nb[0] - 1), 0)
    wsel = lambda j, be, nb: (be[j], 0, 0)
    hmid = pl.pallas_call(
        functools.partial(_expert_up_kernel, kc=512),
        grid_spec=pltpu.PrefetchScalarGridSpec(
            num_scalar_prefetch=2, grid=(n_blocks,),
            in_specs=[pl.BlockSpec((tb, d), blk), pl.BlockSpec((1, d, f), wsel), pl.BlockSpec((1, d, f), wsel)],
            out_specs=pl.BlockSpec((tb, f), blk),
        ),
        out_shape=jax.ShapeDtypeStruct((rows, f), BF16),
        compiler_params=_cparams(("arbitrary",), 60 * 1024 * 1024),
        name="expert_up",
    )(block_e, n_blk, xs, w_gate, w_up)
    return pl.pallas_call(
        functools.partial(_expert_down_kernel, kc=f // 2),
        grid_spec=pltpu.PrefetchScalarGridSpec(
            num_scalar_prefetch=2, grid=(n_blocks,),
            in_specs=[pl.BlockSpec((tb, f), blk), pl.BlockSpec((1, f, d), wsel)],
            out_specs=pl.BlockSpec((tb, d), blk),
        ),
        out_shape=jax.ShapeDtypeStruct((rows, d), F32),
        compiler_params=_cparams(("arbitrary",)),
        name="expert_down",
    )(block_e, n_blk, hmid, w_down)


def _combine_kernel(dest_ref, x_ref, info_ref, g_ref, ys_ref, yp_ref, ys_out_ref, rbuf, sem, *, tm, n_prompt_tiles):
    i = pl.program_id(0)
    n = pl.num_programs(0)
    slot = i % 2

    def row_copy(sl, k, r, d):
        return pltpu.make_async_copy(ys_ref.at[pl.ds(d, 1), :], rbuf.at[sl, k, pl.ds(r, 1), :], sem.at[sl])

    def issue(step, sl):
        def body(r, c):
            a = (step * tm + r) * TOP_K
            for k in range(TOP_K):
                row_copy(sl, k, r, dest_ref[a + k]).start()
            return c
        lax.fori_loop(0, tm, body, 0)

    @pl.when(i == 0)
    def _():
        issue(0, 0)

    @pl.when(i + 1 < n)
    def _():
        issue(i + 1, 1 - slot)

    def wait_body(r, c):
        for k in range(TOP_K):
            row_copy(slot, k, 0, 0).wait()
        return c
    lax.fori_loop(0, tm, wait_body, 0)

    info = info_ref[...]
    lane_i = lax.broadcasted_iota(jnp.int32, info.shape, 1)
    w1 = jnp.sum(jnp.where(lane_i == 4, info, 0.0), axis=-1, keepdims=True)
    w2 = jnp.sum(jnp.where(lane_i == 5, info, 0.0), axis=-1, keepdims=True)
    y = x_ref[...] + (rbuf[slot, 0] * w1 + rbuf[slot, 1] * w2)
    out = _rms(y, g_ref[...])

    @pl.when(i < n_prompt_tiles)
    def _():
        yp_ref[...] = out

    @pl.when(i >= n_prompt_tiles)
    def _():
        ys_out_ref[...] = out


def _combine(dest, x1, info, final_g, ys, *, tm, n_prompt, n_sample):
    n, d = x1.shape
    npt = n_prompt // tm
    grid_spec = pltpu.PrefetchScalarGridSpec(
        num_scalar_prefetch=1, grid=(n // tm,),
        in_specs=[
            pl.BlockSpec((tm, d), lambda i, ds: (i, 0)),
            pl.BlockSpec((tm, LANES), lambda i, ds: (i, 0)),
            pl.BlockSpec((1, d), lambda i, ds: (0, 0)),
            pl.BlockSpec(memory_space=pl.ANY),
        ],
        out_specs=[
            pl.BlockSpec((tm, d), lambda i, ds: (jnp.minimum(i, npt - 1), 0)),
            pl.BlockSpec((tm, d), lambda i, ds: (jnp.maximum(i - npt, 0), 0)),
        ],
        scratch_shapes=[pltpu.VMEM((2, TOP_K, tm, d), F32), pltpu.SemaphoreType.DMA((2,))],
    )
    return pl.pallas_call(
        functools.partial(_combine_kernel, tm=tm, n_prompt_tiles=npt), grid_spec=grid_spec,
        out_shape=[jax.ShapeDtypeStruct((n_prompt, d), F32), jax.ShapeDtypeStruct((n_sample, d), F32)],
        compiler_params=_cparams(("arbitrary",)),
        name="moe_combine",
    )(dest, x1, info, final_g, ys)


def _rope_tables(pos):
    half = ROPE_DIM // 2
    inv_freq = ROPE_BASE ** (-jnp.arange(half, dtype=F32) / half)
    ang = pos.astype(F32)[:, None] * inv_freq[None, :]
    cos, sin = jnp.cos(ang), jnp.sin(ang)
    z = jnp.zeros_like(cos)
    z2 = jnp.concatenate([z, z], axis=1)
    return (jnp.concatenate([cos, cos, z2], axis=1), jnp.concatenate([-sin, z, z2], axis=1),
            jnp.concatenate([z, sin, z2], axis=1))


def kernel(x_prompt, x_sample, cache_kv_latent, cache_k_rope, state_conv, page_table, attn_norm_g, w_in, q_norm_g,
           w_q_up, kv_norm_g, w_kv_up, conv_w, conv_b, conv_ln_g, conv_ln_b, w_out, ffn_norm_g, w_router_group,
           b_router_group, w_router_expert, b_router_expert, w_exp_gate, w_exp_up, w_exp_down, final_norm_g):
    batch, seq, d = x_prompt.shape
    nb, t_dec, _ = x_sample.shape
    depth, q_lora, n_heads, qk_dim = w_q_up.shape
    kv_lora = w_kv_up.shape[1]
    v_dim = w_kv_up.shape[3] - NOPE_DIM
    conv_ch = conv_w.shape[2]
    width = conv_w.shape[1]
    n_exp = w_exp_gate.shape[1]
    n_grp = w_router_group.shape[2]
    per_grp = n_exp // n_grp
    page = cache_kv_latent.shape[2]
    past = page_table.shape[1] * page
    assert depth == 1 and t_dec == 1 and qk_dim == NOPE_DIM + ROPE_DIM and v_dim == NOPE_DIM
    scale = float(qk_dim) ** -0.5
    n_p = batch * seq
    n_tok = n_p + nb

    wi = w_in[0]
    o_kv, o_kr, o_a = q_lora, q_lora + kv_lora, q_lora + kv_lora + ROPE_DIM
    w_in_p = jnp.concatenate(
        [wi[:, :o_kr], wi[:, o_a:], wi[:, o_kr:o_a], jnp.zeros((d, LANES - ROPE_DIM), F32)], axis=1).astype(BF16)
    wq_p = jnp.pad(w_q_up[0], ((0, 0), (0, 0), (0, HEAD_PAD - qk_dim))).reshape(q_lora, n_heads * HEAD_PAD)
    wq_p = wq_p.astype(BF16)
    wkv_p = w_kv_up[0].reshape(kv_lora, n_heads * HEAD_PAD).astype(BF16)
    w_out_b = w_out[0].astype(BF16)
    w_r = jnp.concatenate([w_router_expert[0], w_router_group[0],
                           jnp.zeros((d, LANES - n_exp - n_grp), F32)], axis=1)
    w_r_hi = w_r.astype(BF16)
    w_r_lo = (w_r - w_r_hi.astype(F32)).astype(BF16)
    b_r = jnp.concatenate([b_router_expert[0], b_router_group[0], jnp.zeros((LANES - n_exp - n_grp,), F32)])[None]
    row2 = lambda v: v.reshape(1, -1)

    tm_p = 256
    rc_p, ra_p, rb_p = _rope_tables(jnp.arange(seq))
    rc_s, ra_s, rb_s = _rope_tables(jnp.full((nb,), past, jnp.int32))
    dims = dict(n_heads=n_heads, q_lora=q_lora, kv_lora=kv_lora, conv_ch=conv_ch)
    xp2 = x_prompt.reshape(n_p, d)
    xs2 = x_sample.reshape(nb, d)
    q_p, c_p, kr_p, u_p, kv_p, krb_p = _in_proj(
        xp2, row2(attn_norm_g[0]), w_in_p, row2(q_norm_g[0]), wq_p, row2(kv_norm_g[0]), wkv_p, rc_p, ra_p, rb_p,
        tm=tm_p, prompt=True, **dims)
    qs_s, c_s, kr_s, u_s = _in_proj(
        xs2, row2(attn_norm_g[0]), w_in_p, row2(q_norm_g[0]), wq_p, row2(kv_norm_g[0]), wkv_p, rc_s, ra_s, rb_s,
        tm=nb, prompt=False, **dims)

    o_att_p = _attn_prompt(q_p, kv_p, krb_p, batch=batch, seq=seq, n_heads=n_heads, v_dim=v_dim, scale=scale)
    cw, cb, cg, cbeta = conv_w[0], row2(conv_b[0]), row2(conv_ln_g[0]), row2(conv_ln_b[0])
    o_conv_p = _conv_prompt(u_p, cw, cb, cg, cbeta, batch=batch, seq=seq)

    o_lat = _attn_sample(page_table, qs_s.reshape(nb, n_heads, QS_HEAD), c_s.reshape(nb, 1, kv_lora),
                         kr_s.reshape(nb, 1, ROPE_DIM), cache_kv_latent[0], cache_k_rope[0], scale=scale)
    o_att_s = _v_up(o_lat.reshape(nb, n_heads * kv_lora), wkv_p, n_heads=n_heads, kv_lora=kv_lora, v_dim=v_dim)
    o_conv_s = _conv_sample(state_conv[0], u_s, cw, cb, cg, cbeta)

    x1 = _out_proj(o_att_p, o_conv_p, w_out_b, xp2, tm=512, n_total=n_tok, row_off=0)
    x1 = _out_proj(o_att_s, o_conv_s, w_out_b, xs2, tm=nb, n_total=n_tok, row_off=n_p, prev=x1)

    info, counts = _router(x1, row2(ffn_norm_g[0]), w_r_hi, w_r_lo, b_r, tm=TOK_TILE, n_exp=n_exp, n_grp=n_grp,
                           per_grp=per_grp)
    tb = MOE_ROWS
    n_assign = n_tok * TOP_K
    n_blocks = -(-n_assign // tb) + n_exp
    cnt = counts[0, :n_exp].astype(jnp.int32)
    padded = (cnt + tb - 1) // tb * tb
    pad_end = jnp.cumsum(padded)
    pad_start = pad_end - padded
    e_idx = info[:, 0:TOP_K].astype(jnp.int32)
    rank = info[:, TOP_K:2 * TOP_K].astype(jnp.int32)
    dest = (pad_start[e_idx] + rank).reshape(-1)
    n_blk = jnp.maximum(pad_end[-1] // tb, 1).astype(jnp.int32)
    blk_start = jnp.minimum(jnp.arange(n_blocks, dtype=jnp.int32), n_blk - 1) * tb
    block_e = jnp.minimum(jnp.searchsorted(pad_end, blk_start, side='right'), n_exp - 1).astype(jnp.int32)

    xs = _dispatch(dest, x1, row2(ffn_norm_g[0]), jnp.zeros((n_blocks * tb, d), F32), tm=TOK_TILE)
    ys = _experts(block_e, n_blk.reshape(1), xs, w_exp_gate[0], w_exp_up[0], w_exp_down[0], tb=tb)
    y_p, y_s = _combine(dest, x1, info, row2(final_norm_g), ys, tm=CMB_TILE, n_prompt=n_p, n_sample=nb)

    hist = width - 1
    new_conv_p = u_p.reshape(batch, seq, conv_ch)[:, seq - hist:]
    new_conv_s = jnp.concatenate([state_conv[0][:, 1:], u_s[:, None, :]], axis=1)
    return (y_p.reshape(batch, seq, d), y_s.reshape(nb, t_dec, d),
            c_p.reshape(1, batch, seq, kv_lora), kr_p.reshape(1, batch, seq, ROPE_DIM), new_conv_p[None],
            c_s.reshape(1, nb, t_dec, kv_lora), kr_s.reshape(1, nb, t_dec, ROPE_DIM), new_conv_s[None])
```

```python
import functools

import jax
import jax.numpy as jnp
from jax import lax
from jax.experimental import pallas as pl
from jax.experimental.pallas import tpu as pltpu

F32 = jnp.float32
BF16 = jnp.bfloat16

EPS = 1e-6
NEG_INF = -1e30
ROPE_BASE = 10000.0

LANES = 128
SUBLANES = 8
VMEM_LIMIT = 56 * 1024 * 1024

NOPE_DIM = 128
ROPE_DIM = 64
TOP_K = 2
HEAD_PAD = 256
QS_HEAD = 640

MOE_SUB = 128
MOE_ITEM_SUBS = 4
TOK_TILE = 416
CMB_TILE = 128


def _cparams(sem, vmem=VMEM_LIMIT):
    return pltpu.CompilerParams(dimension_semantics=sem, vmem_limit_bytes=vmem)


def _rms(x, g):
    return x * lax.rsqrt(jnp.mean(x * x, axis=-1, keepdims=True) + EPS) * g


def _rope128(v, c, a, b):
    return v * c + pltpu.roll(v, 96, 1) * a + pltpu.roll(v, 32, 1) * b


def _sigmoid(x):
    return 1.0 / (1.0 + jnp.exp(-x))


def _nt_dot(a, b):
    return lax.dot_general(a, b, (((1,), (1,)), ((), ())), preferred_element_type=F32)


def _in_proj_kernel(*refs, n_heads, q_lora, kv_lora, conv_ch, prompt):
    (x_ref, g_ref, w_ref, qg_ref, wq_ref, kvg_ref, wkv_ref, rc_ref, ra_ref, rb_ref) = refs[:10]
    outs = refs[10:]
    if prompt:
        q_ref, c_ref, kr_ref, u_ref, kv_ref, krb_ref = outs
    else:
        q_ref, c_ref, kr_ref, u_ref = outs
    o_kv = q_lora
    o_a = q_lora + kv_lora
    o_g = o_a + conv_ch
    o_kr = o_g + conv_ch

    h = _rms(x_ref[...], g_ref[...]).astype(BF16)
    rc, ra, rb = rc_ref[...], ra_ref[...], rb_ref[...]

    zq = jnp.dot(h, w_ref[:, 0:q_lora], preferred_element_type=F32)
    qn = _rms(zq, qg_ref[...]).astype(BF16)
    qf = jnp.dot(qn, wq_ref[...], preferred_element_type=F32)

    zkv = jnp.dot(h, w_ref[:, o_kv:o_a], preferred_element_type=F32)
    c = _rms(zkv, kvg_ref[...])
    c_ref[...] = c

    zkr = jnp.dot(h, w_ref[:, o_kr:o_kr + LANES], preferred_element_type=F32)
    kr = _rope128(zkr, rc, ra, rb)
    kr_ref[...] = kr[:, :ROPE_DIM]

    a = jnp.dot(h, w_ref[:, o_a:o_g], preferred_element_type=F32)
    gate = jnp.dot(h, w_ref[:, o_g:o_kr], preferred_element_type=F32)
    u_ref[...] = a * _sigmoid(gate)

    if prompt:
        krb_ref[...] = kr.astype(BF16)
        kv_ref[...] = jnp.dot(c.astype(BF16), wkv_ref[...], preferred_element_type=F32).astype(BF16)
        for hd in range(n_heads):
            o = hd * HEAD_PAD
            q_ref[:, o:o + NOPE_DIM] = qf[:, o:o + NOPE_DIM].astype(BF16)
            q_ref[:, o + NOPE_DIM:o + HEAD_PAD] = _rope128(
                qf[:, o + NOPE_DIM:o + HEAD_PAD], rc, ra, rb).astype(BF16)
    else:
        for hd in range(n_heads):
            o = hd * HEAD_PAD
            qnope = qf[:, o:o + NOPE_DIM].astype(BF16)
            wkn = wkv_ref[:, o:o + NOPE_DIM]
            qo = hd * QS_HEAD
            q_ref[:, qo:qo + kv_lora] = _nt_dot(qnope, wkn).astype(BF16)
            q_ref[:, qo + kv_lora:qo + QS_HEAD] = _rope128(
                qf[:, o + NOPE_DIM:o + HEAD_PAD], rc, ra, rb).astype(BF16)


def _in_proj(x, attn_g, w_in_p, q_g, wq_p, kv_g, wkv_p, rc, ra, rb, *, tm, prompt, n_heads, q_lora, kv_lora,
             conv_ch):
    n, d = x.shape
    n_tab = rc.shape[0] // tm
    const = lambda i: (0, 0)
    row = lambda i: (i, 0)
    tab = lambda i: (i % n_tab, 0)
    single = pl.Buffered(1)
    in_specs = [
        pl.BlockSpec((tm, d), row),
        pl.BlockSpec((1, d), const),
        pl.BlockSpec(w_in_p.shape, const, pipeline_mode=single),
        pl.BlockSpec((1, q_lora), const),
        pl.BlockSpec(wq_p.shape, const, pipeline_mode=single),
        pl.BlockSpec((1, kv_lora), const),
        pl.BlockSpec(wkv_p.shape, const, pipeline_mode=single),
        pl.BlockSpec((tm, LANES), tab),
        pl.BlockSpec((tm, LANES), tab),
        pl.BlockSpec((tm, LANES), tab),
    ]
    q_cols = n_heads * (HEAD_PAD if prompt else QS_HEAD)
    out_shape = [
        jax.ShapeDtypeStruct((n, q_cols), BF16),
        jax.ShapeDtypeStruct((n, kv_lora), F32),
        jax.ShapeDtypeStruct((n, ROPE_DIM), F32),
        jax.ShapeDtypeStruct((n, conv_ch), F32),
    ]
    out_specs = [
        pl.BlockSpec((tm, q_cols), row),
        pl.BlockSpec((tm, kv_lora), row),
        pl.BlockSpec((tm, ROPE_DIM), row),
        pl.BlockSpec((tm, conv_ch), row),
    ]
    if prompt:
        out_shape += [jax.ShapeDtypeStruct((n, wkv_p.shape[1]), BF16), jax.ShapeDtypeStruct((n, LANES), BF16)]
        out_specs += [pl.BlockSpec((tm, wkv_p.shape[1]), row), pl.BlockSpec((tm, LANES), row)]
    kern = functools.partial(_in_proj_kernel, n_heads=n_heads, q_lora=q_lora, kv_lora=kv_lora, conv_ch=conv_ch,
                             prompt=prompt)
    return pl.pallas_call(
        kern, grid=(n // tm,), in_specs=in_specs, out_specs=out_specs, out_shape=out_shape,
        compiler_params=_cparams(("arbitrary",)),
        name="in_proj_prompt" if prompt else "in_proj_sample",
    )(x, attn_g, w_in_p, q_g, wq_p, kv_g, wkv_p, rc, ra, rb)


def _attn_prompt_kernel(q_ref, kv_ref, krb_ref, o_ref, kcat_ref, *, tq, tk, scale):
    i = pl.program_id(2)

    @pl.when(i == 0)
    def _():
        kcat_ref[:, :NOPE_DIM] = kv_ref[:, :NOPE_DIM]
        kcat_ref[:, NOPE_DIM:] = krb_ref[...]

    q = q_ref[...]
    v_dim = o_ref.shape[1]

    def scores(j):
        k = kcat_ref[pl.ds(pl.multiple_of(j * tk, tk), tk), :]
        return _nt_dot(q, k) * scale

    def update(j, s, carry):
        m, l, acc = carry
        m_new = jnp.maximum(m, jnp.max(s, axis=-1, keepdims=True))
        alpha = jnp.exp(m - m_new)
        p = jnp.exp(s - m_new)
        v = kv_ref[pl.ds(pl.multiple_of(j * tk, tk), tk), NOPE_DIM:]
        acc = acc * alpha + jnp.dot(p.astype(BF16), v, preferred_element_type=F32)
        return m_new, l * alpha + jnp.sum(p, axis=-1, keepdims=True), acc

    def body(j, carry):
        s_next = scores(j + 1)
        return update(j, carry[3], carry[:3]) + (s_next,)

    init = (jnp.full((tq, 1), NEG_INF, F32), jnp.zeros((tq, 1), F32), jnp.zeros((tq, v_dim), F32), scores(0))
    m, l, acc, s = lax.fori_loop(0, i, body, init)
    rowi = lax.broadcasted_iota(jnp.int32, (tq, tk), 0)
    coli = lax.broadcasted_iota(jnp.int32, (tq, tk), 1)
    m, l, acc = update(i, jnp.where(coli <= rowi, s, NEG_INF), (m, l, acc))
    o_ref[...] = (acc / l).astype(o_ref.dtype)


def _attn_prompt(q, kv, krb, *, batch, seq, n_heads, v_dim, scale, tq=512):
    n = q.shape[0]
    nq = seq // tq
    kern = functools.partial(_attn_prompt_kernel, tq=tq, tk=tq, scale=scale)
    return pl.pallas_call(
        kern, grid=(batch, n_heads, nq),
        in_specs=[
            pl.BlockSpec((tq, HEAD_PAD), lambda b, h, i: (b * nq + i, h)),
            pl.BlockSpec((seq, HEAD_PAD), lambda b, h, i: (b, h)),
            pl.BlockSpec((seq, LANES), lambda b, h, i: (b, 0)),
        ],
        out_specs=pl.BlockSpec((tq, v_dim), lambda b, h, i: (b * nq + i, h)),
        out_shape=jax.ShapeDtypeStruct((n, n_heads * v_dim), BF16),
        scratch_shapes=[pltpu.VMEM((seq, HEAD_PAD), BF16)],
        compiler_params=_cparams(("arbitrary", "arbitrary", "arbitrary")),
        name="attn_prompt",
    )(q, kv, krb)


def _ln_silu(y, g, b):
    yc = y - jnp.mean(y, axis=-1, keepdims=True)
    z = yc * lax.rsqrt(jnp.mean(yc * yc, axis=-1, keepdims=True) + EPS) * g + b
    return z * _sigmoid(z)


def _conv_prompt_kernel(halo_ref, u_ref, w_ref, b_ref, g_ref, beta_ref, o_ref, win_ref, y_ref, *, tt, halo,
                        width, rc):
    i = pl.program_id(1)

    @pl.when(i == 0)
    def _():
        win_ref[0:halo, :] = jnp.zeros((halo, win_ref.shape[1]), F32)

    @pl.when(i > 0)
    def _():
        win_ref[0:halo, :] = halo_ref[...]

    win_ref[halo:, :] = u_ref[...]
    base = halo - (width - 1)
    ch = win_ref.shape[1]
    for lc in range(ch // LANES):
        ls = slice(lc * LANES, (lc + 1) * LANES)
        wl = w_ref[:, ls]
        for r in range(tt // rc):
            acc = jnp.zeros((rc, LANES), F32)
            for k in range(width):
                s0 = r * rc + base + k
                acc = acc + win_ref[s0:s0 + rc, ls] * wl[k:k + 1, :]
            y_ref[r * rc:(r + 1) * rc, ls] = acc
    o_ref[...] = _ln_silu(y_ref[...] + b_ref[...], g_ref[...], beta_ref[...]).astype(o_ref.dtype)


def _conv_prompt(u, conv_w, conv_b, ln_g, ln_b, *, batch, seq, tt=256, halo=32, rc=64):
    n, ch = u.shape
    width = conv_w.shape[0]
    nt = seq // tt
    hb = tt // halo
    const = lambda b, i: (0, 0)
    kern = functools.partial(_conv_prompt_kernel, tt=tt, halo=halo, width=width, rc=rc)
    return pl.pallas_call(
        kern, grid=(batch, nt),
        in_specs=[
            pl.BlockSpec((halo, ch), lambda b, i: (jnp.maximum((b * nt + i) * hb - 1, 0), 0)),
            pl.BlockSpec((tt, ch), lambda b, i: (b * nt + i, 0)),
            pl.BlockSpec((width, ch), const),
            pl.BlockSpec((1, ch), const),
            pl.BlockSpec((1, ch), const),
            pl.BlockSpec((1, ch), const),
        ],
        out_specs=pl.BlockSpec((tt, ch), lambda b, i: (b * nt + i, 0)),
        out_shape=jax.ShapeDtypeStruct((n, ch), BF16),
        scratch_shapes=[pltpu.VMEM((halo + tt, ch), F32), pltpu.VMEM((tt, ch), F32)],
        compiler_params=_cparams(("arbitrary", "arbitrary")),
        name="conv_prompt",
    )(u, u, conv_w, conv_b, ln_g, ln_b)


def _conv_sample_kernel(st_ref, u_ref, w_ref, b_ref, g_ref, beta_ref, o_ref, *, width):
    w = w_ref[...]
    y = u_ref[...] * w[width - 1:width, :] + b_ref[...]
    for k in range(width - 1):
        y = y + st_ref[k] * w[k:k + 1, :]
    o_ref[...] = _ln_silu(y, g_ref[...], beta_ref[...]).astype(o_ref.dtype)


def _conv_sample(state_t, u, conv_w, conv_b, ln_g, ln_b, *, tb=32):
    hist, nb, ch = state_t.shape
    width = conv_w.shape[0]
    const = lambda i: (0, 0)
    return pl.pallas_call(
        functools.partial(_conv_sample_kernel, width=width), grid=(nb // tb,),
        in_specs=[
            pl.BlockSpec((hist, tb, ch), lambda i: (0, i, 0)),
            pl.BlockSpec((tb, ch), lambda i: (i, 0)),
            pl.BlockSpec((width, ch), const),
            pl.BlockSpec((1, ch), const),
            pl.BlockSpec((1, ch), const),
            pl.BlockSpec((1, ch), const),
        ],
        out_specs=pl.BlockSpec((tb, ch), lambda i: (i, 0)),
        out_shape=jax.ShapeDtypeStruct((nb, ch), BF16),
        compiler_params=_cparams(("arbitrary",)),
        name="conv_sample",
    )(state_t, u, conv_w, conv_b, ln_g, ln_b)


def _attn_sample_kernel(pt_ref, qs_ref, cn_ref, krn_ref, cache_c, cache_krt, o_ref, cbuf, rbuf, sem, m_ref, l_ref,
                        acc_ref, *, pages, n_chunks, page, kv_lora, sub, scale):
    b = pl.program_id(0)
    ch = pl.program_id(1)
    g = b * n_chunks + ch
    n_steps = pl.num_programs(0) * n_chunks
    slot = g % 2

    def copies(step, sl):
        out = []
        for p in range(pages):
            pid = pt_ref[step * pages + p]
            keys = pl.ds(p * page, page)
            out.append(pltpu.make_async_copy(cache_c.at[pid], cbuf.at[sl, keys, :], sem.at[0, sl]))
            out.append(pltpu.make_async_copy(cache_krt.at[pid], rbuf.at[sl, :, keys], sem.at[1, sl]))
        return out

    @pl.when(g == 0)
    def _():
        for cp in copies(0, 0):
            cp.start()

    @pl.when(g + 1 < n_steps)
    def _():
        for cp in copies(g + 1, 1 - slot):
            cp.start()

    @pl.when(ch == 0)
    def _():
        m_ref[...] = jnp.full(m_ref.shape, NEG_INF, F32)
        l_ref[...] = jnp.zeros(l_ref.shape, F32)
        acc_ref[...] = jnp.zeros(acc_ref.shape, F32)

    for cp in copies(g, slot):
        cp.wait()

    q = qs_ref[0]
    ql = q[:, :kv_lora]
    qr = q[:, kv_lora:kv_lora + ROPE_DIM]
    cbs, scores = [], []
    for k0 in range(0, pages * page, sub):
        cb = cbuf[slot, k0:k0 + sub, :].astype(BF16)
        rb = rbuf[slot, :, k0:k0 + sub].astype(BF16)
        cbs.append(cb)
        scores.append((_nt_dot(ql, cb) + jnp.dot(qr, rb, preferred_element_type=F32)) * scale)
    probs = []
    for s in scores:
        m_k = jnp.max(s, axis=-1, keepdims=True)
        p = jnp.exp(s - m_k)
        probs.append((m_k, jnp.sum(p, axis=-1, keepdims=True), p.astype(BF16)))
    parts = [(m_k, l_k, jnp.dot(p, cb, preferred_element_type=F32)) for (m_k, l_k, p), cb in zip(probs, cbs)]
    m_old = m_ref[...]
    m_new = m_old
    for m_k, _, _ in parts:
        m_new = jnp.maximum(m_new, m_k)
    alpha = jnp.exp(m_old - m_new)
    l_new = l_ref[...] * alpha
    acc = acc_ref[...] * alpha
    for m_k, l_k, a_k in parts:
        w_k = jnp.exp(m_k - m_new)
        l_new = l_new + l_k * w_k
        acc = acc + a_k * w_k
    l_ref[...] = l_new
    acc_ref[...] = acc
    m_ref[...] = m_new

    @pl.when(ch == n_chunks - 1)
    def _():
        cn = cn_ref[0].astype(BF16).astype(F32)
        krn = krn_ref[0].astype(BF16).astype(F32)
        s_new = (jnp.sum(ql.astype(F32) * cn, axis=-1, keepdims=True)
                 + jnp.sum(qr.astype(F32) * krn, axis=-1, keepdims=True)) * scale
        m_o = m_ref[...]
        m_n = jnp.maximum(m_o, s_new)
        al = jnp.exp(m_o - m_n)
        p_new = jnp.exp(s_new - m_n)
        l_fin = l_ref[...] * al + p_new
        acc = acc_ref[...] * al + p_new.astype(BF16).astype(F32) * cn
        o_ref[0] = acc / l_fin


def _attn_sample(page_table, qs3, c_new3, kr_new3, cache_c, cache_krt, *, scale, pages=32, sub=512):
    nb, n_heads, qw = qs3.shape
    n_pages = page_table.shape[1]
    page, kv_lora = cache_c.shape[1:]
    rope = cache_krt.shape[1]
    n_chunks = n_pages // pages
    keys = pages * page
    kern = functools.partial(_attn_sample_kernel, pages=pages, n_chunks=n_chunks, page=page, kv_lora=kv_lora,
                             sub=sub, scale=scale)
    grid_spec = pltpu.PrefetchScalarGridSpec(
        num_scalar_prefetch=1, grid=(nb, n_chunks),
        in_specs=[
            pl.BlockSpec((1, n_heads, qw), lambda b, c, pt: (b, 0, 0)),
            pl.BlockSpec((1, 1, kv_lora), lambda b, c, pt: (b, 0, 0)),
            pl.BlockSpec((1, 1, rope), lambda b, c, pt: (b, 0, 0)),
            pl.BlockSpec(memory_space=pl.ANY),
            pl.BlockSpec(memory_space=pl.ANY),
        ],
        out_specs=pl.BlockSpec((1, n_heads, kv_lora), lambda b, c, pt: (b, 0, 0)),
        scratch_shapes=[
            pltpu.VMEM((2, keys, kv_lora), F32),
            pltpu.VMEM((2, rope, keys), F32),
            pltpu.SemaphoreType.DMA((2, 2)),
            pltpu.VMEM((n_heads, 1), F32),
            pltpu.VMEM((n_heads, 1), F32),
            pltpu.VMEM((n_heads, kv_lora), F32),
        ],
    )
    return pl.pallas_call(
        kern, grid_spec=grid_spec,
        out_shape=jax.ShapeDtypeStruct((nb, n_heads, kv_lora), F32),
        compiler_params=_cparams(("arbitrary", "arbitrary")),
        name="attn_sample",
    )(page_table.reshape(-1), qs3, c_new3, kr_new3, cache_c, cache_krt)


def _v_up_kernel(ol_ref, wkv_ref, o_ref, *, n_heads, kv_lora, v_dim):
    for hd in range(n_heads):
        ol = ol_ref[:, hd * kv_lora:(hd + 1) * kv_lora].astype(BF16)
        wv = wkv_ref[:, hd * HEAD_PAD + NOPE_DIM:(hd + 1) * HEAD_PAD]
        o_ref[:, hd * v_dim:(hd + 1) * v_dim] = jnp.dot(ol, wv, preferred_element_type=F32).astype(o_ref.dtype)


def _v_up(o_lat2, wkv_p, *, n_heads, kv_lora, v_dim):
    nb = o_lat2.shape[0]
    return pl.pallas_call(
        functools.partial(_v_up_kernel, n_heads=n_heads, kv_lora=kv_lora, v_dim=v_dim),
        out_shape=jax.ShapeDtypeStruct((nb, n_heads * v_dim), BF16),
        compiler_params=_cparams(None),
        name="v_up_sample",
    )(o_lat2, wkv_p)


def _out_proj_kernel(ap_ref, bp_ref, xp_ref, as_ref, bs_ref, xs_ref, w_ref, o_ref, *, half, n_prompt_tiles):
    i = pl.program_id(0)

    def mix(a_ref, b_ref, x_ref):
        return (x_ref[...] + jnp.dot(a_ref[...], w_ref[0:half, :], preferred_element_type=F32)
                + jnp.dot(b_ref[...], w_ref[half:, :], preferred_element_type=F32))

    @pl.when(i < n_prompt_tiles)
    def _():
        o_ref[...] = mix(ap_ref, bp_ref, xp_ref)

    @pl.when(i == n_prompt_tiles)
    def _():
        o_ref[0:as_ref.shape[0], :] = mix(as_ref, bs_ref, xs_ref)


def _out_proj(a_p, b_p, x_p, a_s, b_s, x_s, w, *, tm):
    n_p, half = a_p.shape
    n_s = a_s.shape[0]
    d = w.shape[1]
    npt = n_p // tm
    assert n_s <= tm
    const = lambda i: (0, 0)
    prow = lambda i: (jnp.minimum(i, npt - 1), 0)
    return pl.pallas_call(
        functools.partial(_out_proj_kernel, half=half, n_prompt_tiles=npt), grid=(npt + 1,),
        in_specs=[
            pl.BlockSpec((tm, half), prow),
            pl.BlockSpec((tm, half), prow),
            pl.BlockSpec((tm, d), prow),
            pl.BlockSpec((n_s, half), const),
            pl.BlockSpec((n_s, half), const),
            pl.BlockSpec((n_s, d), const),
            pl.BlockSpec(w.shape, const, pipeline_mode=pl.Buffered(1)),
        ],
        out_specs=pl.BlockSpec((tm, d), lambda i: (i, 0)),
        out_shape=jax.ShapeDtypeStruct((n_p + n_s, d), F32),
        compiler_params=_cparams(("arbitrary",)),
        name="out_proj",
    )(a_p, b_p, x_p, a_s, b_s, x_s, w)


def _router_kernel(x_ref, g_ref, whi_ref, wlo_ref, bias_ref, info_ref, cnt_ref, run_ref, *, n_exp, n_grp, per_grp):
    i = pl.program_id(0)

    @pl.when(i == 0)
    def _():
        run_ref[...] = jnp.zeros(run_ref.shape, F32)

    tm = x_ref.shape[0]
    h = _rms(x_ref[...], g_ref[...])
    h_hi = h.astype(BF16)
    h_lo = (h - h_hi.astype(F32)).astype(BF16)
    whi = whi_ref[...]
    logits = (jnp.dot(h_hi, whi, preferred_element_type=F32) + jnp.dot(h_lo, whi, preferred_element_type=F32)
              + jnp.dot(h_hi, wlo_ref[...], preferred_element_type=F32) + bias_ref[...])
    lane_i = lax.broadcasted_iota(jnp.int32, (tm, LANES), 1)
    lane = lane_i.astype(F32)
    big = float(LANES)

    def first_max(vals):
        mx = jnp.max(vals, axis=-1, keepdims=True)
        idx = jnp.min(jnp.where(vals == mx, lane, big), axis=-1, keepdims=True)
        return mx, idx

    is_grp = (lane_i >= n_exp) & (lane_i < n_exp + n_grp)
    lg = jnp.where(is_grp, logits, NEG_INF)
    g_max, g_lane = first_max(lg)
    g_p = 1.0 / jnp.sum(jnp.where(is_grp, jnp.exp(lg - g_max), 0.0), axis=-1, keepdims=True)
    g_idx = g_lane - float(n_exp)
    lo = g_idx * float(per_grp)
    in_grp = (lane >= lo) & (lane < lo + float(per_grp))
    le = jnp.where(in_grp, logits, NEG_INF)
    m1, i1 = first_max(le)
    m2, i2 = first_max(jnp.where(lane == i1, NEG_INF, le))
    e2 = jnp.exp(m2 - m1)
    w1 = g_p / (1.0 + e2)
    w2 = g_p * e2 / (1.0 + e2)

    oh1 = lane == i1
    oh2 = lane == i2
    onehot = jnp.where(oh1 | oh2, 1.0, 0.0)
    r_i = lax.broadcasted_iota(jnp.int32, (tm, tm), 0)
    c_i = lax.broadcasted_iota(jnp.int32, (tm, tm), 1)
    lower = jnp.where(c_i < r_i, 1.0, 0.0).astype(BF16)
    before = run_ref[...] + jnp.dot(lower, onehot.astype(BF16), preferred_element_type=F32)
    rank1 = jnp.sum(jnp.where(oh1, before, 0.0), axis=-1, keepdims=True)
    rank2 = jnp.sum(jnp.where(oh2, before, 0.0), axis=-1, keepdims=True)
    run_ref[...] = run_ref[...] + jnp.sum(onehot, axis=0, keepdims=True)
    cnt_ref[...] = run_ref[...]

    info = jnp.where(lane_i == 0, i1, jnp.where(lane_i == 1, i2, jnp.where(lane_i == 2, rank1, jnp.where(
        lane_i == 3, rank2, jnp.where(lane_i == 4, w1, jnp.where(lane_i == 5, w2, 0.0))))))
    info_ref[...] = info


def _router(x1, ffn_g, w_hi, w_lo, bias, *, tm, n_exp, n_grp, per_grp):
    n, d = x1.shape
    const = lambda i: (0, 0)
    return pl.pallas_call(
        functools.partial(_router_kernel, n_exp=n_exp, n_grp=n_grp, per_grp=per_grp), grid=(n // tm,),
        in_specs=[
            pl.BlockSpec((tm, d), lambda i: (i, 0)),
            pl.BlockSpec((1, d), const),
            pl.BlockSpec((d, LANES), const),
            pl.BlockSpec((d, LANES), const),
            pl.BlockSpec((1, LANES), const),
        ],
        out_specs=[pl.BlockSpec((tm, LANES), lambda i: (i, 0)), pl.BlockSpec((1, LANES), const)],
        out_shape=[jax.ShapeDtypeStruct((n, LANES), F32), jax.ShapeDtypeStruct((1, LANES), F32)],
        scratch_shapes=[pltpu.VMEM((1, LANES), F32)],
        compiler_params=_cparams(("arbitrary",)),
        name="router",
    )(x1, ffn_g, w_hi, w_lo, bias)


def _pow2_pieces(limit):
    p = limit // 2
    while p >= 1:
        yield p
        p //= 2


def _dispatch_kernel(dest_ref, zrow_ref, zcnt_ref, pend_ref, x_ref, g_ref, xs_ref, hbuf, zbuf, sem, zsem, *, tm, sub):
    i = pl.program_id(0)
    n = pl.num_programs(0)
    slot = i % 2
    n_exp = zrow_ref.shape[0]
    n_sub_blocks = xs_ref.shape[0] // sub

    def row_copy(sl, r, d):
        return pltpu.make_async_copy(hbuf.at[sl, pl.ds(r, 1), :], xs_ref.at[pl.ds(d, 1), :], sem.at[sl])

    def wait_all(sl):
        def body(r, c):
            for _ in range(TOP_K):
                row_copy(sl, 0, 0).wait()
            return c
        lax.fori_loop(0, tm, body, 0)

    def zero_copy(row, size):
        return pltpu.make_async_copy(zbuf.at[pl.ds(0, size), :], xs_ref.at[pl.ds(row, size), :], zsem)

    def zero_fill(start):
        def per_expert(e, c):
            row = zrow_ref[e]
            cnt = zcnt_ref[e]
            head = (-row) & (SUBLANES - 1)
            for r in range(SUBLANES - 1):
                @pl.when(r < jnp.minimum(head, cnt))
                def _():
                    cp = zero_copy(row + r, 1)
                    cp.start() if start else cp.wait()
            rest = jnp.maximum(cnt - head, 0)
            row = pl.multiple_of(row + head, SUBLANES)
            for piece in _pow2_pieces(sub):
                if piece < SUBLANES:
                    break
                @pl.when((rest & piece) != 0)
                def _():
                    cp = zero_copy(row, piece)
                    cp.start() if start else cp.wait()
                row = pl.multiple_of(row + (rest & piece), SUBLANES)
            return c
        lax.fori_loop(0, n_exp, per_expert, 0)

        def per_block(b, c):
            @pl.when(b * sub >= pend_ref[0])
            def _():
                cp = zero_copy(b * sub, sub)
                cp.start() if start else cp.wait()
            return c
        lax.fori_loop(0, n_sub_blocks, per_block, 0)

    @pl.when(i == 0)
    def _():
        zbuf[...] = jnp.zeros(zbuf.shape, F32)
        zero_fill(True)

    hbuf[slot] = _rms(x_ref[...], g_ref[...])

    def issue(r, c):
        a = (i * tm + r) * TOP_K
        for k in range(TOP_K):
            row_copy(slot, r, dest_ref[a + k]).start()
        return c
    lax.fori_loop(0, tm, issue, 0)

    @pl.when(i == 0)
    def _():
        zero_fill(False)

    @pl.when(i > 0)
    def _():
        wait_all(1 - slot)

    @pl.when(i == n - 1)
    def _():
        wait_all(slot)


def _dispatch(dest, zrow, zcnt, pend, x1, ffn_g, *, tm, rows, sub):
    n, d = x1.shape
    grid_spec = pltpu.PrefetchScalarGridSpec(
        num_scalar_prefetch=4, grid=(n // tm,),
        in_specs=[
            pl.BlockSpec((tm, d), lambda i, *_: (i, 0)),
            pl.BlockSpec((1, d), lambda i, *_: (0, 0)),
        ],
        out_specs=pl.BlockSpec(memory_space=pl.ANY),
        scratch_shapes=[pltpu.VMEM((2, tm, d), F32), pltpu.VMEM((sub, d), F32), pltpu.SemaphoreType.DMA((2,)),
                        pltpu.SemaphoreType.DMA(())],
    )
    return pl.pallas_call(
        functools.partial(_dispatch_kernel, tm=tm, sub=sub), grid_spec=grid_spec,
        out_shape=jax.ShapeDtypeStruct((rows, d), F32),
        compiler_params=_cparams(("arbitrary",)),
        name="moe_dispatch",
    )(dest, zrow, zcnt, pend, x1, ffn_g)


def _experts_kernel(ie_ref, ir_ref, ins_ref, nit_ref, pend_ref, xs_ref, wg_hbm, wu_hbm, wd_hbm, ys_ref,
                    gu_ring, d_ring, xbuf, gacc, uacc, hbuf, ybuf, zbuf,
                    gu_sem, d_sem, x_sem, y_sem, z_sem, *, sub, n_sub_max, kc, fcs):
    n_items = nit_ref[0]
    d = xs_ref.shape[1]
    n_kc = d // kc
    n_fc = len(fcs)
    assert n_fc == 2 and n_kc >= 2
    item_rows = sub * n_sub_max
    f_offs = [sum(fcs[:j]) for j in range(n_fc)]

    def gu_copies(item, pos, slot):
        e = ie_ref[item]
        rows = pl.ds(pos * kc, kc)
        return (pltpu.make_async_copy(wg_hbm.at[e, rows, :], gu_ring.at[slot, 0], gu_sem.at[slot]),
                pltpu.make_async_copy(wu_hbm.at[e, rows, :], gu_ring.at[slot, 1], gu_sem.at[slot]))

    def d_copy(item, j):
        e = ie_ref[item]
        return pltpu.make_async_copy(wd_hbm.at[e, pl.ds(f_offs[j], fcs[j]), :], d_ring.at[j, pl.ds(0, fcs[j]), :],
                                     d_sem.at[j])

    def x_copies(item, slot):
        r = pl.multiple_of(ir_ref[item], sub)
        return [pltpu.make_async_copy(xs_ref.at[pl.ds(r, item_rows), pl.ds(k * kc, kc)], xbuf.at[slot, k],
                                      x_sem.at[slot]) for k in range(n_kc)]

    def y_copy(item, sb):
        r = pl.multiple_of(ir_ref[item] + sb * sub, sub)
        return pltpu.make_async_copy(ybuf.at[pl.ds(sb * sub, sub), :], ys_ref.at[pl.ds(r, sub), :], y_sem)

    def for_rows(n_sub, fn):
        for ns in range(1, n_sub_max + 1):
            @pl.when(n_sub == ns)
            def _():
                fn(ns * sub)

    def y_writes(item, start):
        n_sub = ins_ref[item]
        for sb in range(n_sub_max):
            @pl.when(sb < n_sub)
            def _():
                cp = y_copy(item, sb)
                cp.start() if start else cp.wait()

    zbuf[...] = jnp.zeros(zbuf.shape, F32)
    n_out_blocks = ys_ref.shape[0] // sub

    def tail(start):
        def body(b, c):
            @pl.when(b * sub >= pend_ref[0])
            def _():
                cp = pltpu.make_async_copy(zbuf, ys_ref.at[pl.ds(b * sub, sub), :], z_sem)
                cp.start() if start else cp.wait()
            return c
        lax.fori_loop(0, n_out_blocks, body, 0)
    tail(True)

    @pl.when(n_items > 0)
    def _():
        for cp in x_copies(0, 0):
            cp.start()
        for pos in range(2):
            for cp in gu_copies(0, pos, pos):
                cp.start()

    def item_body(item, carry):
        xslot = item % 2
        n_sub = ins_ref[item]
        has_next = item + 1 < n_items

        @pl.when(has_next)
        def _():
            for cp in x_copies(item + 1, 1 - xslot):
                cp.start()

        for cp in x_copies(item, xslot):
            cp.wait()

        def zero_acc(rows):
            gacc[0:rows, :] = jnp.zeros((rows, gacc.shape[1]), F32)
            uacc[0:rows, :] = jnp.zeros((rows, uacc.shape[1]), F32)
        for_rows(n_sub, zero_acc)

        def gu_step(pos, c):
            slot = pos % 2
            for cp in gu_copies(item, pos, slot):
                cp.wait()

            def mm(rows):
                xk = xbuf[xslot, pos, 0:rows, :].astype(BF16)
                gacc[0:rows, :] += jnp.dot(xk, gu_ring[slot, 0].astype(BF16), preferred_element_type=F32)
                uacc[0:rows, :] += jnp.dot(xk, gu_ring[slot, 1].astype(BF16), preferred_element_type=F32)
            for_rows(n_sub, mm)

            @pl.when(pos + 2 < n_kc)
            def _():
                for cp in gu_copies(item, pos + 2, slot):
                    cp.start()
            for j in range(n_fc):
                @pl.when(pos + 2 == n_kc + j)
                def _():
                    d_copy(item, j).start()
            return c
        lax.fori_loop(0, n_kc, gu_step, 0)

        def act(rows):
            g = gacc[0:rows, :]
            hbuf[0:rows, :] = (g * _sigmoid(g) * uacc[0:rows, :]).astype(BF16)
        for_rows(n_sub, act)

        @pl.when(item > 0)
        def _():
            y_writes(item - 1, False)

        for j in range(n_fc):
            d_copy(item, j).wait()

            def mm_down(rows, j=j):
                y = jnp.dot(hbuf[0:rows, f_offs[j]:f_offs[j] + fcs[j]], d_ring[j, 0:fcs[j], :].astype(BF16),
                            preferred_element_type=F32)
                if j == 0:
                    ybuf[0:rows, :] = y
                else:
                    ybuf[0:rows, :] += y
            for_rows(n_sub, mm_down)

            @pl.when(has_next)
            def _():
                for cp in gu_copies(item + 1, j, j):
                    cp.start()
        y_writes(item, True)
        return carry

    lax.fori_loop(0, n_items, item_body, 0)

    @pl.when(n_items > 0)
    def _():
        y_writes(n_items - 1, False)
    tail(False)


def _experts(item_e, item_row, item_nsub, n_items, pend, xs, w_gate, w_up, w_down, *, sub, n_sub_max):
    rows, d = xs.shape
    f = w_gate.shape[2]
    kc = 512
    lanes_f = f // LANES
    fcs = ((lanes_f + 1) // 2 * LANES, lanes_f // 2 * LANES)
    item_rows = sub * n_sub_max
    any_spec = pl.BlockSpec(memory_space=pl.ANY)
    kern = functools.partial(_experts_kernel, sub=sub, n_sub_max=n_sub_max, kc=kc, fcs=fcs)
    return pl.pallas_call(
        kern,
        grid_spec=pltpu.PrefetchScalarGridSpec(
            num_scalar_prefetch=5, grid=(1,),
            in_specs=[any_spec, any_spec, any_spec, any_spec],
            out_specs=any_spec,
            scratch_shapes=[
                pltpu.VMEM((2, 2, kc, f), F32),
                pltpu.VMEM((2, fcs[0], d), F32),
                pltpu.VMEM((2, d // kc, item_rows, kc), F32),
                pltpu.VMEM((item_rows, f), F32),
                pltpu.VMEM((item_rows, f), F32),
                pltpu.VMEM((item_rows, f), BF16),
                pltpu.VMEM((item_rows, d), F32),
                pltpu.VMEM((sub, d), F32),
                pltpu.SemaphoreType.DMA((2,)),
                pltpu.SemaphoreType.DMA((2,)),
                pltpu.SemaphoreType.DMA((2,)),
                pltpu.SemaphoreType.DMA(()),
                pltpu.SemaphoreType.DMA(()),
            ],
        ),
        out_shape=jax.ShapeDtypeStruct((rows, d), F32),
        compiler_params=_cparams(("arbitrary",), 60 * 1024 * 1024),
        name="moe_experts",
    )(item_e, item_row, item_nsub, n_items, pend, xs, w_gate, w_up, w_down)


def _combine_kernel(dest_ref, x_ref, info_ref, g_ref, ys_ref, yp_ref, ys_out_ref, rbuf, sem, *, tm, n_prompt_tiles):
    i = pl.program_id(0)
    n = pl.num_programs(0)
    slot = i % 2

    def row_copy(sl, k, r, d):
        return pltpu.make_async_copy(ys_ref.at[pl.ds(d, 1), :], rbuf.at[sl, k, pl.ds(r, 1), :], sem.at[sl])

    def issue(step, sl):
        def body(r, c):
            a = (step * tm + r) * TOP_K
            for k in range(TOP_K):
                row_copy(sl, k, r, dest_ref[a + k]).start()
            return c
        lax.fori_loop(0, tm, body, 0)

    @pl.when(i == 0)
    def _():
        issue(0, 0)

    @pl.when(i + 1 < n)
    def _():
        issue(i + 1, 1 - slot)

    def wait_body(r, c):
        for k in range(TOP_K):
            row_copy(slot, k, 0, 0).wait()
        return c
    lax.fori_loop(0, tm, wait_body, 0)

    info = info_ref[...]
    lane_i = lax.broadcasted_iota(jnp.int32, info.shape, 1)
    w1 = jnp.sum(jnp.where(lane_i == 4, info, 0.0), axis=-1, keepdims=True)
    w2 = jnp.sum(jnp.where(lane_i == 5, info, 0.0), axis=-1, keepdims=True)
    y = x_ref[...] + (rbuf[slot, 0] * w1 + rbuf[slot, 1] * w2)
    out = _rms(y, g_ref[...])

    @pl.when(i < n_prompt_tiles)
    def _():
        yp_ref[...] = out

    @pl.when(i >= n_prompt_tiles)
    def _():
        ys_out_ref[...] = out


def _combine(dest, x1, info, final_g, ys, *, tm, n_prompt, n_sample):
    n, d = x1.shape
    npt = n_prompt // tm
    grid_spec = pltpu.PrefetchScalarGridSpec(
        num_scalar_prefetch=1, grid=(n // tm,),
        in_specs=[
            pl.BlockSpec((tm, d), lambda i, ds: (i, 0)),
            pl.BlockSpec((tm, LANES), lambda i, ds: (i, 0)),
            pl.BlockSpec((1, d), lambda i, ds: (0, 0)),
            pl.BlockSpec(memory_space=pl.ANY),
        ],
        out_specs=[
            pl.BlockSpec((tm, d), lambda i, ds: (jnp.minimum(i, npt - 1), 0)),
            pl.BlockSpec((tm, d), lambda i, ds: (jnp.maximum(i - npt, 0), 0)),
        ],
        scratch_shapes=[pltpu.VMEM((2, TOP_K, tm, d), F32), pltpu.SemaphoreType.DMA((2,))],
    )
    return pl.pallas_call(
        functools.partial(_combine_kernel, tm=tm, n_prompt_tiles=npt), grid_spec=grid_spec,
        out_shape=[jax.ShapeDtypeStruct((n_prompt, d), F32), jax.ShapeDtypeStruct((n_sample, d), F32)],
        compiler_params=_cparams(("arbitrary",)),
        name="moe_combine",
    )(dest, x1, info, final_g, ys)


def _rope_tables(pos):
    half = ROPE_DIM // 2
    inv_freq = ROPE_BASE ** (-jnp.arange(half, dtype=F32) / half)
    ang = pos.astype(F32)[:, None] * inv_freq[None, :]
    cos, sin = jnp.cos(ang), jnp.sin(ang)
    z = jnp.zeros_like(cos)
    z2 = jnp.concatenate([z, z], axis=1)
    return (jnp.concatenate([cos, cos, z2], axis=1), jnp.concatenate([-sin, z, z2], axis=1),
            jnp.concatenate([z, sin, z2], axis=1))


def kernel(x_prompt, x_sample, cache_kv_latent, cache_k_rope, state_conv, page_table, attn_norm_g, w_in, q_norm_g,
           w_q_up, kv_norm_g, w_kv_up, conv_w, conv_b, conv_ln_g, conv_ln_b, w_out, ffn_norm_g, w_router_group,
           b_router_group, w_router_expert, b_router_expert, w_exp_gate, w_exp_up, w_exp_down, final_norm_g):
    batch, seq, d = x_prompt.shape
    nb, t_dec, _ = x_sample.shape
    depth, q_lora, n_heads, qk_dim = w_q_up.shape
    kv_lora = w_kv_up.shape[1]
    v_dim = w_kv_up.shape[3] - NOPE_DIM
    conv_ch = conv_w.shape[2]
    width = conv_w.shape[1]
    n_exp = w_exp_gate.shape[1]
    n_grp = w_router_group.shape[2]
    per_grp = n_exp // n_grp
    page = cache_kv_latent.shape[2]
    past = page_table.shape[1] * page
    assert depth == 1 and t_dec == 1 and qk_dim == NOPE_DIM + ROPE_DIM and v_dim == NOPE_DIM
    scale = float(qk_dim) ** -0.5
    n_p = batch * seq
    n_tok = n_p + nb

    wi = w_in[0]
    o_kv, o_kr, o_a = q_lora, q_lora + kv_lora, q_lora + kv_lora + ROPE_DIM
    w_in_p = jnp.concatenate(
        [wi[:, :o_kr], wi[:, o_a:], wi[:, o_kr:o_a], jnp.zeros((d, LANES - ROPE_DIM), F32)], axis=1).astype(BF16)
    wq_p = jnp.pad(w_q_up[0], ((0, 0), (0, 0), (0, HEAD_PAD - qk_dim))).reshape(q_lora, n_heads * HEAD_PAD)
    wq_p = wq_p.astype(BF16)
    wkv_p = w_kv_up[0].reshape(kv_lora, n_heads * HEAD_PAD).astype(BF16)
    w_out_b = w_out[0].astype(BF16)
    w_r = jnp.concatenate([w_router_expert[0], w_router_group[0],
                           jnp.zeros((d, LANES - n_exp - n_grp), F32)], axis=1)
    w_r_hi = w_r.astype(BF16)
    w_r_lo = (w_r - w_r_hi.astype(F32)).astype(BF16)
    b_r = jnp.concatenate([b_router_expert[0], b_router_group[0], jnp.zeros((LANES - n_exp - n_grp,), F32)])[None]
    row2 = lambda v: v.reshape(1, -1)

    tm_p = 256
    rc_p, ra_p, rb_p = _rope_tables(jnp.arange(seq))
    rc_s, ra_s, rb_s = _rope_tables(jnp.full((nb,), past, jnp.int32))
    dims = dict(n_heads=n_heads, q_lora=q_lora, kv_lora=kv_lora, conv_ch=conv_ch)
    xp2 = x_prompt.reshape(n_p, d)
    xs2 = x_sample.reshape(nb, d)
    q_p, c_p, kr_p, u_p, kv_p, krb_p = _in_proj(
        xp2, row2(attn_norm_g[0]), w_in_p, row2(q_norm_g[0]), wq_p, row2(kv_norm_g[0]), wkv_p, rc_p, ra_p, rb_p,
        tm=tm_p, prompt=True, **dims)
    qs_s, c_s, kr_s, u_s = _in_proj(
        xs2, row2(attn_norm_g[0]), w_in_p, row2(q_norm_g[0]), wq_p, row2(kv_norm_g[0]), wkv_p, rc_s, ra_s, rb_s,
        tm=nb, prompt=False, **dims)

    o_att_p = _attn_prompt(q_p, kv_p, krb_p, batch=batch, seq=seq, n_heads=n_heads, v_dim=v_dim, scale=scale)
    cw, cb, cg, cbeta = conv_w[0], row2(conv_b[0]), row2(conv_ln_g[0]), row2(conv_ln_b[0])
    o_conv_p = _conv_prompt(u_p, cw, cb, cg, cbeta, batch=batch, seq=seq)

    o_lat = _attn_sample(page_table, qs_s.reshape(nb, n_heads, QS_HEAD), c_s.reshape(nb, 1, kv_lora),
                         kr_s.reshape(nb, 1, ROPE_DIM), cache_kv_latent[0],
                         jnp.transpose(cache_k_rope[0], (0, 2, 1)), scale=scale)
    o_att_s = _v_up(o_lat.reshape(nb, n_heads * kv_lora), wkv_p, n_heads=n_heads, kv_lora=kv_lora, v_dim=v_dim)
    state_t = jnp.transpose(state_conv[0], (1, 0, 2))
    o_conv_s = _conv_sample(state_t, u_s, cw, cb, cg, cbeta)

    x1 = _out_proj(o_att_p, o_conv_p, xp2, o_att_s, o_conv_s, xs2, w_out_b, tm=512)

    info, counts = _router(x1, row2(ffn_norm_g[0]), w_r_hi, w_r_lo, b_r, tm=TOK_TILE, n_exp=n_exp, n_grp=n_grp,
                           per_grp=per_grp)
    i32 = jnp.int32
    sub, item_rows = MOE_SUB, MOE_SUB * MOE_ITEM_SUBS
    n_assign = n_tok * TOP_K
    rows_alloc = (-(-n_assign // sub) + n_exp) * sub + item_rows
    max_items = n_assign // item_rows + n_exp + 1
    cnt = counts[0, :n_exp].astype(i32)
    padded = (cnt + sub - 1) // sub * sub
    pad_end = jnp.cumsum(padded)
    pad_start = pad_end - padded
    e_idx = info[:, 0:TOP_K].astype(i32)
    rank = info[:, TOP_K:2 * TOP_K].astype(i32)
    dest = (pad_start[e_idx] + rank).reshape(-1)
    items_per_e = (padded + item_rows - 1) // item_rows
    it_end = jnp.cumsum(items_per_e)
    it_start = it_end - items_per_e
    w_ids = jnp.arange(max_items, dtype=i32)
    item_e = jnp.minimum(jnp.searchsorted(it_end, w_ids, side='right'), n_exp - 1).astype(i32)
    item_i = w_ids - it_start[item_e]
    item_row = jnp.clip(pad_start[item_e] + item_rows * item_i, 0, rows_alloc - item_rows).astype(i32)
    item_nsub = jnp.clip((padded[item_e] - item_rows * item_i) // sub, 1, MOE_ITEM_SUBS).astype(i32)
    n_items = it_end[-1:].astype(i32)
    pend = pad_end[-1:].astype(i32)

    xs = _dispatch(dest, (pad_start + cnt).astype(i32), (padded - cnt).astype(i32), pend, x1, row2(ffn_norm_g[0]),
                   tm=TOK_TILE, rows=rows_alloc, sub=sub)
    ys = _experts(item_e, item_row, item_nsub, n_items, pend, xs, w_exp_gate[0], w_exp_up[0], w_exp_down[0],
                  sub=sub, n_sub_max=MOE_ITEM_SUBS)
    y_p, y_s = _combine(dest, x1, info, row2(final_norm_g), ys, tm=CMB_TILE, n_prompt=n_p, n_sample=nb)

    hist = width - 1
    new_conv_p = u_p.reshape(batch, seq, conv_ch)[:, seq - hist:]
    new_conv_s = jnp.transpose(jnp.concatenate([state_t[1:], u_s[None]], axis=0), (1, 0, 2))
    return (y_p.reshape(batch, seq, d), y_s.reshape(nb, t_dec, d),
            c_p.reshape(1, batch, seq, kv_lora), kr_p.reshape(1, batch, seq, ROPE_DIM), new_conv_p[None],
            c_s.reshape(1, nb, t_dec, kv_lora), kr_s.reshape(1, nb, t_dec, ROPE_DIM), new_conv_s[None])
```

```python
import functools

import jax
import jax.numpy as jnp
from jax import lax
from jax.experimental import pallas as pl
from jax.experimental.pallas import tpu as pltpu

F32 = jnp.float32
BF16 = jnp.bfloat16

EPS = 1e-6
NEG_INF = -1e30
ROPE_BASE = 10000.0

LANES = 128
SUBLANES = 8
VMEM_LIMIT = 56 * 1024 * 1024

NOPE_DIM = 128
ROPE_DIM = 64
TOP_K = 2
HEAD_PAD = 256
QS_HEAD = 640

MOE_SUB = 128
MOE_ITEM_SUBS = 4
TOK_TILE = 416
CMB_TILE = 128


def _cparams(sem, vmem=VMEM_LIMIT):
    return pltpu.CompilerParams(dimension_semantics=sem, vmem_limit_bytes=vmem)


def _rms(x, g):
    return x * lax.rsqrt(jnp.mean(x * x, axis=-1, keepdims=True) + EPS) * g


def _rope128(v, c, a, b):
    return v * c + pltpu.roll(v, 96, 1) * a + pltpu.roll(v, 32, 1) * b


def _sigmoid(x):
    return 1.0 / (1.0 + jnp.exp(-x))


def _nt_dot(a, b):
    return lax.dot_general(a, b, (((1,), (1,)), ((), ())), preferred_element_type=F32)


def _in_proj_kernel(*refs, n_heads, q_lora, kv_lora, conv_ch, prompt):
    (x_ref, g_ref, w_ref, qg_ref, wq_ref, kvg_ref, wkv_ref, rc_ref, ra_ref, rb_ref) = refs[:10]
    outs = refs[10:]
    if prompt:
        q_ref, c_ref, kr_ref, u_ref, kv_ref, krb_ref = outs
    else:
        q_ref, c_ref, kr_ref, u_ref = outs
    o_kv = q_lora
    o_a = q_lora + kv_lora
    o_g = o_a + conv_ch
    o_kr = o_g + conv_ch

    h = _rms(x_ref[...], g_ref[...]).astype(BF16)
    rc, ra, rb = rc_ref[...], ra_ref[...], rb_ref[...]

    zq = jnp.dot(h, w_ref[:, 0:q_lora], preferred_element_type=F32)
    qn = _rms(zq, qg_ref[...]).astype(BF16)
    qf = jnp.dot(qn, wq_ref[...], preferred_element_type=F32)

    zkv = jnp.dot(h, w_ref[:, o_kv:o_a], preferred_element_type=F32)
    c = _rms(zkv, kvg_ref[...])
    c_ref[...] = c

    zkr = jnp.dot(h, w_ref[:, o_kr:o_kr + LANES], preferred_element_type=F32)
    kr = _rope128(zkr, rc, ra, rb)
    kr_ref[...] = kr[:, :ROPE_DIM]

    a = jnp.dot(h, w_ref[:, o_a:o_g], preferred_element_type=F32)
    gate = jnp.dot(h, w_ref[:, o_g:o_kr], preferred_element_type=F32)
    u_ref[...] = a * _sigmoid(gate)

    if prompt:
        krb_ref[...] = kr.astype(BF16)
        kv_ref[...] = jnp.dot(c.astype(BF16), wkv_ref[...], preferred_element_type=F32).astype(BF16)
        for hd in range(n_heads):
            o = hd * HEAD_PAD
            q_ref[:, o:o + NOPE_DIM] = qf[:, o:o + NOPE_DIM].astype(BF16)
            q_ref[:, o + NOPE_DIM:o + HEAD_PAD] = _rope128(
                qf[:, o + NOPE_DIM:o + HEAD_PAD], rc, ra, rb).astype(BF16)
    else:
        for hd in range(n_heads):
            o = hd * HEAD_PAD
            qnope = qf[:, o:o + NOPE_DIM].astype(BF16)
            wkn = wkv_ref[:, o:o + NOPE_DIM]
            qo = hd * QS_HEAD
            q_ref[:, qo:qo + kv_lora] = _nt_dot(qnope, wkn).astype(BF16)
            q_ref[:, qo + kv_lora:qo + QS_HEAD] = _rope128(
                qf[:, o + NOPE_DIM:o + HEAD_PAD], rc, ra, rb).astype(BF16)


def _in_proj(x, attn_g, w_in_p, q_g, wq_p, kv_g, wkv_p, rc, ra, rb, *, tm, prompt, n_heads, q_lora, kv_lora,
             conv_ch):
    n, d = x.shape
    n_tab = rc.shape[0] // tm
    const = lambda i: (0, 0)
    row = lambda i: (i, 0)
    tab = lambda i: (i % n_tab, 0)
    single = pl.Buffered(1)
    in_specs = [
        pl.BlockSpec((tm, d), row),
        pl.BlockSpec((1, d), const),
        pl.BlockSpec(w_in_p.shape, const, pipeline_mode=single),
        pl.BlockSpec((1, q_lora), const),
        pl.BlockSpec(wq_p.shape, const, pipeline_mode=single),
        pl.BlockSpec((1, kv_lora), const),
        pl.BlockSpec(wkv_p.shape, const, pipeline_mode=single),
        pl.BlockSpec((tm, LANES), tab),
        pl.BlockSpec((tm, LANES), tab),
        pl.BlockSpec((tm, LANES), tab),
    ]
    q_cols = n_heads * (HEAD_PAD if prompt else QS_HEAD)
    out_shape = [
        jax.ShapeDtypeStruct((n, q_cols), BF16),
        jax.ShapeDtypeStruct((n, kv_lora), F32),
        jax.ShapeDtypeStruct((n, ROPE_DIM), F32),
        jax.ShapeDtypeStruct((n, conv_ch), F32),
    ]
    out_specs = [
        pl.BlockSpec((tm, q_cols), row),
        pl.BlockSpec((tm, kv_lora), row),
        pl.BlockSpec((tm, ROPE_DIM), row),
        pl.BlockSpec((tm, conv_ch), row),
    ]
    if prompt:
        out_shape += [jax.ShapeDtypeStruct((n, wkv_p.shape[1]), BF16), jax.ShapeDtypeStruct((n, LANES), BF16)]
        out_specs += [pl.BlockSpec((tm, wkv_p.shape[1]), row), pl.BlockSpec((tm, LANES), row)]
    kern = functools.partial(_in_proj_kernel, n_heads=n_heads, q_lora=q_lora, kv_lora=kv_lora, conv_ch=conv_ch,
                             prompt=prompt)
    return pl.pallas_call(
        kern, grid=(n // tm,), in_specs=in_specs, out_specs=out_specs, out_shape=out_shape,
        compiler_params=_cparams(("arbitrary",)),
        name="in_proj_prompt" if prompt else "in_proj_sample",
    )(x, attn_g, w_in_p, q_g, wq_p, kv_g, wkv_p, rc, ra, rb)


def _attn_prompt_kernel(q_ref, kv_ref, krb_ref, o_ref, kcat_ref, *, tq, tk, scale):
    i = pl.program_id(2)

    @pl.when(i == 0)
    def _():
        kcat_ref[:, :NOPE_DIM] = kv_ref[:, :NOPE_DIM]
        kcat_ref[:, NOPE_DIM:] = krb_ref[...]

    q = q_ref[...]
    v_dim = o_ref.shape[1]

    def scores(j):
        k = kcat_ref[pl.ds(pl.multiple_of(j * tk, tk), tk), :]
        return _nt_dot(q, k) * scale

    def update(j, s, carry):
        m, l, acc = carry
        m_new = jnp.maximum(m, jnp.max(s, axis=-1, keepdims=True))
        alpha = jnp.exp(m - m_new)
        p = jnp.exp(s - m_new)
        v = kv_ref[pl.ds(pl.multiple_of(j * tk, tk), tk), NOPE_DIM:]
        acc = acc * alpha + jnp.dot(p.astype(BF16), v, preferred_element_type=F32)
        return m_new, l * alpha + jnp.sum(p, axis=-1, keepdims=True), acc

    def body(j, carry):
        s_next = scores(j + 1)
        return update(j, carry[3], carry[:3]) + (s_next,)

    init = (jnp.full((tq, 1), NEG_INF, F32), jnp.zeros((tq, 1), F32), jnp.zeros((tq, v_dim), F32), scores(0))
    m, l, acc, s = lax.fori_loop(0, i, body, init)
    rowi = lax.broadcasted_iota(jnp.int32, (tq, tk), 0)
    coli = lax.broadcasted_iota(jnp.int32, (tq, tk), 1)
    m, l, acc = update(i, jnp.where(coli <= rowi, s, NEG_INF), (m, l, acc))
    o_ref[...] = (acc / l).astype(o_ref.dtype)


def _attn_prompt(q, kv, krb, *, batch, seq, n_heads, v_dim, scale, tq=512):
    n = q.shape[0]
    nq = seq // tq
    kern = functools.partial(_attn_prompt_kernel, tq=tq, tk=tq, scale=scale)
    return pl.pallas_call(
        kern, grid=(batch, n_heads, nq),
        in_specs=[
            pl.BlockSpec((tq, HEAD_PAD), lambda b, h, i: (b * nq + i, h)),
            pl.BlockSpec((seq, HEAD_PAD), lambda b, h, i: (b, h)),
            pl.BlockSpec((seq, LANES), lambda b, h, i: (b, 0)),
        ],
        out_specs=pl.BlockSpec((tq, v_dim), lambda b, h, i: (b * nq + i, h)),
        out_shape=jax.ShapeDtypeStruct((n, n_heads * v_dim), BF16),
        scratch_shapes=[pltpu.VMEM((seq, HEAD_PAD), BF16)],
        compiler_params=_cparams(("arbitrary", "arbitrary", "arbitrary")),
        name="attn_prompt",
    )(q, kv, krb)


def _ln_silu(y, g, b):
    yc = y - jnp.mean(y, axis=-1, keepdims=True)
    z = yc * lax.rsqrt(jnp.mean(yc * yc, axis=-1, keepdims=True) + EPS) * g + b
    return z * _sigmoid(z)


def _conv_prompt_kernel(halo_ref, u_ref, w_ref, b_ref, g_ref, beta_ref, o_ref, win_ref, y_ref, sh_ref, *, tt,
                        halo, width, rc):
    i = pl.program_id(1)

    @pl.when(i == 0)
    def _():
        win_ref[0:halo, :] = jnp.zeros((halo, win_ref.shape[1]), F32)

    @pl.when(i > 0)
    def _():
        win_ref[0:halo, :] = halo_ref[...]

    win_ref[halo:, :] = u_ref[...]
    base = halo - (width - 1)
    ch = win_ref.shape[1]
    n_sh = tt + halo - SUBLANES
    for lc in range(ch // LANES):
        ls = slice(lc * LANES, (lc + 1) * LANES)
        wl = w_ref[:, ls]
        for g in range(1, SUBLANES):
            sh_ref[g, 0:n_sh, :] = win_ref[g:g + n_sh, ls]

        def row_chunk(r, c, ls=ls, wl=wl):
            r0 = pl.multiple_of(r * rc, rc)
            acc = jnp.zeros((rc, LANES), F32)
            for k in range(width):
                g = (base + k) % SUBLANES
                s0 = r0 + ((base + k) - g)
                tap = win_ref[pl.ds(s0, rc), ls] if g == 0 else sh_ref[g, pl.ds(s0, rc), :]
                acc = acc + tap * wl[k:k + 1, :]
            y_ref[pl.ds(r0, rc), ls] = acc
            return c
        lax.fori_loop(0, tt // rc, row_chunk, 0)
    o_ref[...] = _ln_silu(y_ref[...] + b_ref[...], g_ref[...], beta_ref[...]).astype(o_ref.dtype)


def _conv_prompt(u, conv_w, conv_b, ln_g, ln_b, *, batch, seq, tt=256, halo=32, rc=64):
    n, ch = u.shape
    width = conv_w.shape[0]
    nt = seq // tt
    hb = tt // halo
    const = lambda b, i: (0, 0)
    kern = functools.partial(_conv_prompt_kernel, tt=tt, halo=halo, width=width, rc=rc)
    return pl.pallas_call(
        kern, grid=(batch, nt),
        in_specs=[
            pl.BlockSpec((halo, ch), lambda b, i: (jnp.maximum((b * nt + i) * hb - 1, 0), 0)),
            pl.BlockSpec((tt, ch), lambda b, i: (b * nt + i, 0)),
            pl.BlockSpec((width, ch), const),
            pl.BlockSpec((1, ch), const),
            pl.BlockSpec((1, ch), const),
            pl.BlockSpec((1, ch), const),
        ],
        out_specs=pl.BlockSpec((tt, ch), lambda b, i: (b * nt + i, 0)),
        out_shape=jax.ShapeDtypeStruct((n, ch), BF16),
        scratch_shapes=[pltpu.VMEM((halo + tt, ch), F32), pltpu.VMEM((tt, ch), F32),
                        pltpu.VMEM((SUBLANES, halo + tt, LANES), F32)],
        compiler_params=_cparams(("arbitrary", "arbitrary")),
        name="conv_prompt",
    )(u, u, conv_w, conv_b, ln_g, ln_b)


def _conv_sample_kernel(st_ref, u_ref, w_ref, b_ref, g_ref, beta_ref, o_ref, *, width):
    w = w_ref[...]
    y = u_ref[...] * w[width - 1:width, :] + b_ref[...]
    for k in range(width - 1):
        y = y + st_ref[k] * w[k:k + 1, :]
    o_ref[...] = _ln_silu(y, g_ref[...], beta_ref[...]).astype(o_ref.dtype)


def _conv_sample(state_t, u, conv_w, conv_b, ln_g, ln_b, *, tb=32):
    hist, nb, ch = state_t.shape
    width = conv_w.shape[0]
    const = lambda i: (0, 0)
    return pl.pallas_call(
        functools.partial(_conv_sample_kernel, width=width), grid=(nb // tb,),
        in_specs=[
            pl.BlockSpec((hist, tb, ch), lambda i: (0, i, 0)),
            pl.BlockSpec((tb, ch), lambda i: (i, 0)),
            pl.BlockSpec((width, ch), const),
            pl.BlockSpec((1, ch), const),
            pl.BlockSpec((1, ch), const),
            pl.BlockSpec((1, ch), const),
        ],
        out_specs=pl.BlockSpec((tb, ch), lambda i: (i, 0)),
        out_shape=jax.ShapeDtypeStruct((nb, ch), BF16),
        compiler_params=_cparams(("arbitrary",)),
        name="conv_sample",
    )(state_t, u, conv_w, conv_b, ln_g, ln_b)


def _attn_sample_kernel(pt_ref, qs_ref, cn_ref, krn_ref, cache_c, cache_krt, o_ref, cbuf, rbuf, sem, m_ref, l_ref,
                        acc_ref, *, pages, n_chunks, page, kv_lora, sub, scale):
    b = pl.program_id(0)
    ch = pl.program_id(1)
    g = b * n_chunks + ch
    n_steps = pl.num_programs(0) * n_chunks
    slot = g % 2

    def copies(step, sl):
        out = []
        for p in range(pages):
            pid = pt_ref[step * pages + p]
            keys = pl.ds(p * page, page)
            out.append(pltpu.make_async_copy(cache_c.at[pid], cbuf.at[sl, keys, :], sem.at[0, sl]))
            out.append(pltpu.make_async_copy(cache_krt.at[pid], rbuf.at[sl, :, keys], sem.at[1, sl]))
        return out

    @pl.when(g == 0)
    def _():
        for cp in copies(0, 0):
            cp.start()

    @pl.when(g + 1 < n_steps)
    def _():
        for cp in copies(g + 1, 1 - slot):
            cp.start()

    @pl.when(ch == 0)
    def _():
        m_ref[...] = jnp.full(m_ref.shape, NEG_INF, F32)
        l_ref[...] = jnp.zeros(l_ref.shape, F32)
        acc_ref[...] = jnp.zeros(acc_ref.shape, F32)

    for cp in copies(g, slot):
        cp.wait()

    q = qs_ref[0]
    ql = q[:, :kv_lora]
    qr = q[:, kv_lora:kv_lora + ROPE_DIM]
    cbs, scores = [], []
    for k0 in range(0, pages * page, sub):
        cb = cbuf[slot, k0:k0 + sub, :].astype(BF16)
        rb = rbuf[slot, :, k0:k0 + sub].astype(BF16)
        cbs.append(cb)
        scores.append((_nt_dot(ql, cb) + jnp.dot(qr, rb, preferred_element_type=F32)) * scale)
    probs = []
    for s in scores:
        m_k = jnp.max(s, axis=-1, keepdims=True)
        p = jnp.exp(s - m_k)
        probs.append((m_k, jnp.sum(p, axis=-1, keepdims=True), p.astype(BF16)))
    parts = [(m_k, l_k, jnp.dot(p, cb, preferred_element_type=F32)) for (m_k, l_k, p), cb in zip(probs, cbs)]
    m_old = m_ref[...]
    m_new = m_old
    for m_k, _, _ in parts:
        m_new = jnp.maximum(m_new, m_k)
    alpha = jnp.exp(m_old - m_new)
    l_new = l_ref[...] * alpha
    acc = acc_ref[...] * alpha
    for m_k, l_k, a_k in parts:
        w_k = jnp.exp(m_k - m_new)
        l_new = l_new + l_k * w_k
        acc = acc + a_k * w_k
    l_ref[...] = l_new
    acc_ref[...] = acc
    m_ref[...] = m_new

    @pl.when(ch == n_chunks - 1)
    def _():
        cn = cn_ref[0].astype(BF16).astype(F32)
        krn = krn_ref[0].astype(BF16).astype(F32)
        s_new = (jnp.sum(ql.astype(F32) * cn, axis=-1, keepdims=True)
                 + jnp.sum(qr.astype(F32) * krn, axis=-1, keepdims=True)) * scale
        m_o = m_ref[...]
        m_n = jnp.maximum(m_o, s_new)
        al = jnp.exp(m_o - m_n)
        p_new = jnp.exp(s_new - m_n)
        l_fin = l_ref[...] * al + p_new
        acc = acc_ref[...] * al + p_new.astype(BF16).astype(F32) * cn
        o_ref[0] = acc / l_fin


def _attn_sample(page_table, qs3, c_new3, kr_new3, cache_c, cache_krt, *, scale, pages=32, sub=512):
    nb, n_heads, qw = qs3.shape
    n_pages = page_table.shape[1]
    page, kv_lora = cache_c.shape[1:]
    rope = cache_krt.shape[1]
    n_chunks = n_pages // pages
    keys = pages * page
    kern = functools.partial(_attn_sample_kernel, pages=pages, n_chunks=n_chunks, page=page, kv_lora=kv_lora,
                             sub=sub, scale=scale)
    grid_spec = pltpu.PrefetchScalarGridSpec(
        num_scalar_prefetch=1, grid=(nb, n_chunks),
        in_specs=[
            pl.BlockSpec((1, n_heads, qw), lambda b, c, pt: (b, 0, 0)),
            pl.BlockSpec((1, 1, kv_lora), lambda b, c, pt: (b, 0, 0)),
            pl.BlockSpec((1, 1, rope), lambda b, c, pt: (b, 0, 0)),
            pl.BlockSpec(memory_space=pl.ANY),
            pl.BlockSpec(memory_space=pl.ANY),
        ],
        out_specs=pl.BlockSpec((1, n_heads, kv_lora), lambda b, c, pt: (b, 0, 0)),
        scratch_shapes=[
            pltpu.VMEM((2, keys, kv_lora), F32),
            pltpu.VMEM((2, rope, keys), F32),
            pltpu.SemaphoreType.DMA((2, 2)),
            pltpu.VMEM((n_heads, 1), F32),
            pltpu.VMEM((n_heads, 1), F32),
            pltpu.VMEM((n_heads, kv_lora), F32),
        ],
    )
    return pl.pallas_call(
        kern, grid_spec=grid_spec,
        out_shape=jax.ShapeDtypeStruct((nb, n_heads, kv_lora), F32),
        compiler_params=_cparams(("arbitrary", "arbitrary")),
        name="attn_sample",
    )(page_table.reshape(-1), qs3, c_new3, kr_new3, cache_c, cache_krt)


def _v_up_kernel(ol_ref, wkv_ref, o_ref, *, n_heads, kv_lora, v_dim):
    for hd in range(n_heads):
        ol = ol_ref[:, hd * kv_lora:(hd + 1) * kv_lora].astype(BF16)
        wv = wkv_ref[:, hd * HEAD_PAD + NOPE_DIM:(hd + 1) * HEAD_PAD]
        o_ref[:, hd * v_dim:(hd + 1) * v_dim] = jnp.dot(ol, wv, preferred_element_type=F32).astype(o_ref.dtype)


def _v_up(o_lat2, wkv_p, *, n_heads, kv_lora, v_dim):
    nb = o_lat2.shape[0]
    return pl.pallas_call(
        functools.partial(_v_up_kernel, n_heads=n_heads, kv_lora=kv_lora, v_dim=v_dim),
        out_shape=jax.ShapeDtypeStruct((nb, n_heads * v_dim), BF16),
        compiler_params=_cparams(None),
        name="v_up_sample",
    )(o_lat2, wkv_p)


def _out_proj_kernel(ap_ref, bp_ref, xp_ref, as_ref, bs_ref, xs_ref, w_ref, o_ref, *, half, n_prompt_tiles):
    i = pl.program_id(0)

    def mix(a_ref, b_ref, x_ref):
        return (x_ref[...] + jnp.dot(a_ref[...], w_ref[0:half, :], preferred_element_type=F32)
                + jnp.dot(b_ref[...], w_ref[half:, :], preferred_element_type=F32))

    @pl.when(i < n_prompt_tiles)
    def _():
        o_ref[...] = mix(ap_ref, bp_ref, xp_ref)

    @pl.when(i == n_prompt_tiles)
    def _():
        o_ref[0:as_ref.shape[0], :] = mix(as_ref, bs_ref, xs_ref)


def _out_proj(a_p, b_p, x_p, a_s, b_s, x_s, w, *, tm):
    n_p, half = a_p.shape
    n_s = a_s.shape[0]
    d = w.shape[1]
    npt = n_p // tm
    assert n_s <= tm
    const = lambda i: (0, 0)
    prow = lambda i: (jnp.minimum(i, npt - 1), 0)
    return pl.pallas_call(
        functools.partial(_out_proj_kernel, half=half, n_prompt_tiles=npt), grid=(npt + 1,),
        in_specs=[
            pl.BlockSpec((tm, half), prow),
            pl.BlockSpec((tm, half), prow),
            pl.BlockSpec((tm, d), prow),
            pl.BlockSpec((n_s, half), const),
            pl.BlockSpec((n_s, half), const),
            pl.BlockSpec((n_s, d), const),
            pl.BlockSpec(w.shape, const, pipeline_mode=pl.Buffered(1)),
        ],
        out_specs=pl.BlockSpec((tm, d), lambda i: (i, 0)),
        out_shape=jax.ShapeDtypeStruct((n_p + n_s, d), F32),
        compiler_params=_cparams(("arbitrary",)),
        name="out_proj",
    )(a_p, b_p, x_p, a_s, b_s, x_s, w)


def _router_kernel(x_ref, g_ref, w_ref, bias_ref, info_ref, cnt_ref, run_ref, *, n_exp, n_grp, per_grp):
    i = pl.program_id(0)

    @pl.when(i == 0)
    def _():
        run_ref[...] = jnp.zeros(run_ref.shape, F32)

    tm = x_ref.shape[0]
    h = _rms(x_ref[...], g_ref[...]).astype(BF16)
    logits = jnp.dot(h, w_ref[...], preferred_element_type=F32) + bias_ref[...]
    lane_i = lax.broadcasted_iota(jnp.int32, (tm, LANES), 1)
    lane = lane_i.astype(F32)
    big = float(LANES)

    def first_max(vals):
        mx = jnp.max(vals, axis=-1, keepdims=True)
        idx = jnp.min(jnp.where(vals == mx, lane, big), axis=-1, keepdims=True)
        return mx, idx

    is_grp = (lane_i >= n_exp) & (lane_i < n_exp + n_grp)
    lg = jnp.where(is_grp, logits, NEG_INF)
    g_max, g_lane = first_max(lg)
    g_p = 1.0 / jnp.sum(jnp.where(is_grp, jnp.exp(lg - g_max), 0.0), axis=-1, keepdims=True)
    g_idx = g_lane - float(n_exp)
    lo = g_idx * float(per_grp)
    in_grp = (lane >= lo) & (lane < lo + float(per_grp))
    le = jnp.where(in_grp, logits, NEG_INF)
    m1, i1 = first_max(le)
    m2, i2 = first_max(jnp.where(lane == i1, NEG_INF, le))
    e2 = jnp.exp(m2 - m1)
    w1 = g_p / (1.0 + e2)
    w2 = g_p * e2 / (1.0 + e2)

    oh1 = lane == i1
    oh2 = lane == i2
    onehot = jnp.where(oh1 | oh2, 1.0, 0.0)
    r_i = lax.broadcasted_iota(jnp.int32, (tm, tm), 0)
    c_i = lax.broadcasted_iota(jnp.int32, (tm, tm), 1)
    lower = jnp.where(c_i < r_i, 1.0, 0.0).astype(BF16)
    before = run_ref[...] + jnp.dot(lower, onehot.astype(BF16), preferred_element_type=F32)
    rank1 = jnp.sum(jnp.where(oh1, before, 0.0), axis=-1, keepdims=True)
    rank2 = jnp.sum(jnp.where(oh2, before, 0.0), axis=-1, keepdims=True)
    run_ref[...] = run_ref[...] + jnp.sum(onehot, axis=0, keepdims=True)
    cnt_ref[...] = run_ref[...]

    info = jnp.where(lane_i == 0, i1, jnp.where(lane_i == 1, i2, jnp.where(lane_i == 2, rank1, jnp.where(
        lane_i == 3, rank2, jnp.where(lane_i == 4, w1, jnp.where(lane_i == 5, w2, 0.0))))))
    info_ref[...] = info


def _router(x1, ffn_g, w_r, bias, *, tm, n_exp, n_grp, per_grp):
    n, d = x1.shape
    const = lambda i: (0, 0)
    return pl.pallas_call(
        functools.partial(_router_kernel, n_exp=n_exp, n_grp=n_grp, per_grp=per_grp), grid=(n // tm,),
        in_specs=[
            pl.BlockSpec((tm, d), lambda i: (i, 0)),
            pl.BlockSpec((1, d), const),
            pl.BlockSpec((d, LANES), const),
            pl.BlockSpec((1, LANES), const),
        ],
        out_specs=[pl.BlockSpec((tm, LANES), lambda i: (i, 0)), pl.BlockSpec((1, LANES), const)],
        out_shape=[jax.ShapeDtypeStruct((n, LANES), F32), jax.ShapeDtypeStruct((1, LANES), F32)],
        scratch_shapes=[pltpu.VMEM((1, LANES), F32)],
        compiler_params=_cparams(("arbitrary",)),
        name="router",
    )(x1, ffn_g, w_r, bias)


def _pow2_pieces(limit):
    p = limit // 2
    while p >= 1:
        yield p
        p //= 2


def _dispatch_kernel(dest_ref, zrow_ref, zcnt_ref, pend_ref, x_ref, g_ref, xs_ref, hbuf, zbuf, sem, zsem, *, tm, sub):
    i = pl.program_id(0)
    n = pl.num_programs(0)
    slot = i % 2
    n_exp = zrow_ref.shape[0]
    n_sub_blocks = xs_ref.shape[0] // sub

    def row_copy(sl, r, d):
        return pltpu.make_async_copy(hbuf.at[sl, pl.ds(r, 1), :], xs_ref.at[pl.ds(d, 1), :], sem.at[sl])

    def wait_all(sl):
        for _ in range(TOP_K):
            pltpu.make_async_copy(hbuf.at[sl], xs_ref.at[pl.ds(0, tm), :], sem.at[sl]).wait()

    def zero_copy(row, size):
        return pltpu.make_async_copy(zbuf.at[pl.ds(0, size), :], xs_ref.at[pl.ds(row, size), :], zsem)

    def zero_fill(start):
        def per_expert(e, c):
            row = zrow_ref[e]
            cnt = zcnt_ref[e]
            head = (-row) & (SUBLANES - 1)
            for r in range(SUBLANES - 1):
                @pl.when(r < jnp.minimum(head, cnt))
                def _():
                    cp = zero_copy(row + r, 1)
                    cp.start() if start else cp.wait()
            rest = jnp.maximum(cnt - head, 0)
            row = pl.multiple_of(row + head, SUBLANES)
            for piece in _pow2_pieces(sub):
                if piece < SUBLANES:
                    break
                @pl.when((rest & piece) != 0)
                def _():
                    cp = zero_copy(row, piece)
                    cp.start() if start else cp.wait()
                row = pl.multiple_of(row + (rest & piece), SUBLANES)
            return c
        lax.fori_loop(0, n_exp, per_expert, 0)

        def per_block(b, c):
            @pl.when(b * sub >= pend_ref[0])
            def _():
                cp = zero_copy(b * sub, sub)
                cp.start() if start else cp.wait()
            return c
        lax.fori_loop(0, n_sub_blocks, per_block, 0)

    @pl.when(i == 0)
    def _():
        zbuf[...] = jnp.zeros(zbuf.shape, F32)
        zero_fill(True)

    hbuf[slot] = _rms(x_ref[...], g_ref[...])

    def issue(r, c):
        a = (i * tm + r) * TOP_K
        for k in range(TOP_K):
            row_copy(slot, r, dest_ref[a + k]).start()
        return c
    lax.fori_loop(0, tm, issue, 0, unroll=8)

    @pl.when(i == 0)
    def _():
        zero_fill(False)

    @pl.when(i > 0)
    def _():
        wait_all(1 - slot)

    @pl.when(i == n - 1)
    def _():
        wait_all(slot)


def _dispatch(dest, zrow, zcnt, pend, x1, ffn_g, *, tm, rows, sub):
    n, d = x1.shape
    grid_spec = pltpu.PrefetchScalarGridSpec(
        num_scalar_prefetch=4, grid=(n // tm,),
        in_specs=[
            pl.BlockSpec((tm, d), lambda i, *_: (i, 0)),
            pl.BlockSpec((1, d), lambda i, *_: (0, 0)),
        ],
        out_specs=pl.BlockSpec(memory_space=pl.ANY),
        scratch_shapes=[pltpu.VMEM((2, tm, d), F32), pltpu.VMEM((sub, d), F32), pltpu.SemaphoreType.DMA((2,)),
                        pltpu.SemaphoreType.DMA(())],
    )
    return pl.pallas_call(
        functools.partial(_dispatch_kernel, tm=tm, sub=sub), grid_spec=grid_spec,
        out_shape=jax.ShapeDtypeStruct((rows, d), F32),
        compiler_params=_cparams(("arbitrary",)),
        name="moe_dispatch",
    )(dest, zrow, zcnt, pend, x1, ffn_g)


def _experts_kernel(ie_ref, ir_ref, ins_ref, nit_ref, pend_ref, xs_ref, wg_hbm, wu_hbm, wd_hbm, ys_ref,
                    gu_ring, d_ring, xbuf, gacc, uacc, hbuf, ybuf, zbuf,
                    gu_sem, d_sem, x_sem, y_sem, z_sem, *, sub, n_sub_max, kc, fcs):
    n_items = nit_ref[0]
    d = xs_ref.shape[1]
    n_kc = d // kc
    n_fc = len(fcs)
    assert n_fc == 2 and n_kc >= 2
    item_rows = sub * n_sub_max
    f_offs = [sum(fcs[:j]) for j in range(n_fc)]

    def gu_copies(item, pos, slot):
        e = ie_ref[item]
        rows = pl.ds(pos * kc, kc)
        return (pltpu.make_async_copy(wg_hbm.at[e, rows, :], gu_ring.at[slot, 0], gu_sem.at[slot]),
                pltpu.make_async_copy(wu_hbm.at[e, rows, :], gu_ring.at[slot, 1], gu_sem.at[slot]))

    def d_copy(item, j):
        e = ie_ref[item]
        return pltpu.make_async_copy(wd_hbm.at[e, pl.ds(f_offs[j], fcs[j]), :], d_ring.at[j, pl.ds(0, fcs[j]), :],
                                     d_sem.at[j])

    def x_xfer(item, slot, start):
        n_sub = ins_ref[item]
        for sb in range(n_sub_max):
            @pl.when(sb < n_sub)
            def _():
                r = pl.multiple_of(ir_ref[item] + sb * sub, sub)
                for k in range(n_kc):
                    cp = pltpu.make_async_copy(xs_ref.at[pl.ds(r, sub), pl.ds(k * kc, kc)],
                                               xbuf.at[slot, k, pl.ds(sb * sub, sub), :], x_sem.at[slot])
                    cp.start() if start else cp.wait()

    def y_copy(item, sb):
        r = pl.multiple_of(ir_ref[item] + sb * sub, sub)
        return pltpu.make_async_copy(ybuf.at[pl.ds(sb * sub, sub), :], ys_ref.at[pl.ds(r, sub), :], y_sem)

    def for_rows(n_sub, fn):
        for ns in range(1, n_sub_max + 1):
            @pl.when(n_sub == ns)
            def _():
                fn(ns * sub)

    def y_writes(item, start):
        n_sub = ins_ref[item]
        for sb in range(n_sub_max):
            @pl.when(sb < n_sub)
            def _():
                cp = y_copy(item, sb)
                cp.start() if start else cp.wait()

    zbuf[...] = jnp.zeros(zbuf.shape, F32)
    n_out_blocks = ys_ref.shape[0] // sub

    def tail(start):
        def body(b, c):
            @pl.when(b * sub >= pend_ref[0])
            def _():
                cp = pltpu.make_async_copy(zbuf, ys_ref.at[pl.ds(b * sub, sub), :], z_sem)
                cp.start() if start else cp.wait()
            return c
        lax.fori_loop(0, n_out_blocks, body, 0)
    tail(True)

    @pl.when(n_items > 0)
    def _():
        x_xfer(0, 0, True)
        for pos in range(2):
            for cp in gu_copies(0, pos, pos):
                cp.start()

    def item_body(item, carry):
        xslot = item % 2
        n_sub = ins_ref[item]
        has_next = item + 1 < n_items

        @pl.when(has_next)
        def _():
            x_xfer(item + 1, 1 - xslot, True)

        x_xfer(item, xslot, False)

        def zero_acc(rows):
            gacc[0:rows, :] = jnp.zeros((rows, gacc.shape[1]), F32)
            uacc[0:rows, :] = jnp.zeros((rows, uacc.shape[1]), F32)
        for_rows(n_sub, zero_acc)

        def gu_step(pos, c):
            slot = pos % 2
            for cp in gu_copies(item, pos, slot):
                cp.wait()

            def mm(rows):
                xk = xbuf[xslot, pos, 0:rows, :].astype(BF16)
                gacc[0:rows, :] += jnp.dot(xk, gu_ring[slot, 0].astype(BF16), preferred_element_type=F32)
                uacc[0:rows, :] += jnp.dot(xk, gu_ring[slot, 1].astype(BF16), preferred_element_type=F32)
            for_rows(n_sub, mm)

            @pl.when(pos + 2 < n_kc)
            def _():
                for cp in gu_copies(item, pos + 2, slot):
                    cp.start()
            for j in range(n_fc):
                @pl.when(pos + 2 == n_kc + j)
                def _():
                    d_copy(item, j).start()
            return c
        lax.fori_loop(0, n_kc, gu_step, 0)

        def act(rows):
            g = gacc[0:rows, :]
            hbuf[0:rows, :] = (g * _sigmoid(g) * uacc[0:rows, :]).astype(BF16)
        for_rows(n_sub, act)

        @pl.when(item > 0)
        def _():
            y_writes(item - 1, False)

        for j in range(n_fc):
            d_copy(item, j).wait()

            def mm_down(rows, j=j):
                y = jnp.dot(hbuf[0:rows, f_offs[j]:f_offs[j] + fcs[j]], d_ring[j, 0:fcs[j], :].astype(BF16),
                            preferred_element_type=F32)
                if j == 0:
                    ybuf[0:rows, :] = y
                else:
                    ybuf[0:rows, :] += y
            for_rows(n_sub, mm_down)

            @pl.when(has_next)
            def _():
                for cp in gu_copies(item + 1, j, j):
                    cp.start()
        y_writes(item, True)
        return carry

    lax.fori_loop(0, n_items, item_body, 0)

    @pl.when(n_items > 0)
    def _():
        y_writes(n_items - 1, False)
    tail(False)


def _experts(item_e, item_row, item_nsub, n_items, pend, xs, w_gate, w_up, w_down, *, sub, n_sub_max):
    rows, d = xs.shape
    f = w_gate.shape[2]
    kc = 512
    lanes_f = f // LANES
    fcs = ((lanes_f + 1) // 2 * LANES, lanes_f // 2 * LANES)
    item_rows = sub * n_sub_max
    any_spec = pl.BlockSpec(memory_space=pl.ANY)
    kern = functools.partial(_experts_kernel, sub=sub, n_sub_max=n_sub_max, kc=kc, fcs=fcs)
    return pl.pallas_call(
        kern,
        grid_spec=pltpu.PrefetchScalarGridSpec(
            num_scalar_prefetch=5, grid=(1,),
            in_specs=[any_spec, any_spec, any_spec, any_spec],
            out_specs=any_spec,
            scratch_shapes=[
                pltpu.VMEM((2, 2, kc, f), F32),
                pltpu.VMEM((2, fcs[0], d), F32),
                pltpu.VMEM((2, d // kc, item_rows, kc), F32),
                pltpu.VMEM((item_rows, f), F32),
                pltpu.VMEM((item_rows, f), F32),
                pltpu.VMEM((item_rows, f), BF16),
                pltpu.VMEM((item_rows, d), F32),
                pltpu.VMEM((sub, d), F32),
                pltpu.SemaphoreType.DMA((2,)),
                pltpu.SemaphoreType.DMA((2,)),
                pltpu.SemaphoreType.DMA((2,)),
                pltpu.SemaphoreType.DMA(()),
                pltpu.SemaphoreType.DMA(()),
            ],
        ),
        out_shape=jax.ShapeDtypeStruct((rows, d), F32),
        compiler_params=_cparams(("arbitrary",), 60 * 1024 * 1024),
        name="moe_experts",
    )(item_e, item_row, item_nsub, n_items, pend, xs, w_gate, w_up, w_down)


def _combine_kernel(dest_ref, x_ref, info_ref, g_ref, ys_ref, yp_ref, ys_out_ref, rbuf, sem, *, tm, n_prompt_tiles):
    i = pl.program_id(0)
    n = pl.num_programs(0)
    slot = i % 2

    def row_copy(sl, k, r, d):
        return pltpu.make_async_copy(ys_ref.at[pl.ds(d, 1), :], rbuf.at[sl, k, pl.ds(r, 1), :], sem.at[sl])

    def issue(step, sl):
        def body(r, c):
            a = (step * tm + r) * TOP_K
            for k in range(TOP_K):
                row_copy(sl, k, r, dest_ref[a + k]).start()
            return c
        lax.fori_loop(0, tm, body, 0, unroll=8)

    @pl.when(i == 0)
    def _():
        issue(0, 0)

    @pl.when(i + 1 < n)
    def _():
        issue(i + 1, 1 - slot)

    for k in range(TOP_K):
        pltpu.make_async_copy(ys_ref.at[pl.ds(0, tm), :], rbuf.at[slot, k], sem.at[slot]).wait()

    info = info_ref[...]
    lane_i = lax.broadcasted_iota(jnp.int32, info.shape, 1)
    w1 = jnp.sum(jnp.where(lane_i == 4, info, 0.0), axis=-1, keepdims=True)
    w2 = jnp.sum(jnp.where(lane_i == 5, info, 0.0), axis=-1, keepdims=True)
    y = x_ref[...] + (rbuf[slot, 0] * w1 + rbuf[slot, 1] * w2)
    out = _rms(y, g_ref[...])

    @pl.when(i < n_prompt_tiles)
    def _():
        yp_ref[...] = out

    @pl.when(i >= n_prompt_tiles)
    def _():
        ys_out_ref[...] = out


def _combine(dest, x1, info, final_g, ys, *, tm, n_prompt, n_sample):
    n, d = x1.shape
    npt = n_prompt // tm
    grid_spec = pltpu.PrefetchScalarGridSpec(
        num_scalar_prefetch=1, grid=(n // tm,),
        in_specs=[
            pl.BlockSpec((tm, d), lambda i, ds: (i, 0)),
            pl.BlockSpec((tm, LANES), lambda i, ds: (i, 0)),
            pl.BlockSpec((1, d), lambda i, ds: (0, 0)),
            pl.BlockSpec(memory_space=pl.ANY),
        ],
        out_specs=[
            pl.BlockSpec((tm, d), lambda i, ds: (jnp.minimum(i, npt - 1), 0)),
            pl.BlockSpec((tm, d), lambda i, ds: (jnp.maximum(i - npt, 0), 0)),
        ],
        scratch_shapes=[pltpu.VMEM((2, TOP_K, tm, d), F32), pltpu.SemaphoreType.DMA((2,))],
    )
    return pl.pallas_call(
        functools.partial(_combine_kernel, tm=tm, n_prompt_tiles=npt), grid_spec=grid_spec,
        out_shape=[jax.ShapeDtypeStruct((n_prompt, d), F32), jax.ShapeDtypeStruct((n_sample, d), F32)],
        compiler_params=_cparams(("arbitrary",)),
        name="moe_combine",
    )(dest, x1, info, final_g, ys)


def _rope_tables(pos):
    half = ROPE_DIM // 2
    inv_freq = ROPE_BASE ** (-jnp.arange(half, dtype=F32) / half)
    ang = pos.astype(F32)[:, None] * inv_freq[None, :]
    cos, sin = jnp.cos(ang), jnp.sin(ang)
    z = jnp.zeros_like(cos)
    z2 = jnp.concatenate([z, z], axis=1)
    return (jnp.concatenate([cos, cos, z2], axis=1), jnp.concatenate([-sin, z, z2], axis=1),
            jnp.concatenate([z, sin, z2], axis=1))


def kernel(x_prompt, x_sample, cache_kv_latent, cache_k_rope, state_conv, page_table, attn_norm_g, w_in, q_norm_g,
           w_q_up, kv_norm_g, w_kv_up, conv_w, conv_b, conv_ln_g, conv_ln_b, w_out, ffn_norm_g, w_router_group,
           b_router_group, w_router_expert, b_router_expert, w_exp_gate, w_exp_up, w_exp_down, final_norm_g):
    batch, seq, d = x_prompt.shape
    nb, t_dec, _ = x_sample.shape
    depth, q_lora, n_heads, qk_dim = w_q_up.shape
    kv_lora = w_kv_up.shape[1]
    v_dim = w_kv_up.shape[3] - NOPE_DIM
    conv_ch = conv_w.shape[2]
    width = conv_w.shape[1]
    n_exp = w_exp_gate.shape[1]
    n_grp = w_router_group.shape[2]
    per_grp = n_exp // n_grp
    page = cache_kv_latent.shape[2]
    past = page_table.shape[1] * page
    assert depth == 1 and t_dec == 1 and qk_dim == NOPE_DIM + ROPE_DIM and v_dim == NOPE_DIM
    scale = float(qk_dim) ** -0.5
    n_p = batch * seq
    n_tok = n_p + nb

    wi = w_in[0]
    o_kv, o_kr, o_a = q_lora, q_lora + kv_lora, q_lora + kv_lora + ROPE_DIM
    w_in_p = jnp.concatenate(
        [wi[:, :o_kr], wi[:, o_a:], wi[:, o_kr:o_a], jnp.zeros((d, LANES - ROPE_DIM), F32)], axis=1).astype(BF16)
    wq_p = jnp.pad(w_q_up[0], ((0, 0), (0, 0), (0, HEAD_PAD - qk_dim))).reshape(q_lora, n_heads * HEAD_PAD)
    wq_p = wq_p.astype(BF16)
    wkv_p = w_kv_up[0].reshape(kv_lora, n_heads * HEAD_PAD).astype(BF16)
    w_out_b = w_out[0].astype(BF16)
    w_r = jnp.concatenate([w_router_expert[0], w_router_group[0],
                           jnp.zeros((d, LANES - n_exp - n_grp), F32)], axis=1).astype(BF16)
    b_r = jnp.concatenate([b_router_expert[0], b_router_group[0], jnp.zeros((LANES - n_exp - n_grp,), F32)])[None]
    row2 = lambda v: v.reshape(1, -1)

    tm_p = 256
    rc_p, ra_p, rb_p = _rope_tables(jnp.arange(seq))
    rc_s, ra_s, rb_s = _rope_tables(jnp.full((nb,), past, jnp.int32))
    dims = dict(n_heads=n_heads, q_lora=q_lora, kv_lora=kv_lora, conv_ch=conv_ch)
    xp2 = x_prompt.reshape(n_p, d)
    xs2 = x_sample.reshape(nb, d)
    q_p, c_p, kr_p, u_p, kv_p, krb_p = _in_proj(
        xp2, row2(attn_norm_g[0]), w_in_p, row2(q_norm_g[0]), wq_p, row2(kv_norm_g[0]), wkv_p, rc_p, ra_p, rb_p,
        tm=tm_p, prompt=True, **dims)
    qs_s, c_s, kr_s, u_s = _in_proj(
        xs2, row2(attn_norm_g[0]), w_in_p, row2(q_norm_g[0]), wq_p, row2(kv_norm_g[0]), wkv_p, rc_s, ra_s, rb_s,
        tm=nb, prompt=False, **dims)

    o_att_p = _attn_prompt(q_p, kv_p, krb_p, batch=batch, seq=seq, n_heads=n_heads, v_dim=v_dim, scale=scale)
    cw, cb, cg, cbeta = conv_w[0], row2(conv_b[0]), row2(conv_ln_g[0]), row2(conv_ln_b[0])
    o_conv_p = _conv_prompt(u_p, cw, cb, cg, cbeta, batch=batch, seq=seq)

    o_lat = _attn_sample(page_table, qs_s.reshape(nb, n_heads, QS_HEAD), c_s.reshape(nb, 1, kv_lora),
                         kr_s.reshape(nb, 1, ROPE_DIM), cache_kv_latent[0],
                         jnp.transpose(cache_k_rope[0], (0, 2, 1)), scale=scale)
    o_att_s = _v_up(o_lat.reshape(nb, n_heads * kv_lora), wkv_p, n_heads=n_heads, kv_lora=kv_lora, v_dim=v_dim)
    state_t = jnp.transpose(state_conv[0], (1, 0, 2))
    o_conv_s = _conv_sample(state_t, u_s, cw, cb, cg, cbeta)

    x1 = _out_proj(o_att_p, o_conv_p, xp2, o_att_s, o_conv_s, xs2, w_out_b, tm=512)

    info, counts = _router(x1, row2(ffn_norm_g[0]), w_r, b_r, tm=TOK_TILE, n_exp=n_exp, n_grp=n_grp,
                           per_grp=per_grp)
    i32 = jnp.int32
    sub, item_rows = MOE_SUB, MOE_SUB * MOE_ITEM_SUBS
    n_assign = n_tok * TOP_K
    rows_alloc = (-(-n_assign // sub) + n_exp) * sub
    max_items = n_assign // item_rows + n_exp + 1
    cnt = counts[0, :n_exp].astype(i32)
    padded = (cnt + sub - 1) // sub * sub
    pad_end = jnp.cumsum(padded)
    pad_start = pad_end - padded
    e_flat = info[:, 0:TOP_K].astype(i32).reshape(-1)
    rank = info[:, TOP_K:2 * TOP_K].astype(i32).reshape(-1)
    is_e = e_flat[:, None] == jnp.arange(n_exp, dtype=i32)[None, :]
    dest = rank + jnp.sum(jnp.where(is_e, pad_start[None, :], 0), axis=1)
    items_per_e = (padded + item_rows - 1) // item_rows
    it_end = jnp.cumsum(items_per_e)
    it_start = it_end - items_per_e
    w_ids = jnp.arange(max_items, dtype=i32)
    item_e = jnp.minimum(jnp.searchsorted(it_end, w_ids, side='right'), n_exp - 1).astype(i32)
    item_i = w_ids - it_start[item_e]
    item_row = jnp.clip(pad_start[item_e] + item_rows * item_i, 0, rows_alloc - item_rows).astype(i32)
    item_nsub = jnp.clip((padded[item_e] - item_rows * item_i) // sub, 1, MOE_ITEM_SUBS).astype(i32)
    n_items = it_end[-1:].astype(i32)
    pend = pad_end[-1:].astype(i32)

    xs = _dispatch(dest, (pad_start + cnt).astype(i32), (padded - cnt).astype(i32), pend, x1, row2(ffn_norm_g[0]),
                   tm=TOK_TILE, rows=rows_alloc, sub=sub)
    ys = _experts(item_e, item_row, item_nsub, n_items, pend, xs, w_exp_gate[0], w_exp_up[0], w_exp_down[0],
                  sub=sub, n_sub_max=MOE_ITEM_SUBS)
    y_p, y_s = _combine(dest, x1, info, row2(final_norm_g), ys, tm=CMB_TILE, n_prompt=n_p, n_sample=nb)

    hist = width - 1
    new_conv_p = u_p.reshape(batch, seq, conv_ch)[:, seq - hist:]
    new_conv_s = jnp.transpose(jnp.concatenate([state_t[1:], u_s[None]], axis=0), (1, 0, 2))
    return (y_p.reshape(batch, seq, d), y_s.reshape(nb, t_dec, d),
            c_p.reshape(1, batch, seq, kv_lora), kr_p.reshape(1, batch, seq, ROPE_DIM), new_conv_p[None],
            c_s.reshape(1, nb, t_dec, kv_lora), kr_s.reshape(1, nb, t_dec, ROPE_DIM), new_conv_s[None])
```

```python
import functools

import jax
import jax.numpy as jnp
from jax import lax
from jax.experimental import pallas as pl
from jax.experimental.pallas import tpu as pltpu

F32 = jnp.float32
BF16 = jnp.bfloat16

EPS = 1e-6
NEG_INF = -1e30
ROPE_BASE = 10000.0

LANES = 128
SUBLANES = 8
VMEM_LIMIT = 56 * 1024 * 1024

NOPE_DIM = 128
ROPE_DIM = 64
TOP_K = 2
HEAD_PAD = 256
QS_HEAD = 640

MOE_SUB = 128
MOE_ITEM_SUBS = 4
TOK_TILE = 416
CMB_TILE = 128


def _cparams(sem, vmem=VMEM_LIMIT):
    return pltpu.CompilerParams(dimension_semantics=sem, vmem_limit_bytes=vmem)


def _rms(x, g):
    return x * lax.rsqrt(jnp.mean(x * x, axis=-1, keepdims=True) + EPS) * g


def _rope128(v, c, a, b):
    return v * c + pltpu.roll(v, 96, 1) * a + pltpu.roll(v, 32, 1) * b


def _sigmoid(x):
    return 1.0 / (1.0 + jnp.exp(-x))


def _nt_dot(a, b):
    return lax.dot_general(a, b, (((1,), (1,)), ((), ())), preferred_element_type=F32)


def _in_proj_kernel(*refs, n_heads, q_lora, kv_lora, conv_ch, prompt):
    (x_ref, g_ref, w_ref, qg_ref, wq_ref, kvg_ref, wkv_ref, rc_ref, ra_ref, rb_ref) = refs[:10]
    outs = refs[10:]
    if prompt:
        q_ref, c_ref, kr_ref, u_ref, kv_ref, krb_ref = outs
    else:
        q_ref, c_ref, kr_ref, u_ref = outs
    o_kv = q_lora
    o_kr = q_lora + kv_lora
    o_g = o_kr + conv_ch
    n_in = o_kr + ROPE_DIM + 2 * conv_ch

    h = _rms(x_ref[...], g_ref[...]).astype(BF16)
    rc, ra, rb = rc_ref[...], ra_ref[...], rb_ref[...]

    zq = jnp.dot(h, w_ref[:, 0:q_lora], preferred_element_type=F32)
    qn = _rms(zq, qg_ref[...]).astype(BF16)
    qf = jnp.dot(qn, wq_ref[...], preferred_element_type=F32)

    zkv = jnp.dot(h, w_ref[:, o_kv:o_kr], preferred_element_type=F32)
    c = _rms(zkv, kvg_ref[...])
    c_ref[...] = c

    za = jnp.dot(h, w_ref[:, o_kr:o_g + LANES], preferred_element_type=F32)
    zg = jnp.dot(h, w_ref[:, o_g:n_in], preferred_element_type=F32)
    kr = _rope128(za[:, 0:LANES], rc, ra, rb)
    kr_ref[...] = kr[:, :ROPE_DIM]
    glu = za[:, 0:ROPE_DIM + conv_ch] * _sigmoid(zg)
    u_ref[...] = glu[:, ROPE_DIM:]

    if prompt:
        krb_ref[...] = kr.astype(BF16)
        kv_ref[...] = jnp.dot(c.astype(BF16), wkv_ref[...], preferred_element_type=F32).astype(BF16)
        for hd in range(n_heads):
            o = hd * HEAD_PAD
            q_ref[:, o:o + NOPE_DIM] = qf[:, o:o + NOPE_DIM].astype(BF16)
            q_ref[:, o + NOPE_DIM:o + HEAD_PAD] = _rope128(
                qf[:, o + NOPE_DIM:o + HEAD_PAD], rc, ra, rb).astype(BF16)
    else:
        for hd in range(n_heads):
            o = hd * HEAD_PAD
            qnope = qf[:, o:o + NOPE_DIM].astype(BF16)
            wkn = wkv_ref[:, o:o + NOPE_DIM]
            qo = hd * QS_HEAD
            q_ref[:, qo:qo + kv_lora] = _nt_dot(qnope, wkn).astype(BF16)
            q_ref[:, qo + kv_lora:qo + QS_HEAD] = _rope128(
                qf[:, o + NOPE_DIM:o + HEAD_PAD], rc, ra, rb).astype(BF16)


def _in_proj(x, attn_g, w_in_p, q_g, wq_p, kv_g, wkv_p, rc, ra, rb, *, tm, prompt, n_heads, q_lora, kv_lora,
             conv_ch):
    n, d = x.shape
    n_tab = rc.shape[0] // tm
    const = lambda i: (0, 0)
    row = lambda i: (i, 0)
    tab = lambda i: (i % n_tab, 0)
    single = pl.Buffered(1)
    in_specs = [
        pl.BlockSpec((tm, d), row),
        pl.BlockSpec((1, d), const),
        pl.BlockSpec(w_in_p.shape, const, pipeline_mode=single),
        pl.BlockSpec((1, q_lora), const),
        pl.BlockSpec(wq_p.shape, const, pipeline_mode=single),
        pl.BlockSpec((1, kv_lora), const),
        pl.BlockSpec(wkv_p.shape, const, pipeline_mode=single),
        pl.BlockSpec((tm, LANES), tab),
        pl.BlockSpec((tm, LANES), tab),
        pl.BlockSpec((tm, LANES), tab),
    ]
    q_cols = n_heads * (HEAD_PAD if prompt else QS_HEAD)
    out_shape = [
        jax.ShapeDtypeStruct((n, q_cols), BF16),
        jax.ShapeDtypeStruct((n, kv_lora), F32),
        jax.ShapeDtypeStruct((n, ROPE_DIM), F32),
        jax.ShapeDtypeStruct((n, conv_ch), F32),
    ]
    out_specs = [
        pl.BlockSpec((tm, q_cols), row),
        pl.BlockSpec((tm, kv_lora), row),
        pl.BlockSpec((tm, ROPE_DIM), row),
        pl.BlockSpec((tm, conv_ch), row),
    ]
    if prompt:
        out_shape += [jax.ShapeDtypeStruct((n, wkv_p.shape[1]), BF16), jax.ShapeDtypeStruct((n, LANES), BF16)]
        out_specs += [pl.BlockSpec((tm, wkv_p.shape[1]), row), pl.BlockSpec((tm, LANES), row)]
    kern = functools.partial(_in_proj_kernel, n_heads=n_heads, q_lora=q_lora, kv_lora=kv_lora, conv_ch=conv_ch,
                             prompt=prompt)
    return pl.pallas_call(
        kern, grid=(n // tm,), in_specs=in_specs, out_specs=out_specs, out_shape=out_shape,
        compiler_params=_cparams(("arbitrary",)),
        name="in_proj_prompt" if prompt else "in_proj_sample",
    )(x, attn_g, w_in_p, q_g, wq_p, kv_g, wkv_p, rc, ra, rb)


def _attn_prompt_kernel(q_ref, kv_ref, krb_ref, o_ref, kcat_ref, *, tq, tk, scale):
    i = pl.program_id(2)

    @pl.when(i == 0)
    def _():
        kcat_ref[:, :NOPE_DIM] = kv_ref[:, :NOPE_DIM]
        kcat_ref[:, NOPE_DIM:] = krb_ref[...]

    q = q_ref[...]
    v_dim = o_ref.shape[1]

    def scores(j):
        k = kcat_ref[pl.ds(pl.multiple_of(j * tk, tk), tk), :]
        return _nt_dot(q, k) * scale

    def update(j, s, carry):
        m, l, acc = carry
        m_new = jnp.maximum(m, jnp.max(s, axis=-1, keepdims=True))
        alpha = jnp.exp(m - m_new)
        p = jnp.exp(s - m_new)
        v = kv_ref[pl.ds(pl.multiple_of(j * tk, tk), tk), NOPE_DIM:]
        acc = acc * alpha + jnp.dot(p.astype(BF16), v, preferred_element_type=F32)
        return m_new, l * alpha + jnp.sum(p, axis=-1, keepdims=True), acc

    def body(j, carry):
        s_next = scores(j + 1)
        return update(j, carry[3], carry[:3]) + (s_next,)

    init = (jnp.full((tq, 1), NEG_INF, F32), jnp.zeros((tq, 1), F32), jnp.zeros((tq, v_dim), F32), scores(0))
    m, l, acc, s = lax.fori_loop(0, i, body, init)
    rowi = lax.broadcasted_iota(jnp.int32, (tq, tk), 0)
    coli = lax.broadcasted_iota(jnp.int32, (tq, tk), 1)
    m, l, acc = update(i, jnp.where(coli <= rowi, s, NEG_INF), (m, l, acc))
    o_ref[...] = (acc / l).astype(o_ref.dtype)


def _attn_prompt(q, kv, krb, *, batch, seq, n_heads, v_dim, scale, tq=512):
    n = q.shape[0]
    nq = seq // tq
    kern = functools.partial(_attn_prompt_kernel, tq=tq, tk=tq, scale=scale)
    return pl.pallas_call(
        kern, grid=(batch, n_heads, nq),
        in_specs=[
            pl.BlockSpec((tq, HEAD_PAD), lambda b, h, i: (b * nq + i, h)),
            pl.BlockSpec((seq, HEAD_PAD), lambda b, h, i: (b, h)),
            pl.BlockSpec((seq, LANES), lambda b, h, i: (b, 0)),
        ],
        out_specs=pl.BlockSpec((tq, v_dim), lambda b, h, i: (b * nq + i, h)),
        out_shape=jax.ShapeDtypeStruct((n, n_heads * v_dim), BF16),
        scratch_shapes=[pltpu.VMEM((seq, HEAD_PAD), BF16)],
        compiler_params=_cparams(("arbitrary", "arbitrary", "arbitrary")),
        name="attn_prompt",
    )(q, kv, krb)


def _ln_silu(y, g, b):
    yc = y - jnp.mean(y, axis=-1, keepdims=True)
    z = yc * lax.rsqrt(jnp.mean(yc * yc, axis=-1, keepdims=True) + EPS) * g + b
    return z * _sigmoid(z)


def _conv_prompt_kernel(halo_ref, u_ref, w_ref, b_ref, g_ref, beta_ref, o_ref, win_ref, y_ref, sh_ref, *, tt,
                        halo, width, rc):
    i = pl.program_id(1)

    @pl.when(i == 0)
    def _():
        win_ref[0:halo, :] = jnp.zeros((halo, win_ref.shape[1]), F32)

    @pl.when(i > 0)
    def _():
        win_ref[0:halo, :] = halo_ref[...]

    win_ref[halo:, :] = u_ref[...]
    base = halo - (width - 1)
    ch = win_ref.shape[1]
    n_sh = tt + halo - SUBLANES
    for lc in range(ch // LANES):
        ls = slice(lc * LANES, (lc + 1) * LANES)
        wl = w_ref[:, ls]
        for g in range(1, SUBLANES):
            sh_ref[g, 0:n_sh, :] = win_ref[g:g + n_sh, ls]

        def row_chunk(r, c, ls=ls, wl=wl):
            r0 = pl.multiple_of(r * rc, rc)
            acc = jnp.zeros((rc, LANES), F32)
            for k in range(width):
                g = (base + k) % SUBLANES
                s0 = r0 + ((base + k) - g)
                tap = win_ref[pl.ds(s0, rc), ls] if g == 0 else sh_ref[g, pl.ds(s0, rc), :]
                acc = acc + tap * wl[k:k + 1, :]
            y_ref[pl.ds(r0, rc), ls] = acc
            return c
        lax.fori_loop(0, tt // rc, row_chunk, 0)
    o_ref[...] = _ln_silu(y_ref[...] + b_ref[...], g_ref[...], beta_ref[...]).astype(o_ref.dtype)


def _conv_prompt(u, conv_w, conv_b, ln_g, ln_b, *, batch, seq, tt=256, halo=32, rc=64):
    n, ch = u.shape
    width = conv_w.shape[0]
    nt = seq // tt
    hb = tt // halo
    const = lambda b, i: (0, 0)
    kern = functools.partial(_conv_prompt_kernel, tt=tt, halo=halo, width=width, rc=rc)
    return pl.pallas_call(
        kern, grid=(batch, nt),
        in_specs=[
            pl.BlockSpec((halo, ch), lambda b, i: (jnp.maximum((b * nt + i) * hb - 1, 0), 0)),
            pl.BlockSpec((tt, ch), lambda b, i: (b * nt + i, 0)),
            pl.BlockSpec((width, ch), const),
            pl.BlockSpec((1, ch), const),
            pl.BlockSpec((1, ch), const),
            pl.BlockSpec((1, ch), const),
        ],
        out_specs=pl.BlockSpec((tt, ch), lambda b, i: (b * nt + i, 0)),
        out_shape=jax.ShapeDtypeStruct((n, ch), BF16),
        scratch_shapes=[pltpu.VMEM((halo + tt, ch), F32), pltpu.VMEM((tt, ch), F32),
                        pltpu.VMEM((SUBLANES, halo + tt, LANES), F32)],
        compiler_params=_cparams(("arbitrary", "arbitrary")),
        name="conv_prompt",
    )(u, u, conv_w, conv_b, ln_g, ln_b)


def _conv_sample_kernel(st_ref, u_ref, w_ref, b_ref, g_ref, beta_ref, o_ref, new_ref, *, width):
    w = w_ref[...]
    u = u_ref[...]
    y = u * w[width - 1:width, :] + b_ref[...]
    for k in range(width - 1):
        y = y + st_ref[k] * w[k:k + 1, :]
    o_ref[...] = _ln_silu(y, g_ref[...], beta_ref[...]).astype(o_ref.dtype)
    for k in range(width - 2):
        new_ref[k] = st_ref[k + 1]
    new_ref[width - 2] = u


def _conv_sample(state_t, u, conv_w, conv_b, ln_g, ln_b, *, tb=32):
    hist, nb, ch = state_t.shape
    width = conv_w.shape[0]
    const = lambda i: (0, 0)
    return pl.pallas_call(
        functools.partial(_conv_sample_kernel, width=width), grid=(nb // tb,),
        in_specs=[
            pl.BlockSpec((hist, tb, ch), lambda i: (0, i, 0)),
            pl.BlockSpec((tb, ch), lambda i: (i, 0)),
            pl.BlockSpec((width, ch), const),
            pl.BlockSpec((1, ch), const),
            pl.BlockSpec((1, ch), const),
            pl.BlockSpec((1, ch), const),
        ],
        out_specs=[pl.BlockSpec((tb, ch), lambda i: (i, 0)), pl.BlockSpec((hist, tb, ch), lambda i: (0, i, 0))],
        out_shape=[jax.ShapeDtypeStruct((nb, ch), BF16), jax.ShapeDtypeStruct((hist, nb, ch), F32)],
        compiler_params=_cparams(("arbitrary",)),
        name="conv_sample",
    )(state_t, u, conv_w, conv_b, ln_g, ln_b)


def _attn_sample_kernel(pt_ref, qs_ref, cn_ref, krn_ref, cache_c, cache_krt, o_ref, cbuf, rbuf, sem, m_ref, l_ref,
                        acc_ref, *, pages, n_chunks, page, kv_lora, sub, scale):
    b = pl.program_id(0)
    ch = pl.program_id(1)
    g = b * n_chunks + ch
    n_steps = pl.num_programs(0) * n_chunks
    slot = g % 2

    def copies(step, sl):
        out = []
        for p in range(pages):
            pid = pt_ref[step * pages + p]
            keys = pl.ds(p * page, page)
            out.append(pltpu.make_async_copy(cache_c.at[pid], cbuf.at[sl, keys, :], sem.at[0, sl]))
            out.append(pltpu.make_async_copy(cache_krt.at[pid], rbuf.at[sl, :, keys], sem.at[1, sl]))
        return out

    @pl.when(g == 0)
    def _():
        for cp in copies(0, 0):
            cp.start()

    @pl.when(g + 1 < n_steps)
    def _():
        for cp in copies(g + 1, 1 - slot):
            cp.start()

    @pl.when(ch == 0)
    def _():
        m_ref[...] = jnp.full(m_ref.shape, NEG_INF, F32)
        l_ref[...] = jnp.zeros(l_ref.shape, F32)
        acc_ref[...] = jnp.zeros(acc_ref.shape, F32)

    for cp in copies(g, slot):
        cp.wait()

    q = qs_ref[0]
    ql = q[:, :kv_lora]
    qr = q[:, kv_lora:kv_lora + ROPE_DIM]
    cbs, scores = [], []
    for k0 in range(0, pages * page, sub):
        cb = cbuf[slot, k0:k0 + sub, :].astype(BF16)
        rb = rbuf[slot, :, k0:k0 + sub].astype(BF16)
        cbs.append(cb)
        scores.append((_nt_dot(ql, cb) + jnp.dot(qr, rb, preferred_element_type=F32)) * scale)
    probs = []
    for s in scores:
        m_k = jnp.max(s, axis=-1, keepdims=True)
        p = jnp.exp(s - m_k)
        probs.append((m_k, jnp.sum(p, axis=-1, keepdims=True), p.astype(BF16)))
    parts = [(m_k, l_k, jnp.dot(p, cb, preferred_element_type=F32)) for (m_k, l_k, p), cb in zip(probs, cbs)]
    m_old = m_ref[...]
    m_new = m_old
    for m_k, _, _ in parts:
        m_new = jnp.maximum(m_new, m_k)
    alpha = jnp.exp(m_old - m_new)
    l_new = l_ref[...] * alpha
    acc = acc_ref[...] * alpha
    for m_k, l_k, a_k in parts:
        w_k = jnp.exp(m_k - m_new)
        l_new = l_new + l_k * w_k
        acc = acc + a_k * w_k
    l_ref[...] = l_new
    acc_ref[...] = acc
    m_ref[...] = m_new

    @pl.when(ch == n_chunks - 1)
    def _():
        cn = cn_ref[0].astype(BF16).astype(F32)
        krn = krn_ref[0].astype(BF16).astype(F32)
        s_new = (jnp.sum(ql.astype(F32) * cn, axis=-1, keepdims=True)
                 + jnp.sum(qr.astype(F32) * krn, axis=-1, keepdims=True)) * scale
        m_o = m_ref[...]
        m_n = jnp.maximum(m_o, s_new)
        al = jnp.exp(m_o - m_n)
        p_new = jnp.exp(s_new - m_n)
        l_fin = l_ref[...] * al + p_new
        acc = acc_ref[...] * al + p_new.astype(BF16).astype(F32) * cn
        o_ref[0] = acc / l_fin


def _attn_sample(page_table, qs3, c_new3, kr_new3, cache_c, cache_krt, *, scale, pages=32, sub=512):
    nb, n_heads, qw = qs3.shape
    n_pages = page_table.shape[1]
    page, kv_lora = cache_c.shape[1:]
    rope = cache_krt.shape[1]
    n_chunks = n_pages // pages
    keys = pages * page
    kern = functools.partial(_attn_sample_kernel, pages=pages, n_chunks=n_chunks, page=page, kv_lora=kv_lora,
                             sub=sub, scale=scale)
    grid_spec = pltpu.PrefetchScalarGridSpec(
        num_scalar_prefetch=1, grid=(nb, n_chunks),
        in_specs=[
            pl.BlockSpec((1, n_heads, qw), lambda b, c, pt: (b, 0, 0)),
            pl.BlockSpec((1, 1, kv_lora), lambda b, c, pt: (b, 0, 0)),
            pl.BlockSpec((1, 1, rope), lambda b, c, pt: (b, 0, 0)),
            pl.BlockSpec(memory_space=pl.ANY),
            pl.BlockSpec(memory_space=pl.ANY),
        ],
        out_specs=pl.BlockSpec((1, n_heads, kv_lora), lambda b, c, pt: (b, 0, 0)),
        scratch_shapes=[
            pltpu.VMEM((2, keys, kv_lora), F32),
            pltpu.VMEM((2, rope, keys), F32),
            pltpu.SemaphoreType.DMA((2, 2)),
            pltpu.VMEM((n_heads, 1), F32),
            pltpu.VMEM((n_heads, 1), F32),
            pltpu.VMEM((n_heads, kv_lora), F32),
        ],
    )
    return pl.pallas_call(
        kern, grid_spec=grid_spec,
        out_shape=jax.ShapeDtypeStruct((nb, n_heads, kv_lora), F32),
        compiler_params=_cparams(("arbitrary", "arbitrary")),
        name="attn_sample",
    )(page_table.reshape(-1), qs3, c_new3, kr_new3, cache_c, cache_krt)


def _v_up_kernel(ol_ref, wkv_ref, o_ref, *, n_heads, kv_lora, v_dim):
    for hd in range(n_heads):
        ol = ol_ref[:, hd * kv_lora:(hd + 1) * kv_lora].astype(BF16)
        wv = wkv_ref[:, hd * HEAD_PAD + NOPE_DIM:(hd + 1) * HEAD_PAD]
        o_ref[:, hd * v_dim:(hd + 1) * v_dim] = jnp.dot(ol, wv, preferred_element_type=F32).astype(o_ref.dtype)


def _v_up(o_lat2, wkv_p, *, n_heads, kv_lora, v_dim):
    nb = o_lat2.shape[0]
    return pl.pallas_call(
        functools.partial(_v_up_kernel, n_heads=n_heads, kv_lora=kv_lora, v_dim=v_dim),
        out_shape=jax.ShapeDtypeStruct((nb, n_heads * v_dim), BF16),
        compiler_params=_cparams(None),
        name="v_up_sample",
    )(o_lat2, wkv_p)


def _out_proj_kernel(ap_ref, bp_ref, xp_ref, as_ref, bs_ref, xs_ref, w_ref, o_ref, *, half, n_prompt_tiles):
    i = pl.program_id(0)

    def mix(a_ref, b_ref, x_ref):
        return (x_ref[...] + jnp.dot(a_ref[...], w_ref[0:half, :], preferred_element_type=F32)
                + jnp.dot(b_ref[...], w_ref[half:, :], preferred_element_type=F32))

    @pl.when(i < n_prompt_tiles)
    def _():
        o_ref[...] = mix(ap_ref, bp_ref, xp_ref)

    @pl.when(i == n_prompt_tiles)
    def _():
        o_ref[0:as_ref.shape[0], :] = mix(as_ref, bs_ref, xs_ref)


def _out_proj(a_p, b_p, x_p, a_s, b_s, x_s, w, *, tm):
    n_p, half = a_p.shape
    n_s = a_s.shape[0]
    d = w.shape[1]
    npt = n_p // tm
    assert n_s <= tm
    const = lambda i: (0, 0)
    prow = lambda i: (jnp.minimum(i, npt - 1), 0)
    return pl.pallas_call(
        functools.partial(_out_proj_kernel, half=half, n_prompt_tiles=npt), grid=(npt + 1,),
        in_specs=[
            pl.BlockSpec((tm, half), prow),
            pl.BlockSpec((tm, half), prow),
            pl.BlockSpec((tm, d), prow),
            pl.BlockSpec((n_s, half), const),
            pl.BlockSpec((n_s, half), const),
            pl.BlockSpec((n_s, d), const),
            pl.BlockSpec(w.shape, const, pipeline_mode=pl.Buffered(1)),
        ],
        out_specs=pl.BlockSpec((tm, d), lambda i: (i, 0)),
        out_shape=jax.ShapeDtypeStruct((n_p + n_s, d), F32),
        compiler_params=_cparams(("arbitrary",)),
        name="out_proj",
    )(a_p, b_p, x_p, a_s, b_s, x_s, w)


def _router_kernel(x_ref, g_ref, w_ref, bias_ref, info_ref, cnt_ref, run_ref, *, n_exp, n_grp, per_grp):
    i = pl.program_id(0)

    @pl.when(i == 0)
    def _():
        run_ref[...] = jnp.zeros(run_ref.shape, F32)

    tm = x_ref.shape[0]
    h = _rms(x_ref[...], g_ref[...]).astype(BF16)
    logits = jnp.dot(h, w_ref[...], preferred_element_type=F32) + bias_ref[...]
    lane_i = lax.broadcasted_iota(jnp.int32, (tm, LANES), 1)
    lane = lane_i.astype(F32)
    big = float(LANES)

    def first_max(vals):
        mx = jnp.max(vals, axis=-1, keepdims=True)
        idx = jnp.min(jnp.where(vals == mx, lane, big), axis=-1, keepdims=True)
        return mx, idx

    is_grp = (lane_i >= n_exp) & (lane_i < n_exp + n_grp)
    lg = jnp.where(is_grp, logits, NEG_INF)
    g_max, g_lane = first_max(lg)
    g_p = 1.0 / jnp.sum(jnp.where(is_grp, jnp.exp(lg - g_max), 0.0), axis=-1, keepdims=True)
    g_idx = g_lane - float(n_exp)
    lo = g_idx * float(per_grp)
    in_grp = (lane >= lo) & (lane < lo + float(per_grp))
    le = jnp.where(in_grp, logits, NEG_INF)
    m1, i1 = first_max(le)
    m2, i2 = first_max(jnp.where(lane == i1, NEG_INF, le))
    e2 = jnp.exp(m2 - m1)
    w1 = g_p / (1.0 + e2)
    w2 = g_p * e2 / (1.0 + e2)

    oh1 = lane == i1
    oh2 = lane == i2
    onehot = jnp.where(oh1 | oh2, 1.0, 0.0)
    r_i = lax.broadcasted_iota(jnp.int32, (tm, tm), 0)
    c_i = lax.broadcasted_iota(jnp.int32, (tm, tm), 1)
    lower = jnp.where(c_i < r_i, 1.0, 0.0).astype(BF16)
    before = run_ref[...] + jnp.dot(lower, onehot.astype(BF16), preferred_element_type=F32)
    rank1 = jnp.sum(jnp.where(oh1, before, 0.0), axis=-1, keepdims=True)
    rank2 = jnp.sum(jnp.where(oh2, before, 0.0), axis=-1, keepdims=True)
    run_ref[...] = run_ref[...] + jnp.sum(onehot, axis=0, keepdims=True)
    cnt_ref[...] = run_ref[...]

    info = jnp.where(lane_i == 0, i1, jnp.where(lane_i == 1, i2, jnp.where(lane_i == 2, rank1, jnp.where(
        lane_i == 3, rank2, jnp.where(lane_i == 4, w1, jnp.where(lane_i == 5, w2, 0.0))))))
    info_ref[...] = info


def _router(x1, ffn_g, w_r, bias, *, tm, n_exp, n_grp, per_grp):
    n, d = x1.shape
    const = lambda i: (0, 0)
    return pl.pallas_call(
        functools.partial(_router_kernel, n_exp=n_exp, n_grp=n_grp, per_grp=per_grp), grid=(n // tm,),
        in_specs=[
            pl.BlockSpec((tm, d), lambda i: (i, 0)),
            pl.BlockSpec((1, d), const),
            pl.BlockSpec((d, LANES), const),
            pl.BlockSpec((1, LANES), const),
        ],
        out_specs=[pl.BlockSpec((tm, LANES), lambda i: (i, 0)), pl.BlockSpec((1, LANES), const)],
        out_shape=[jax.ShapeDtypeStruct((n, LANES), F32), jax.ShapeDtypeStruct((1, LANES), F32)],
        scratch_shapes=[pltpu.VMEM((1, LANES), F32)],
        compiler_params=_cparams(("arbitrary",)),
        name="router",
    )(x1, ffn_g, w_r, bias)


def _pow2_pieces(limit):
    p = limit // 2
    while p >= 1:
        yield p
        p //= 2


def _dispatch_kernel(dest_ref, zrow_ref, zcnt_ref, pend_ref, x_ref, g_ref, xs_ref, hbuf, zbuf, sem, zsem, *, tm, sub):
    i = pl.program_id(0)
    n = pl.num_programs(0)
    slot = i % 2
    n_exp = zrow_ref.shape[0]
    n_sub_blocks = xs_ref.shape[0] // sub

    def row_copy(sl, r, d):
        return pltpu.make_async_copy(hbuf.at[sl, pl.ds(r, 1), :], xs_ref.at[pl.ds(d, 1), :], sem.at[sl])

    def wait_all(sl):
        for _ in range(TOP_K):
            pltpu.make_async_copy(hbuf.at[sl], xs_ref.at[pl.ds(0, tm), :], sem.at[sl]).wait()

    def zero_copy(row, size):
        return pltpu.make_async_copy(zbuf.at[pl.ds(0, size), :], xs_ref.at[pl.ds(row, size), :], zsem)

    def zero_fill(start):
        def per_expert(e, c):
            row = zrow_ref[e]
            cnt = zcnt_ref[e]
            head = (-row) & (SUBLANES - 1)
            for r in range(SUBLANES - 1):
                @pl.when(r < jnp.minimum(head, cnt))
                def _():
                    cp = zero_copy(row + r, 1)
                    cp.start() if start else cp.wait()
            rest = jnp.maximum(cnt - head, 0)
            row = pl.multiple_of(row + head, SUBLANES)
            for piece in _pow2_pieces(sub):
                if piece < SUBLANES:
                    break
                @pl.when((rest & piece) != 0)
                def _():
                    cp = zero_copy(row, piece)
                    cp.start() if start else cp.wait()
                row = pl.multiple_of(row + (rest & piece), SUBLANES)
            return c
        lax.fori_loop(0, n_exp, per_expert, 0)

        def per_block(b, c):
            @pl.when(b * sub >= pend_ref[0])
            def _():
                cp = zero_copy(b * sub, sub)
                cp.start() if start else cp.wait()
            return c
        lax.fori_loop(0, n_sub_blocks, per_block, 0)

    @pl.when(i == 0)
    def _():
        zbuf[...] = jnp.zeros(zbuf.shape, F32)
        zero_fill(True)

    hbuf[slot] = _rms(x_ref[...], g_ref[...])

    def issue(r, c):
        a = (i * tm + r) * TOP_K
        for k in range(TOP_K):
            row_copy(slot, r, dest_ref[a + k]).start()
        return c
    lax.fori_loop(0, tm, issue, 0, unroll=8)

    @pl.when(i == 0)
    def _():
        zero_fill(False)

    @pl.when(i > 0)
    def _():
        wait_all(1 - slot)

    @pl.when(i == n - 1)
    def _():
        wait_all(slot)


def _dispatch(dest, zrow, zcnt, pend, x1, ffn_g, *, tm, rows, sub):
    n, d = x1.shape
    grid_spec = pltpu.PrefetchScalarGridSpec(
        num_scalar_prefetch=4, grid=(n // tm,),
        in_specs=[
            pl.BlockSpec((tm, d), lambda i, *_: (i, 0)),
            pl.BlockSpec((1, d), lambda i, *_: (0, 0)),
        ],
        out_specs=pl.BlockSpec(memory_space=pl.ANY),
        scratch_shapes=[pltpu.VMEM((2, tm, d), F32), pltpu.VMEM((sub, d), F32), pltpu.SemaphoreType.DMA((2,)),
                        pltpu.SemaphoreType.DMA(())],
    )
    return pl.pallas_call(
        functools.partial(_dispatch_kernel, tm=tm, sub=sub), grid_spec=grid_spec,
        out_shape=jax.ShapeDtypeStruct((rows, d), F32),
        compiler_params=_cparams(("arbitrary",)),
        name="moe_dispatch",
    )(dest, zrow, zcnt, pend, x1, ffn_g)


def _experts_kernel(ie_ref, ir_ref, ins_ref, nit_ref, pend_ref, xs_ref, wg_hbm, wu_hbm, wd_hbm, ys_ref,
                    gu_ring, d_ring, xbuf, gacc, uacc, hbuf, ybuf, zbuf,
                    gu_sem, d_sem, x_sem, y_sem, z_sem, *, sub, n_sub_max, kc, fcs):
    n_items = nit_ref[0]
    d = xs_ref.shape[1]
    n_kc = d // kc
    n_fc = len(fcs)
    assert n_fc == 2 and n_kc >= 2
    item_rows = sub * n_sub_max
    f_offs = [sum(fcs[:j]) for j in range(n_fc)]

    def gu_copies(item, pos, slot):
        e = ie_ref[item]
        rows = pl.ds(pos * kc, kc)
        return (pltpu.make_async_copy(wg_hbm.at[e, rows, :], gu_ring.at[slot, 0], gu_sem.at[slot]),
                pltpu.make_async_copy(wu_hbm.at[e, rows, :], gu_ring.at[slot, 1], gu_sem.at[slot]))

    def d_copy(item, j):
        e = ie_ref[item]
        return pltpu.make_async_copy(wd_hbm.at[e, pl.ds(f_offs[j], fcs[j]), :], d_ring.at[j, pl.ds(0, fcs[j]), :],
                                     d_sem.at[j])

    def x_xfer(item, slot, start):
        n_sub = ins_ref[item]
        for sb in range(n_sub_max):
            @pl.when(sb < n_sub)
            def _():
                r = pl.multiple_of(ir_ref[item] + sb * sub, sub)
                for k in range(n_kc):
                    cp = pltpu.make_async_copy(xs_ref.at[pl.ds(r, sub), pl.ds(k * kc, kc)],
                                               xbuf.at[slot, k, pl.ds(sb * sub, sub), :], x_sem.at[slot])
                    cp.start() if start else cp.wait()

    def y_copy(item, sb):
        r = pl.multiple_of(ir_ref[item] + sb * sub, sub)
        return pltpu.make_async_copy(ybuf.at[pl.ds(sb * sub, sub), :], ys_ref.at[pl.ds(r, sub), :], y_sem)

    def for_rows(n_sub, fn):
        for ns in range(1, n_sub_max + 1):
            @pl.when(n_sub == ns)
            def _():
                fn(ns * sub)

    def y_writes(item, start):
        n_sub = ins_ref[item]
        for sb in range(n_sub_max):
            @pl.when(sb < n_sub)
            def _():
                cp = y_copy(item, sb)
                cp.start() if start else cp.wait()

    zbuf[...] = jnp.zeros(zbuf.shape, F32)
    n_out_blocks = ys_ref.shape[0] // sub

    def tail(start):
        def body(b, c):
            @pl.when(b * sub >= pend_ref[0])
            def _():
                cp = pltpu.make_async_copy(zbuf, ys_ref.at[pl.ds(b * sub, sub), :], z_sem)
                cp.start() if start else cp.wait()
            return c
        lax.fori_loop(0, n_out_blocks, body, 0)
    tail(True)

    @pl.when(n_items > 0)
    def _():
        x_xfer(0, 0, True)
        for pos in range(2):
            for cp in gu_copies(0, pos, pos):
                cp.start()

    def item_body(item, carry):
        xslot = item % 2
        n_sub = ins_ref[item]
        has_next = item + 1 < n_items

        @pl.when(has_next)
        def _():
            x_xfer(item + 1, 1 - xslot, True)

        x_xfer(item, xslot, False)

        def zero_acc(rows):
            gacc[0:rows, :] = jnp.zeros((rows, gacc.shape[1]), F32)
            uacc[0:rows, :] = jnp.zeros((rows, uacc.shape[1]), F32)
        for_rows(n_sub, zero_acc)

        def gu_step(pos, c):
            slot = pos % 2
            for cp in gu_copies(item, pos, slot):
                cp.wait()

            def mm(rows):
                xk = xbuf[xslot, pos, 0:rows, :].astype(BF16)
                gacc[0:rows, :] += jnp.dot(xk, gu_ring[slot, 0].astype(BF16), preferred_element_type=F32)
                uacc[0:rows, :] += jnp.dot(xk, gu_ring[slot, 1].astype(BF16), preferred_element_type=F32)
            for_rows(n_sub, mm)

            @pl.when(pos + 2 < n_kc)
            def _():
                for cp in gu_copies(item, pos + 2, slot):
                    cp.start()
            for j in range(n_fc):
                @pl.when(pos + 2 == n_kc + j)
                def _():
                    d_copy(item, j).start()
            return c
        lax.fori_loop(0, n_kc, gu_step, 0)

        def act(rows):
            g = gacc[0:rows, :]
            hbuf[0:rows, :] = (g * _sigmoid(g) * uacc[0:rows, :]).astype(BF16)
        for_rows(n_sub, act)

        @pl.when(item > 0)
        def _():
            y_writes(item - 1, False)

        for j in range(n_fc):
            d_copy(item, j).wait()

            def mm_down(rows, j=j):
                y = jnp.dot(hbuf[0:rows, f_offs[j]:f_offs[j] + fcs[j]], d_ring[j, 0:fcs[j], :].astype(BF16),
                            preferred_element_type=F32)
                if j == 0:
                    ybuf[0:rows, :] = y
                else:
                    ybuf[0:rows, :] += y
            for_rows(n_sub, mm_down)

            @pl.when(has_next)
            def _():
                for cp in gu_copies(item + 1, j, j):
                    cp.start()
        y_writes(item, True)
        return carry

    lax.fori_loop(0, n_items, item_body, 0)

    @pl.when(n_items > 0)
    def _():
        y_writes(n_items - 1, False)
    tail(False)


def _experts(item_e, item_row, item_nsub, n_items, pend, xs, w_gate, w_up, w_down, *, sub, n_sub_max):
    rows, d = xs.shape
    f = w_gate.shape[2]
    kc = 512
    lanes_f = f // LANES
    fcs = ((lanes_f + 1) // 2 * LANES, lanes_f // 2 * LANES)
    item_rows = sub * n_sub_max
    any_spec = pl.BlockSpec(memory_space=pl.ANY)
    kern = functools.partial(_experts_kernel, sub=sub, n_sub_max=n_sub_max, kc=kc, fcs=fcs)
    return pl.pallas_call(
        kern,
        grid_spec=pltpu.PrefetchScalarGridSpec(
            num_scalar_prefetch=5, grid=(1,),
            in_specs=[any_spec, any_spec, any_spec, any_spec],
            out_specs=any_spec,
            scratch_shapes=[
                pltpu.VMEM((2, 2, kc, f), F32),
                pltpu.VMEM((2, fcs[0], d), F32),
                pltpu.VMEM((2, d // kc, item_rows, kc), F32),
                pltpu.VMEM((item_rows, f), F32),
                pltpu.VMEM((item_rows, f), F32),
                pltpu.VMEM((item_rows, f), BF16),
                pltpu.VMEM((item_rows, d), F32),
                pltpu.VMEM((sub, d), F32),
                pltpu.SemaphoreType.DMA((2,)),
                pltpu.SemaphoreType.DMA((2,)),
                pltpu.SemaphoreType.DMA((2,)),
                pltpu.SemaphoreType.DMA(()),
                pltpu.SemaphoreType.DMA(()),
            ],
        ),
        out_shape=jax.ShapeDtypeStruct((rows, d), F32),
        compiler_params=_cparams(("arbitrary",), 60 * 1024 * 1024),
        name="moe_experts",
    )(item_e, item_row, item_nsub, n_items, pend, xs, w_gate, w_up, w_down)


def _combine_kernel(dest_ref, x_ref, info_ref, g_ref, ys_ref, yp_ref, ys_out_ref, rbuf, sem, *, tm, n_prompt_tiles):
    i = pl.program_id(0)
    n = pl.num_programs(0)
    slot = i % 2

    def row_copy(sl, k, r, d):
        return pltpu.make_async_copy(ys_ref.at[pl.ds(d, 1), :], rbuf.at[sl, k, pl.ds(r, 1), :], sem.at[sl])

    def issue(step, sl):
        def body(r, c):
            a = (step * tm + r) * TOP_K
            for k in range(TOP_K):
                row_copy(sl, k, r, dest_ref[a + k]).start()
            return c
        lax.fori_loop(0, tm, body, 0, unroll=8)

    @pl.when(i == 0)
    def _():
        issue(0, 0)

    @pl.when(i + 1 < n)
    def _():
        issue(i + 1, 1 - slot)

    for k in range(TOP_K):
        pltpu.make_async_copy(ys_ref.at[pl.ds(0, tm), :], rbuf.at[slot, k], sem.at[slot]).wait()

    info = info_ref[...]
    lane_i = lax.broadcasted_iota(jnp.int32, info.shape, 1)
    w1 = jnp.sum(jnp.where(lane_i == 4, info, 0.0), axis=-1, keepdims=True)
    w2 = jnp.sum(jnp.where(lane_i == 5, info, 0.0), axis=-1, keepdims=True)
    y = x_ref[...] + (rbuf[slot, 0] * w1 + rbuf[slot, 1] * w2)
    out = _rms(y, g_ref[...])

    @pl.when(i < n_prompt_tiles)
    def _():
        yp_ref[...] = out

    @pl.when(i >= n_prompt_tiles)
    def _():
        ys_out_ref[...] = out


def _combine(dest, x1, info, final_g, ys, *, tm, n_prompt, n_sample):
    n, d = x1.shape
    npt = n_prompt // tm
    grid_spec = pltpu.PrefetchScalarGridSpec(
        num_scalar_prefetch=1, grid=(n // tm,),
        in_specs=[
            pl.BlockSpec((tm, d), lambda i, ds: (i, 0)),
            pl.BlockSpec((tm, LANES), lambda i, ds: (i, 0)),
            pl.BlockSpec((1, d), lambda i, ds: (0, 0)),
            pl.BlockSpec(memory_space=pl.ANY),
        ],
        out_specs=[
            pl.BlockSpec((tm, d), lambda i, ds: (jnp.minimum(i, npt - 1), 0)),
            pl.BlockSpec((tm, d), lambda i, ds: (jnp.maximum(i - npt, 0), 0)),
        ],
        scratch_shapes=[pltpu.VMEM((2, TOP_K, tm, d), F32), pltpu.SemaphoreType.DMA((2,))],
    )
    return pl.pallas_call(
        functools.partial(_combine_kernel, tm=tm, n_prompt_tiles=npt), grid_spec=grid_spec,
        out_shape=[jax.ShapeDtypeStruct((n_prompt, d), F32), jax.ShapeDtypeStruct((n_sample, d), F32)],
        compiler_params=_cparams(("arbitrary",)),
        name="moe_combine",
    )(dest, x1, info, final_g, ys)


def _rope_tables(pos):
    half = ROPE_DIM // 2
    inv_freq = ROPE_BASE ** (-jnp.arange(half, dtype=F32) / half)
    ang = pos.astype(F32)[:, None] * inv_freq[None, :]
    cos, sin = jnp.cos(ang), jnp.sin(ang)
    z = jnp.zeros_like(cos)
    z2 = jnp.concatenate([z, z], axis=1)
    return (jnp.concatenate([cos, cos, z2], axis=1), jnp.concatenate([-sin, z, z2], axis=1),
            jnp.concatenate([z, sin, z2], axis=1))


def kernel(x_prompt, x_sample, cache_kv_latent, cache_k_rope, state_conv, page_table, attn_norm_g, w_in, q_norm_g,
           w_q_up, kv_norm_g, w_kv_up, conv_w, conv_b, conv_ln_g, conv_ln_b, w_out, ffn_norm_g, w_router_group,
           b_router_group, w_router_expert, b_router_expert, w_exp_gate, w_exp_up, w_exp_down, final_norm_g):
    batch, seq, d = x_prompt.shape
    nb, t_dec, _ = x_sample.shape
    depth, q_lora, n_heads, qk_dim = w_q_up.shape
    kv_lora = w_kv_up.shape[1]
    v_dim = w_kv_up.shape[3] - NOPE_DIM
    conv_ch = conv_w.shape[2]
    width = conv_w.shape[1]
    n_exp = w_exp_gate.shape[1]
    n_grp = w_router_group.shape[2]
    per_grp = n_exp // n_grp
    page = cache_kv_latent.shape[2]
    past = page_table.shape[1] * page
    assert depth == 1 and t_dec == 1 and qk_dim == NOPE_DIM + ROPE_DIM and v_dim == NOPE_DIM
    scale = float(qk_dim) ** -0.5
    n_p = batch * seq
    n_tok = n_p + nb

    w_in_p = w_in[0].astype(BF16)
    wq_p = jnp.pad(w_q_up[0], ((0, 0), (0, 0), (0, HEAD_PAD - qk_dim))).reshape(q_lora, n_heads * HEAD_PAD)
    wq_p = wq_p.astype(BF16)
    wkv_p = w_kv_up[0].reshape(kv_lora, n_heads * HEAD_PAD).astype(BF16)
    w_out_b = w_out[0].astype(BF16)
    w_r = jnp.concatenate([w_router_expert[0], w_router_group[0],
                           jnp.zeros((d, LANES - n_exp - n_grp), F32)], axis=1).astype(BF16)
    b_r = jnp.concatenate([b_router_expert[0], b_router_group[0], jnp.zeros((LANES - n_exp - n_grp,), F32)])[None]
    row2 = lambda v: v.reshape(1, -1)

    tm_p = 512
    rc_p, ra_p, rb_p = _rope_tables(jnp.arange(seq))
    rc_s, ra_s, rb_s = _rope_tables(jnp.full((nb,), past, jnp.int32))
    dims = dict(n_heads=n_heads, q_lora=q_lora, kv_lora=kv_lora, conv_ch=conv_ch)
    xp2 = x_prompt.reshape(n_p, d)
    xs2 = x_sample.reshape(nb, d)
    q_p, c_p, kr_p, u_p, kv_p, krb_p = _in_proj(
        xp2, row2(attn_norm_g[0]), w_in_p, row2(q_norm_g[0]), wq_p, row2(kv_norm_g[0]), wkv_p, rc_p, ra_p, rb_p,
        tm=tm_p, prompt=True, **dims)
    qs_s, c_s, kr_s, u_s = _in_proj(
        xs2, row2(attn_norm_g[0]), w_in_p, row2(q_norm_g[0]), wq_p, row2(kv_norm_g[0]), wkv_p, rc_s, ra_s, rb_s,
        tm=nb, prompt=False, **dims)

    o_att_p = _attn_prompt(q_p, kv_p, krb_p, batch=batch, seq=seq, n_heads=n_heads, v_dim=v_dim, scale=scale)
    cw, cb, cg, cbeta = conv_w[0], row2(conv_b[0]), row2(conv_ln_g[0]), row2(conv_ln_b[0])
    o_conv_p = _conv_prompt(u_p, cw, cb, cg, cbeta, batch=batch, seq=seq)

    o_lat = _attn_sample(page_table, qs_s.reshape(nb, n_heads, QS_HEAD), c_s.reshape(nb, 1, kv_lora),
                         kr_s.reshape(nb, 1, ROPE_DIM), cache_kv_latent[0],
                         jnp.transpose(cache_k_rope[0], (0, 2, 1)), scale=scale)
    o_att_s = _v_up(o_lat.reshape(nb, n_heads * kv_lora), wkv_p, n_heads=n_heads, kv_lora=kv_lora, v_dim=v_dim)
    state_t = jnp.transpose(state_conv[0], (1, 0, 2))
    o_conv_s, new_state_t = _conv_sample(state_t, u_s, cw, cb, cg, cbeta)

    x1 = _out_proj(o_att_p, o_conv_p, xp2, o_att_s, o_conv_s, xs2, w_out_b, tm=512)

    info, counts = _router(x1, row2(ffn_norm_g[0]), w_r, b_r, tm=TOK_TILE, n_exp=n_exp, n_grp=n_grp,
                           per_grp=per_grp)
    i32 = jnp.int32
    sub, item_rows = MOE_SUB, MOE_SUB * MOE_ITEM_SUBS
    n_assign = n_tok * TOP_K
    rows_alloc = (-(-n_assign // sub) + n_exp) * sub
    max_items = n_assign // item_rows + n_exp + 1
    cnt = counts[0, :n_exp].astype(i32)
    padded = (cnt + sub - 1) // sub * sub
    pad_end = jnp.cumsum(padded)
    pad_start = pad_end - padded
    e_flat = info[:, 0:TOP_K].astype(i32).reshape(-1)
    rank = info[:, TOP_K:2 * TOP_K].astype(i32).reshape(-1)
    is_e = e_flat[:, None] == jnp.arange(n_exp, dtype=i32)[None, :]
    dest = rank + jnp.sum(jnp.where(is_e, pad_start[None, :], 0), axis=1)
    items_per_e = (padded + item_rows - 1) // item_rows
    it_end = jnp.cumsum(items_per_e)
    it_start = it_end - items_per_e
    w_ids = jnp.arange(max_items, dtype=i32)
    item_e = jnp.minimum(jnp.sum((it_end[None, :] <= w_ids[:, None]).astype(i32), axis=1), n_exp - 1)
    item_i = w_ids - it_start[item_e]
    item_row = jnp.clip(pad_start[item_e] + item_rows * item_i, 0, rows_alloc - item_rows).astype(i32)
    item_nsub = jnp.clip((padded[item_e] - item_rows * item_i) // sub, 1, MOE_ITEM_SUBS).astype(i32)
    n_items = it_end[-1:].astype(i32)
    pend = pad_end[-1:].astype(i32)

    xs = _dispatch(dest, (pad_start + cnt).astype(i32), (padded - cnt).astype(i32), pend, x1, row2(ffn_norm_g[0]),
                   tm=TOK_TILE, rows=rows_alloc, sub=sub)
    ys = _experts(item_e, item_row, item_nsub, n_items, pend, xs, w_exp_gate[0], w_exp_up[0], w_exp_down[0],
                  sub=sub, n_sub_max=MOE_ITEM_SUBS)
    y_p, y_s = _combine(dest, x1, info, row2(final_norm_g), ys, tm=CMB_TILE, n_prompt=n_p, n_sample=nb)

    hist = width - 1
    new_conv_p = u_p.reshape(batch, seq, conv_ch)[:, seq - hist:]
    new_conv_s = jnp.transpose(new_state_t, (1, 0, 2))
    return (y_p.reshape(batch, seq, d), y_s.reshape(nb, t_dec, d),
            c_p.reshape(1, batch, seq, kv_lora), kr_p.reshape(1, batch, seq, ROPE_DIM), new_conv_p[None],
            c_s.reshape(1, nb, t_dec, kv_lora), kr_s.reshape(1, nb, t_dec, ROPE_DIM), new_conv_s[None])
```

```python
import functools

import jax
import jax.numpy as jnp
from jax import lax
from jax.experimental import pallas as pl
from jax.experimental.pallas import tpu as pltpu

F32 = jnp.float32
BF16 = jnp.bfloat16

EPS = 1e-6
NEG_INF = -1e30
ROPE_BASE = 10000.0

LANES = 128
SUBLANES = 8
VMEM_LIMIT = 56 * 1024 * 1024

NOPE_DIM = 128
ROPE_DIM = 64
TOP_K = 2
HEAD_PAD = 256
QS_HEAD = 640

MOE_SUB = 128
MOE_ITEM_SUBS = 4
TOK_TILE = 416
CMB_TILE = 128


def _cparams(sem, vmem=VMEM_LIMIT):
    return pltpu.CompilerParams(dimension_semantics=sem, vmem_limit_bytes=vmem)


def _rms(x, g):
    return x * lax.rsqrt(jnp.mean(x * x, axis=-1, keepdims=True) + EPS) * g


def _rope128(v, c, a, b):
    return v * c + pltpu.roll(v, 96, 1) * a + pltpu.roll(v, 32, 1) * b


def _sigmoid(x):
    return 1.0 / (1.0 + jnp.exp(-x))


def _nt_dot(a, b):
    return lax.dot_general(a, b, (((1,), (1,)), ((), ())), preferred_element_type=F32)


def _in_proj_kernel(*refs, n_heads, q_lora, kv_lora, conv_ch, prompt):
    (x_ref, g_ref, w_ref, qg_ref, wq_ref, kvg_ref, wkv_ref, rc_ref, ra_ref, rb_ref) = refs[:10]
    outs = refs[10:]
    if prompt:
        q_ref, c_ref, kr_ref, u_ref, kv_ref, krb_ref = outs
    else:
        q_ref, c_ref, kr_ref, u_ref = outs
    o_kv = q_lora
    o_kr = q_lora + kv_lora
    o_g = o_kr + conv_ch
    n_in = o_kr + ROPE_DIM + 2 * conv_ch

    h = _rms(x_ref[...], g_ref[...]).astype(BF16)
    rc, ra, rb = rc_ref[...], ra_ref[...], rb_ref[...]

    zq = jnp.dot(h, w_ref[:, 0:q_lora], preferred_element_type=F32)
    qn = _rms(zq, qg_ref[...]).astype(BF16)
    qf = jnp.dot(qn, wq_ref[...], preferred_element_type=F32)

    zkv = jnp.dot(h, w_ref[:, o_kv:o_kr], preferred_element_type=F32)
    c = _rms(zkv, kvg_ref[...])
    c_ref[...] = c

    za = jnp.dot(h, w_ref[:, o_kr:o_g + LANES], preferred_element_type=F32)
    zg = jnp.dot(h, w_ref[:, o_g:n_in], preferred_element_type=F32)
    kr = _rope128(za[:, 0:LANES], rc, ra, rb)
    kr_ref[...] = kr[:, :ROPE_DIM]
    glu = za[:, 0:ROPE_DIM + conv_ch] * _sigmoid(zg)
    u_ref[...] = glu[:, ROPE_DIM:]

    if prompt:
        krb_ref[...] = kr.astype(BF16)
        kv_ref[...] = jnp.dot(c.astype(BF16), wkv_ref[...], preferred_element_type=F32).astype(BF16)
        for hd in range(n_heads):
            o = hd * HEAD_PAD
            q_ref[:, o:o + NOPE_DIM] = qf[:, o:o + NOPE_DIM].astype(BF16)
            q_ref[:, o + NOPE_DIM:o + HEAD_PAD] = _rope128(
                qf[:, o + NOPE_DIM:o + HEAD_PAD], rc, ra, rb).astype(BF16)
    else:
        for hd in range(n_heads):
            o = hd * HEAD_PAD
            qnope = qf[:, o:o + NOPE_DIM].astype(BF16)
            wkn = wkv_ref[:, o:o + NOPE_DIM]
            qo = hd * QS_HEAD
            q_ref[:, qo:qo + kv_lora] = _nt_dot(qnope, wkn).astype(BF16)
            q_ref[:, qo + kv_lora:qo + QS_HEAD] = _rope128(
                qf[:, o + NOPE_DIM:o + HEAD_PAD], rc, ra, rb).astype(BF16)


def _in_proj(x, attn_g, w_in_p, q_g, wq_p, kv_g, wkv_p, rc, ra, rb, *, tm, prompt, n_heads, q_lora, kv_lora,
             conv_ch):
    n, d = x.shape
    n_tab = rc.shape[0] // tm
    const = lambda i: (0, 0)
    row = lambda i: (i, 0)
    tab = lambda i: (i % n_tab, 0)
    single = pl.Buffered(1)
    in_specs = [
        pl.BlockSpec((tm, d), row),
        pl.BlockSpec((1, d), const),
        pl.BlockSpec(w_in_p.shape, const, pipeline_mode=single),
        pl.BlockSpec((1, q_lora), const),
        pl.BlockSpec(wq_p.shape, const, pipeline_mode=single),
        pl.BlockSpec((1, kv_lora), const),
        pl.BlockSpec(wkv_p.shape, const, pipeline_mode=single),
        pl.BlockSpec((tm, LANES), tab),
        pl.BlockSpec((tm, LANES), tab),
        pl.BlockSpec((tm, LANES), tab),
    ]
    q_cols = n_heads * (HEAD_PAD if prompt else QS_HEAD)
    out_shape = [
        jax.ShapeDtypeStruct((n, q_cols), BF16),
        jax.ShapeDtypeStruct((n, kv_lora), F32),
        jax.ShapeDtypeStruct((n, ROPE_DIM), F32),
        jax.ShapeDtypeStruct((n, conv_ch), F32),
    ]
    out_specs = [
        pl.BlockSpec((tm, q_cols), row),
        pl.BlockSpec((tm, kv_lora), row),
        pl.BlockSpec((tm, ROPE_DIM), row),
        pl.BlockSpec((tm, conv_ch), row),
    ]
    if prompt:
        out_shape += [jax.ShapeDtypeStruct((n, wkv_p.shape[1]), BF16), jax.ShapeDtypeStruct((n, LANES), BF16)]
        out_specs += [pl.BlockSpec((tm, wkv_p.shape[1]), row), pl.BlockSpec((tm, LANES), row)]
    kern = functools.partial(_in_proj_kernel, n_heads=n_heads, q_lora=q_lora, kv_lora=kv_lora, conv_ch=conv_ch,
                             prompt=prompt)
    return pl.pallas_call(
        kern, grid=(n // tm,), in_specs=in_specs, out_specs=out_specs, out_shape=out_shape,
        compiler_params=_cparams(("arbitrary",)),
        name="in_proj_prompt" if prompt else "in_proj_sample",
    )(x, attn_g, w_in_p, q_g, wq_p, kv_g, wkv_p, rc, ra, rb)


def _attn_prompt_kernel(q_ref, kv_ref, krb_ref, o_ref, kcat_ref, *, tq, tk, scale):
    i = pl.program_id(2)

    @pl.when(i == 0)
    def _():
        kcat_ref[:, :NOPE_DIM] = kv_ref[:, :NOPE_DIM]
        kcat_ref[:, NOPE_DIM:] = krb_ref[...]

    q = q_ref[...]
    v_dim = o_ref.shape[1]

    def scores(j):
        k = kcat_ref[pl.ds(pl.multiple_of(j * tk, tk), tk), :]
        return _nt_dot(q, k) * scale

    def update(j, s, carry):
        m, l, acc = carry
        m_new = jnp.maximum(m, jnp.max(s, axis=-1, keepdims=True))
        alpha = jnp.exp(m - m_new)
        p = jnp.exp(s - m_new)
        v = kv_ref[pl.ds(pl.multiple_of(j * tk, tk), tk), NOPE_DIM:]
        acc = acc * alpha + jnp.dot(p.astype(BF16), v, preferred_element_type=F32)
        return m_new, l * alpha + jnp.sum(p, axis=-1, keepdims=True), acc

    def body(j, carry):
        s_next = scores(j + 1)
        return update(j, carry[3], carry[:3]) + (s_next,)

    init = (jnp.full((tq, 1), NEG_INF, F32), jnp.zeros((tq, 1), F32), jnp.zeros((tq, v_dim), F32), scores(0))
    m, l, acc, s = lax.fori_loop(0, i, body, init)
    rowi = lax.broadcasted_iota(jnp.int32, (tq, tk), 0)
    coli = lax.broadcasted_iota(jnp.int32, (tq, tk), 1)
    m, l, acc = update(i, jnp.where(coli <= rowi, s, NEG_INF), (m, l, acc))
    o_ref[...] = (acc / l).astype(o_ref.dtype)


def _attn_prompt(q, kv, krb, *, batch, seq, n_heads, v_dim, scale, tq=512):
    n = q.shape[0]
    nq = seq // tq
    kern = functools.partial(_attn_prompt_kernel, tq=tq, tk=tq, scale=scale)
    return pl.pallas_call(
        kern, grid=(batch, n_heads, nq),
        in_specs=[
            pl.BlockSpec((tq, HEAD_PAD), lambda b, h, i: (b * nq + i, h)),
            pl.BlockSpec((seq, HEAD_PAD), lambda b, h, i: (b, h)),
            pl.BlockSpec((seq, LANES), lambda b, h, i: (b, 0)),
        ],
        out_specs=pl.BlockSpec((tq, v_dim), lambda b, h, i: (b * nq + i, h)),
        out_shape=jax.ShapeDtypeStruct((n, n_heads * v_dim), BF16),
        scratch_shapes=[pltpu.VMEM((seq, HEAD_PAD), BF16)],
        compiler_params=_cparams(("arbitrary", "arbitrary", "arbitrary")),
        name="attn_prompt",
    )(q, kv, krb)


def _ln_silu(y, g, b):
    yc = y - jnp.mean(y, axis=-1, keepdims=True)
    z = yc * lax.rsqrt(jnp.mean(yc * yc, axis=-1, keepdims=True) + EPS) * g + b
    return z * _sigmoid(z)


def _conv_prompt_kernel(halo_ref, u_ref, w_ref, b_ref, g_ref, beta_ref, o_ref, win_ref, y_ref, sh_ref, *, tt,
                        halo, width, rc):
    i = pl.program_id(1)

    @pl.when(i == 0)
    def _():
        win_ref[0:halo, :] = jnp.zeros((halo, win_ref.shape[1]), F32)

    @pl.when(i > 0)
    def _():
        win_ref[0:halo, :] = halo_ref[...]

    win_ref[halo:, :] = u_ref[...]
    base = halo - (width - 1)
    ch = win_ref.shape[1]
    n_sh = tt + halo - SUBLANES
    for lc in range(ch // LANES):
        ls = slice(lc * LANES, (lc + 1) * LANES)
        wl = w_ref[:, ls]
        for g in range(1, SUBLANES):
            sh_ref[g, 0:n_sh, :] = win_ref[g:g + n_sh, ls]

        def row_chunk(r, c, ls=ls, wl=wl):
            r0 = pl.multiple_of(r * rc, rc)
            acc = jnp.zeros((rc, LANES), F32)
            for k in range(width):
                g = (base + k) % SUBLANES
                s0 = r0 + ((base + k) - g)
                tap = win_ref[pl.ds(s0, rc), ls] if g == 0 else sh_ref[g, pl.ds(s0, rc), :]
                acc = acc + tap * wl[k:k + 1, :]
            y_ref[pl.ds(r0, rc), ls] = acc
            return c
        lax.fori_loop(0, tt // rc, row_chunk, 0)
    o_ref[...] = _ln_silu(y_ref[...] + b_ref[...], g_ref[...], beta_ref[...]).astype(o_ref.dtype)


def _conv_prompt(u, conv_w, conv_b, ln_g, ln_b, *, batch, seq, tt=256, halo=32, rc=64):
    n, ch = u.shape
    width = conv_w.shape[0]
    nt = seq // tt
    hb = tt // halo
    const = lambda b, i: (0, 0)
    kern = functools.partial(_conv_prompt_kernel, tt=tt, halo=halo, width=width, rc=rc)
    return pl.pallas_call(
        kern, grid=(batch, nt),
        in_specs=[
            pl.BlockSpec((halo, ch), lambda b, i: (jnp.maximum((b * nt + i) * hb - 1, 0), 0)),
            pl.BlockSpec((tt, ch), lambda b, i: (b * nt + i, 0)),
            pl.BlockSpec((width, ch), const),
            pl.BlockSpec((1, ch), const),
            pl.BlockSpec((1, ch), const),
            pl.BlockSpec((1, ch), const),
        ],
        out_specs=pl.BlockSpec((tt, ch), lambda b, i: (b * nt + i, 0)),
        out_shape=jax.ShapeDtypeStruct((n, ch), BF16),
        scratch_shapes=[pltpu.VMEM((halo + tt, ch), F32), pltpu.VMEM((tt, ch), F32),
                        pltpu.VMEM((SUBLANES, halo + tt, LANES), F32)],
        compiler_params=_cparams(("arbitrary", "arbitrary")),
        name="conv_prompt",
    )(u, u, conv_w, conv_b, ln_g, ln_b)


def _conv_sample_kernel(st_ref, u_ref, w_ref, b_ref, g_ref, beta_ref, o_ref, new_ref, *, width):
    w = w_ref[...]
    u = u_ref[...]
    y = u * w[width - 1:width, :] + b_ref[...]
    for k in range(width - 1):
        y = y + st_ref[k] * w[k:k + 1, :]
    o_ref[...] = _ln_silu(y, g_ref[...], beta_ref[...]).astype(o_ref.dtype)
    for k in range(width - 2):
        new_ref[k] = st_ref[k + 1]
    new_ref[width - 2] = u


def _conv_sample(state_t, u, conv_w, conv_b, ln_g, ln_b, *, tb=32):
    hist, nb, ch = state_t.shape
    width = conv_w.shape[0]
    const = lambda i: (0, 0)
    return pl.pallas_call(
        functools.partial(_conv_sample_kernel, width=width), grid=(nb // tb,),
        in_specs=[
            pl.BlockSpec((hist, tb, ch), lambda i: (0, i, 0)),
            pl.BlockSpec((tb, ch), lambda i: (i, 0)),
            pl.BlockSpec((width, ch), const),
            pl.BlockSpec((1, ch), const),
            pl.BlockSpec((1, ch), const),
            pl.BlockSpec((1, ch), const),
        ],
        out_specs=[pl.BlockSpec((tb, ch), lambda i: (i, 0)), pl.BlockSpec((hist, tb, ch), lambda i: (0, i, 0))],
        out_shape=[jax.ShapeDtypeStruct((nb, ch), BF16), jax.ShapeDtypeStruct((hist, nb, ch), F32)],
        compiler_params=_cparams(("arbitrary",)),
        name="conv_sample",
    )(state_t, u, conv_w, conv_b, ln_g, ln_b)


def _attn_sample_kernel(pt_ref, qs_ref, cn_ref, krn_ref, cache_c, cache_krt, o_ref, cbuf, rbuf, sem, m_ref, l_ref,
                        acc_ref, *, pages, n_chunks, page, kv_lora, sub, scale):
    b = pl.program_id(0)
    ch = pl.program_id(1)
    g = b * n_chunks + ch
    n_steps = pl.num_programs(0) * n_chunks
    slot = g % 2

    def copies(step, sl):
        out = []
        for p in range(pages):
            pid = pt_ref[step * pages + p]
            keys = pl.ds(p * page, page)
            out.append(pltpu.make_async_copy(cache_c.at[pid], cbuf.at[sl, keys, :], sem.at[0, sl]))
            out.append(pltpu.make_async_copy(cache_krt.at[pid], rbuf.at[sl, :, keys], sem.at[1, sl]))
        return out

    @pl.when(g == 0)
    def _():
        for cp in copies(0, 0):
            cp.start()

    @pl.when(g + 1 < n_steps)
    def _():
        for cp in copies(g + 1, 1 - slot):
            cp.start()

    @pl.when(ch == 0)
    def _():
        m_ref[...] = jnp.full(m_ref.shape, NEG_INF, F32)
        l_ref[...] = jnp.zeros(l_ref.shape, F32)
        acc_ref[...] = jnp.zeros(acc_ref.shape, F32)

    for cp in copies(g, slot):
        cp.wait()

    q = qs_ref[0]
    ql = q[:, :kv_lora]
    qr = q[:, kv_lora:kv_lora + ROPE_DIM]
    cbs, scores = [], []
    for k0 in range(0, pages * page, sub):
        cb = cbuf[slot, k0:k0 + sub, :].astype(BF16)
        rb = rbuf[slot, :, k0:k0 + sub].astype(BF16)
        cbs.append(cb)
        scores.append((_nt_dot(ql, cb) + jnp.dot(qr, rb, preferred_element_type=F32)) * scale)
    probs = []
    for s in scores:
        m_k = jnp.max(s, axis=-1, keepdims=True)
        p = jnp.exp(s - m_k)
        probs.append((m_k, jnp.sum(p, axis=-1, keepdims=True), p.astype(BF16)))
    parts = [(m_k, l_k, jnp.dot(p, cb, preferred_element_type=F32)) for (m_k, l_k, p), cb in zip(probs, cbs)]
    m_old = m_ref[...]
    m_new = m_old
    for m_k, _, _ in parts:
        m_new = jnp.maximum(m_new, m_k)
    alpha = jnp.exp(m_old - m_new)
    l_new = l_ref[...] * alpha
    acc = acc_ref[...] * alpha
    for m_k, l_k, a_k in parts:
        w_k = jnp.exp(m_k - m_new)
        l_new = l_new + l_k * w_k
        acc = acc + a_k * w_k
    l_ref[...] = l_new
    acc_ref[...] = acc
    m_ref[...] = m_new

    @pl.when(ch == n_chunks - 1)
    def _():
        cn = cn_ref[0].astype(BF16).astype(F32)
        krn = krn_ref[0].astype(BF16).astype(F32)
        s_new = (jnp.sum(ql.astype(F32) * cn, axis=-1, keepdims=True)
                 + jnp.sum(qr.astype(F32) * krn, axis=-1, keepdims=True)) * scale
        m_o = m_ref[...]
        m_n = jnp.maximum(m_o, s_new)
        al = jnp.exp(m_o - m_n)
        p_new = jnp.exp(s_new - m_n)
        l_fin = l_ref[...] * al + p_new
        acc = acc_ref[...] * al + p_new.astype(BF16).astype(F32) * cn
        o_ref[0] = acc / l_fin


def _attn_sample(page_table, qs3, c_new3, kr_new3, cache_c, cache_krt, *, scale, pages=32, sub=512):
    nb, n_heads, qw = qs3.shape
    n_pages = page_table.shape[1]
    page, kv_lora = cache_c.shape[1:]
    rope = cache_krt.shape[1]
    n_chunks = n_pages // pages
    keys = pages * page
    kern = functools.partial(_attn_sample_kernel, pages=pages, n_chunks=n_chunks, page=page, kv_lora=kv_lora,
                             sub=sub, scale=scale)
    grid_spec = pltpu.PrefetchScalarGridSpec(
        num_scalar_prefetch=1, grid=(nb, n_chunks),
        in_specs=[
            pl.BlockSpec((1, n_heads, qw), lambda b, c, pt: (b, 0, 0)),
            pl.BlockSpec((1, 1, kv_lora), lambda b, c, pt: (b, 0, 0)),
            pl.BlockSpec((1, 1, rope), lambda b, c, pt: (b, 0, 0)),
            pl.BlockSpec(memory_space=pl.ANY),
            pl.BlockSpec(memory_space=pl.ANY),
        ],
        out_specs=pl.BlockSpec((1, n_heads, kv_lora), lambda b, c, pt: (b, 0, 0)),
        scratch_shapes=[
            pltpu.VMEM((2, keys, kv_lora), F32),
            pltpu.VMEM((2, rope, keys), F32),
            pltpu.SemaphoreType.DMA((2, 2)),
            pltpu.VMEM((n_heads, 1), F32),
            pltpu.VMEM((n_heads, 1), F32),
            pltpu.VMEM((n_heads, kv_lora), F32),
        ],
    )
    return pl.pallas_call(
        kern, grid_spec=grid_spec,
        out_shape=jax.ShapeDtypeStruct((nb, n_heads, kv_lora), F32),
        compiler_params=_cparams(("arbitrary", "arbitrary")),
        name="attn_sample",
    )(page_table.reshape(-1), qs3, c_new3, kr_new3, cache_c, cache_krt)


def _v_up_kernel(ol_ref, wkv_ref, o_ref, *, n_heads, kv_lora, v_dim):
    for hd in range(n_heads):
        ol = ol_ref[:, hd * kv_lora:(hd + 1) * kv_lora].astype(BF16)
        wv = wkv_ref[:, hd * HEAD_PAD + NOPE_DIM:(hd + 1) * HEAD_PAD]
        o_ref[:, hd * v_dim:(hd + 1) * v_dim] = jnp.dot(ol, wv, preferred_element_type=F32).astype(o_ref.dtype)


def _v_up(o_lat2, wkv_p, *, n_heads, kv_lora, v_dim):
    nb = o_lat2.shape[0]
    return pl.pallas_call(
        functools.partial(_v_up_kernel, n_heads=n_heads, kv_lora=kv_lora, v_dim=v_dim),
        out_shape=jax.ShapeDtypeStruct((nb, n_heads * v_dim), BF16),
        compiler_params=_cparams(None),
        name="v_up_sample",
    )(o_lat2, wkv_p)


def _pack_bf16_pairs(h):
    w = h.shape[1] // 2
    hi = pltpu.bitcast(h[:, :w].astype(F32), jnp.uint32)
    lo = pltpu.bitcast(h[:, w:].astype(F32), jnp.uint32)
    return hi | (lo >> 16)


def _route_rows(x1, g, w_r, bias, run_ref, *, n_exp, n_grp, per_grp):
    tm = x1.shape[0]
    h = _rms(x1, g).astype(BF16)
    logits = jnp.dot(h, w_r, preferred_element_type=F32) + bias
    lane_i = lax.broadcasted_iota(jnp.int32, (tm, LANES), 1)
    lane = lane_i.astype(F32)
    big = float(LANES)

    def first_max(vals):
        mx = jnp.max(vals, axis=-1, keepdims=True)
        idx = jnp.min(jnp.where(vals == mx, lane, big), axis=-1, keepdims=True)
        return mx, idx

    is_grp = (lane_i >= n_exp) & (lane_i < n_exp + n_grp)
    lg = jnp.where(is_grp, logits, NEG_INF)
    g_max, g_lane = first_max(lg)
    g_p = 1.0 / jnp.sum(jnp.where(is_grp, jnp.exp(lg - g_max), 0.0), axis=-1, keepdims=True)
    g_idx = g_lane - float(n_exp)
    lo = g_idx * float(per_grp)
    in_grp = (lane >= lo) & (lane < lo + float(per_grp))
    le = jnp.where(in_grp, logits, NEG_INF)
    m1, i1 = first_max(le)
    m2, i2 = first_max(jnp.where(lane == i1, NEG_INF, le))
    e2 = jnp.exp(m2 - m1)
    w1 = g_p / (1.0 + e2)
    w2 = g_p * e2 / (1.0 + e2)

    oh1 = lane == i1
    oh2 = lane == i2
    onehot = jnp.where(oh1 | oh2, 1.0, 0.0)
    r_i = lax.broadcasted_iota(jnp.int32, (tm, tm), 0)
    c_i = lax.broadcasted_iota(jnp.int32, (tm, tm), 1)
    lower = jnp.where(c_i < r_i, 1.0, 0.0).astype(BF16)
    before = run_ref[...] + jnp.dot(lower, onehot.astype(BF16), preferred_element_type=F32)
    rank1 = jnp.sum(jnp.where(oh1, before, 0.0), axis=-1, keepdims=True)
    rank2 = jnp.sum(jnp.where(oh2, before, 0.0), axis=-1, keepdims=True)
    run_ref[...] = run_ref[...] + jnp.sum(onehot, axis=0, keepdims=True)

    info = jnp.where(lane_i == 0, i1, jnp.where(lane_i == 1, i2, jnp.where(lane_i == 2, rank1, jnp.where(
        lane_i == 3, rank2, jnp.where(lane_i == 4, w1, jnp.where(lane_i == 5, w2, 0.0))))))
    return info, _pack_bf16_pairs(h)


def _out_proj_route_kernel(ap_ref, bp_ref, xp_ref, as_ref, bs_ref, xs_ref, w_ref, g_ref, wr_ref, bias_ref,
                           o_ref, info_ref, hp_ref, cnt_ref, run_ref, *, half, n_prompt_tiles, route):
    i = pl.program_id(0)

    @pl.when(i == 0)
    def _():
        run_ref[...] = jnp.zeros(run_ref.shape, F32)

    def tile(a_ref, b_ref, x_ref):
        n = a_ref.shape[0]
        x1 = (x_ref[...] + jnp.dot(a_ref[...], w_ref[0:half, :], preferred_element_type=F32)
              + jnp.dot(b_ref[...], w_ref[half:, :], preferred_element_type=F32))
        info, hp = _route_rows(x1, g_ref[...], wr_ref[...], bias_ref[...], run_ref, **route)
        o_ref[0:n, :] = x1
        info_ref[0:n, :] = info
        hp_ref[0:n, :] = hp

    @pl.when(i < n_prompt_tiles)
    def _():
        tile(ap_ref, bp_ref, xp_ref)

    @pl.when(i == n_prompt_tiles)
    def _():
        tile(as_ref, bs_ref, xs_ref)

    cnt_ref[...] = run_ref[...]


def _out_proj_route(a_p, b_p, x_p, a_s, b_s, x_s, w, ffn_g, w_r, bias, *, tm, n_exp, n_grp, per_grp):
    n_p, half = a_p.shape
    n_s = a_s.shape[0]
    d = w.shape[1]
    n = n_p + n_s
    npt = n_p // tm
    assert n_s <= tm
    const = lambda i: (0, 0)
    prow = lambda i: (jnp.minimum(i, npt - 1), 0)
    row = lambda i: (i, 0)
    kern = functools.partial(_out_proj_route_kernel, half=half, n_prompt_tiles=npt,
                             route=dict(n_exp=n_exp, n_grp=n_grp, per_grp=per_grp))
    return pl.pallas_call(
        kern, grid=(npt + 1,),
        in_specs=[
            pl.BlockSpec((tm, half), prow),
            pl.BlockSpec((tm, half), prow),
            pl.BlockSpec((tm, d), prow),
            pl.BlockSpec((n_s, half), const),
            pl.BlockSpec((n_s, half), const),
            pl.BlockSpec((n_s, d), const),
            pl.BlockSpec(w.shape, const, pipeline_mode=pl.Buffered(1)),
            pl.BlockSpec((1, d), const),
            pl.BlockSpec((d, LANES), const),
            pl.BlockSpec((1, LANES), const),
        ],
        out_specs=[pl.BlockSpec((tm, d), row), pl.BlockSpec((tm, LANES), row), pl.BlockSpec((tm, d // 2), row),
                   pl.BlockSpec((1, LANES), const)],
        out_shape=[jax.ShapeDtypeStruct((n, d), F32), jax.ShapeDtypeStruct((n, LANES), F32),
                   jax.ShapeDtypeStruct((n, d // 2), jnp.uint32), jax.ShapeDtypeStruct((1, LANES), F32)],
        scratch_shapes=[pltpu.VMEM((1, LANES), F32)],
        compiler_params=_cparams(("arbitrary",)),
        name="out_proj_route",
    )(a_p, b_p, x_p, a_s, b_s, x_s, w, ffn_g, w_r, bias)


def _pow2_pieces(limit):
    p = limit // 2
    while p >= 1:
        yield p
        p //= 2


def _dispatch_kernel(dest_ref, zrow_ref, zcnt_ref, pend_ref, hp_ref, xs_ref, hbuf, zbuf, sem, zsem, *, tm, sub):
    i = pl.program_id(0)
    n = pl.num_programs(0)
    slot = i % 2
    n_exp = zrow_ref.shape[0]
    n_sub_blocks = xs_ref.shape[0] // sub

    def row_copy(sl, r, d):
        return pltpu.make_async_copy(hbuf.at[sl, pl.ds(r, 1), :], xs_ref.at[pl.ds(d, 1), :], sem.at[sl])

    def wait_all(sl):
        for _ in range(TOP_K):
            pltpu.make_async_copy(hbuf.at[sl], xs_ref.at[pl.ds(0, tm), :], sem.at[sl]).wait()

    def zero_copy(row, size):
        return pltpu.make_async_copy(zbuf.at[pl.ds(0, size), :], xs_ref.at[pl.ds(row, size), :], zsem)

    def zero_fill(start):
        def per_expert(e, c):
            row = zrow_ref[e]
            cnt = zcnt_ref[e]
            head = (-row) & (SUBLANES - 1)
            for r in range(SUBLANES - 1):
                @pl.when(r < jnp.minimum(head, cnt))
                def _():
                    cp = zero_copy(row + r, 1)
                    cp.start() if start else cp.wait()
            rest = jnp.maximum(cnt - head, 0)
            row = pl.multiple_of(row + head, SUBLANES)
            for piece in _pow2_pieces(sub):
                if piece < SUBLANES:
                    break
                @pl.when((rest & piece) != 0)
                def _():
                    cp = zero_copy(row, piece)
                    cp.start() if start else cp.wait()
                row = pl.multiple_of(row + (rest & piece), SUBLANES)
            return c
        lax.fori_loop(0, n_exp, per_expert, 0)

        def per_block(b, c):
            @pl.when(b * sub >= pend_ref[0])
            def _():
                cp = zero_copy(b * sub, sub)
                cp.start() if start else cp.wait()
            return c
        lax.fori_loop(0, n_sub_blocks, per_block, 0)

    @pl.when(i == 0)
    def _():
        zbuf[...] = jnp.zeros(zbuf.shape, zbuf.dtype)
        zero_fill(True)

    hbuf[slot] = hp_ref[...]

    def issue(r, c):
        a = (i * tm + r) * TOP_K
        for k in range(TOP_K):
            row_copy(slot, r, dest_ref[a + k]).start()
        return c
    lax.fori_loop(0, tm, issue, 0, unroll=8)

    @pl.when(i == 0)
    def _():
        zero_fill(False)

    @pl.when(i > 0)
    def _():
        wait_all(1 - slot)

    @pl.when(i == n - 1)
    def _():
        wait_all(slot)


def _dispatch(dest, zrow, zcnt, pend, hp, *, tm, rows, sub):
    n, dw = hp.shape
    grid_spec = pltpu.PrefetchScalarGridSpec(
        num_scalar_prefetch=4, grid=(n // tm,),
        in_specs=[pl.BlockSpec((tm, dw), lambda i, *_: (i, 0))],
        out_specs=pl.BlockSpec(memory_space=pl.ANY),
        scratch_shapes=[pltpu.VMEM((2, tm, dw), hp.dtype), pltpu.VMEM((sub, dw), hp.dtype),
                        pltpu.SemaphoreType.DMA((2,)), pltpu.SemaphoreType.DMA(())],
    )
    return pl.pallas_call(
        functools.partial(_dispatch_kernel, tm=tm, sub=sub), grid_spec=grid_spec,
        out_shape=jax.ShapeDtypeStruct((rows, dw), hp.dtype),
        compiler_params=_cparams(("arbitrary",)),
        name="moe_dispatch",
    )(dest, zrow, zcnt, pend, hp)


def _experts_kernel(ie_ref, ir_ref, ins_ref, nit_ref, pend_ref, xs_ref, wg_hbm, wu_hbm, wd_hbm, ys_ref,
                    gu_ring, d_ring, xbuf, gacc, uacc, hbuf, ybuf, zbuf,
                    gu_sem, d_sem, x_sem, y_sem, z_sem, *, sub, n_sub_max, kc, fcs):
    n_items = nit_ref[0]
    d = ys_ref.shape[1]
    n_kc = d // kc
    n_wc = n_kc // 2
    n_fc = len(fcs)
    assert n_fc == 2 and n_kc >= 2 and n_kc % 2 == 0
    item_rows = sub * n_sub_max
    f_offs = [sum(fcs[:j]) for j in range(n_fc)]

    def gu_copies(item, pos, slot):
        e = ie_ref[item]
        rows = pl.ds(pos * kc, kc)
        return (pltpu.make_async_copy(wg_hbm.at[e, rows, :], gu_ring.at[slot, 0], gu_sem.at[slot]),
                pltpu.make_async_copy(wu_hbm.at[e, rows, :], gu_ring.at[slot, 1], gu_sem.at[slot]))

    def d_copy(item, j):
        e = ie_ref[item]
        return pltpu.make_async_copy(wd_hbm.at[e, pl.ds(f_offs[j], fcs[j]), :], d_ring.at[j, pl.ds(0, fcs[j]), :],
                                     d_sem.at[j])

    def x_xfer(item, slot, start):
        n_sub = ins_ref[item]
        for sb in range(n_sub_max):
            @pl.when(sb < n_sub)
            def _():
                r = pl.multiple_of(ir_ref[item] + sb * sub, sub)
                for k in range(n_wc):
                    cp = pltpu.make_async_copy(xs_ref.at[pl.ds(r, sub), pl.ds(k * kc, kc)],
                                               xbuf.at[slot, k, pl.ds(sb * sub, sub), :], x_sem.at[slot])
                    cp.start() if start else cp.wait()

    def y_copy(item, sb):
        r = pl.multiple_of(ir_ref[item] + sb * sub, sub)
        return pltpu.make_async_copy(ybuf.at[pl.ds(sb * sub, sub), :], ys_ref.at[pl.ds(r, sub), :], y_sem)

    def for_rows(n_sub, fn):
        for ns in range(1, n_sub_max + 1):
            @pl.when(n_sub == ns)
            def _():
                fn(ns * sub)

    def y_writes(item, start):
        n_sub = ins_ref[item]
        for sb in range(n_sub_max):
            @pl.when(sb < n_sub)
            def _():
                cp = y_copy(item, sb)
                cp.start() if start else cp.wait()

    zbuf[...] = jnp.zeros(zbuf.shape, F32)
    n_out_blocks = ys_ref.shape[0] // sub

    def tail(start):
        def body(b, c):
            @pl.when(b * sub >= pend_ref[0])
            def _():
                cp = pltpu.make_async_copy(zbuf, ys_ref.at[pl.ds(b * sub, sub), :], z_sem)
                cp.start() if start else cp.wait()
            return c
        lax.fori_loop(0, n_out_blocks, body, 0)
    tail(True)

    @pl.when(n_items > 0)
    def _():
        x_xfer(0, 0, True)
        for pos in range(2):
            for cp in gu_copies(0, pos, pos):
                cp.start()

    def item_body(item, carry):
        xslot = item % 2
        n_sub = ins_ref[item]
        has_next = item + 1 < n_items

        @pl.when(has_next)
        def _():
            x_xfer(item + 1, 1 - xslot, True)

        x_xfer(item, xslot, False)

        def zero_acc(rows):
            gacc[0:rows, :] = jnp.zeros((rows, gacc.shape[1]), F32)
            uacc[0:rows, :] = jnp.zeros((rows, uacc.shape[1]), F32)
        for_rows(n_sub, zero_acc)

        def gu_step(pos, c):
            slot = pos % 2
            for cp in gu_copies(item, pos, slot):
                cp.wait()

            def mm(rows):
                xw = xbuf[xslot, pos % n_wc, 0:rows, :]
                shift = jnp.asarray((pos // n_wc) * 16, jnp.uint32)
                xk = pltpu.bitcast((xw << shift) & jnp.uint32(0xFFFF0000), F32).astype(BF16)
                gacc[0:rows, :] += jnp.dot(xk, gu_ring[slot, 0].astype(BF16), preferred_element_type=F32)
                uacc[0:rows, :] += jnp.dot(xk, gu_ring[slot, 1].astype(BF16), preferred_element_type=F32)
            for_rows(n_sub, mm)

            @pl.when(pos + 2 < n_kc)
            def _():
                for cp in gu_copies(item, pos + 2, slot):
                    cp.start()
            for j in range(n_fc):
                @pl.when(pos + 2 == n_kc + j)
                def _():
                    d_copy(item, j).start()
            return c
        lax.fori_loop(0, n_kc, gu_step, 0)

        def act(rows):
            g = gacc[0:rows, :]
            hbuf[0:rows, :] = (g * _sigmoid(g) * uacc[0:rows, :]).astype(BF16)
        for_rows(n_sub, act)

        @pl.when(item > 0)
        def _():
            y_writes(item - 1, False)

        for j in range(n_fc):
            d_copy(item, j).wait()

            def mm_down(rows, j=j):
                y = jnp.dot(hbuf[0:rows, f_offs[j]:f_offs[j] + fcs[j]], d_ring[j, 0:fcs[j], :].astype(BF16),
                            preferred_element_type=F32)
                if j == 0:
                    ybuf[0:rows, :] = y
                else:
                    ybuf[0:rows, :] += y
            for_rows(n_sub, mm_down)

            @pl.when(has_next)
            def _():
                for cp in gu_copies(item + 1, j, j):
                    cp.start()
        y_writes(item, True)
        return carry

    lax.fori_loop(0, n_items, item_body, 0)

    @pl.when(n_items > 0)
    def _():
        y_writes(n_items - 1, False)
    tail(False)


def _experts(item_e, item_row, item_nsub, n_items, pend, xs, w_gate, w_up, w_down, *, sub, n_sub_max):
    rows = xs.shape[0]
    d, f = w_gate.shape[1:]
    kc = 512
    lanes_f = f // LANES
    fcs = ((lanes_f + 1) // 2 * LANES, lanes_f // 2 * LANES)
    item_rows = sub * n_sub_max
    any_spec = pl.BlockSpec(memory_space=pl.ANY)
    kern = functools.partial(_experts_kernel, sub=sub, n_sub_max=n_sub_max, kc=kc, fcs=fcs)
    return pl.pallas_call(
        kern,
        grid_spec=pltpu.PrefetchScalarGridSpec(
            num_scalar_prefetch=5, grid=(1,),
            in_specs=[any_spec, any_spec, any_spec, any_spec],
            out_specs=any_spec,
            scratch_shapes=[
                pltpu.VMEM((2, 2, kc, f), F32),
                pltpu.VMEM((2, fcs[0], d), F32),
                pltpu.VMEM((2, d // kc // 2, item_rows, kc), xs.dtype),
                pltpu.VMEM((item_rows, f), F32),
                pltpu.VMEM((item_rows, f), F32),
                pltpu.VMEM((item_rows, f), BF16),
                pltpu.VMEM((item_rows, d), F32),
                pltpu.VMEM((sub, d), F32),
                pltpu.SemaphoreType.DMA((2,)),
                pltpu.SemaphoreType.DMA((2,)),
                pltpu.SemaphoreType.DMA((2,)),
                pltpu.SemaphoreType.DMA(()),
                pltpu.SemaphoreType.DMA(()),
            ],
        ),
        out_shape=jax.ShapeDtypeStruct((rows, d), F32),
        compiler_params=_cparams(("arbitrary",), 60 * 1024 * 1024),
        name="moe_experts",
    )(item_e, item_row, item_nsub, n_items, pend, xs, w_gate, w_up, w_down)


def _combine_kernel(dest_ref, x_ref, info_ref, g_ref, ys_ref, yp_ref, ys_out_ref, rbuf, sem, *, tm, n_prompt_tiles):
    i = pl.program_id(0)
    n = pl.num_programs(0)
    slot = i % 2

    def row_copy(sl, k, r, d):
        return pltpu.make_async_copy(ys_ref.at[pl.ds(d, 1), :], rbuf.at[sl, k, pl.ds(r, 1), :], sem.at[sl])

    def issue(step, sl):
        def body(r, c):
            a = (step * tm + r) * TOP_K
            for k in range(TOP_K):
                row_copy(sl, k, r, dest_ref[a + k]).start()
            return c
        lax.fori_loop(0, tm, body, 0, unroll=8)

    @pl.when(i == 0)
    def _():
        issue(0, 0)

    @pl.when(i + 1 < n)
    def _():
        issue(i + 1, 1 - slot)

    for k in range(TOP_K):
        pltpu.make_async_copy(ys_ref.at[pl.ds(0, tm), :], rbuf.at[slot, k], sem.at[slot]).wait()

    info = info_ref[...]
    lane_i = lax.broadcasted_iota(jnp.int32, info.shape, 1)
    w1 = jnp.sum(jnp.where(lane_i == 4, info, 0.0), axis=-1, keepdims=True)
    w2 = jnp.sum(jnp.where(lane_i == 5, info, 0.0), axis=-1, keepdims=True)
    y = x_ref[...] + (rbuf[slot, 0] * w1 + rbuf[slot, 1] * w2)
    out = _rms(y, g_ref[...])

    @pl.when(i < n_prompt_tiles)
    def _():
        yp_ref[...] = out

    @pl.when(i >= n_prompt_tiles)
    def _():
        ys_out_ref[...] = out


def _combine(dest, x1, info, final_g, ys, *, tm, n_prompt, n_sample):
    n, d = x1.shape
    npt = n_prompt // tm
    grid_spec = pltpu.PrefetchScalarGridSpec(
        num_scalar_prefetch=1, grid=(n // tm,),
        in_specs=[
            pl.BlockSpec((tm, d), lambda i, ds: (i, 0)),
            pl.BlockSpec((tm, LANES), lambda i, ds: (i, 0)),
            pl.BlockSpec((1, d), lambda i, ds: (0, 0)),
            pl.BlockSpec(memory_space=pl.ANY),
        ],
        out_specs=[
            pl.BlockSpec((tm, d), lambda i, ds: (jnp.minimum(i, npt - 1), 0)),
            pl.BlockSpec((tm, d), lambda i, ds: (jnp.maximum(i - npt, 0), 0)),
        ],
        scratch_shapes=[pltpu.VMEM((2, TOP_K, tm, d), F32), pltpu.SemaphoreType.DMA((2,))],
    )
    return pl.pallas_call(
        functools.partial(_combine_kernel, tm=tm, n_prompt_tiles=npt), grid_spec=grid_spec,
        out_shape=[jax.ShapeDtypeStruct((n_prompt, d), F32), jax.ShapeDtypeStruct((n_sample, d), F32)],
        compiler_params=_cparams(("arbitrary",)),
        name="moe_combine",
    )(dest, x1, info, final_g, ys)


def _rope_tables(pos):
    half = ROPE_DIM // 2
    inv_freq = ROPE_BASE ** (-jnp.arange(half, dtype=F32) / half)
    ang = pos.astype(F32)[:, None] * inv_freq[None, :]
    cos, sin = jnp.cos(ang), jnp.sin(ang)
    z = jnp.zeros_like(cos)
    z2 = jnp.concatenate([z, z], axis=1)
    return (jnp.concatenate([cos, cos, z2], axis=1), jnp.concatenate([-sin, z, z2], axis=1),
            jnp.concatenate([z, sin, z2], axis=1))


def kernel(x_prompt, x_sample, cache_kv_latent, cache_k_rope, state_conv, page_table, attn_norm_g, w_in, q_norm_g,
           w_q_up, kv_norm_g, w_kv_up, conv_w, conv_b, conv_ln_g, conv_ln_b, w_out, ffn_norm_g, w_router_group,
           b_router_group, w_router_expert, b_router_expert, w_exp_gate, w_exp_up, w_exp_down, final_norm_g):
    batch, seq, d = x_prompt.shape
    nb, t_dec, _ = x_sample.shape
    depth, q_lora, n_heads, qk_dim = w_q_up.shape
    kv_lora = w_kv_up.shape[1]
    v_dim = w_kv_up.shape[3] - NOPE_DIM
    conv_ch = conv_w.shape[2]
    width = conv_w.shape[1]
    n_exp = w_exp_gate.shape[1]
    n_grp = w_router_group.shape[2]
    per_grp = n_exp // n_grp
    page = cache_kv_latent.shape[2]
    past = page_table.shape[1] * page
    assert depth == 1 and t_dec == 1 and qk_dim == NOPE_DIM + ROPE_DIM and v_dim == NOPE_DIM
    scale = float(qk_dim) ** -0.5
    n_p = batch * seq
    n_tok = n_p + nb

    w_in_p = w_in[0].astype(BF16)
    wq_p = jnp.pad(w_q_up[0], ((0, 0), (0, 0), (0, HEAD_PAD - qk_dim))).reshape(q_lora, n_heads * HEAD_PAD)
    wq_p = wq_p.astype(BF16)
    wkv_p = w_kv_up[0].reshape(kv_lora, n_heads * HEAD_PAD).astype(BF16)
    w_out_b = w_out[0].astype(BF16)
    w_r = jnp.concatenate([w_router_expert[0], w_router_group[0],
                           jnp.zeros((d, LANES - n_exp - n_grp), F32)], axis=1).astype(BF16)
    b_r = jnp.concatenate([b_router_expert[0], b_router_group[0], jnp.zeros((LANES - n_exp - n_grp,), F32)])[None]
    row2 = lambda v: v.reshape(1, -1)

    tm_p = 512
    rc_p, ra_p, rb_p = _rope_tables(jnp.arange(seq))
    rc_s, ra_s, rb_s = _rope_tables(jnp.full((nb,), past, jnp.int32))
    dims = dict(n_heads=n_heads, q_lora=q_lora, kv_lora=kv_lora, conv_ch=conv_ch)
    xp2 = x_prompt.reshape(n_p, d)
    xs2 = x_sample.reshape(nb, d)
    q_p, c_p, kr_p, u_p, kv_p, krb_p = _in_proj(
        xp2, row2(attn_norm_g[0]), w_in_p, row2(q_norm_g[0]), wq_p, row2(kv_norm_g[0]), wkv_p, rc_p, ra_p, rb_p,
        tm=tm_p, prompt=True, **dims)
    qs_s, c_s, kr_s, u_s = _in_proj(
        xs2, row2(attn_norm_g[0]), w_in_p, row2(q_norm_g[0]), wq_p, row2(kv_norm_g[0]), wkv_p, rc_s, ra_s, rb_s,
        tm=nb, prompt=False, **dims)

    o_att_p = _attn_prompt(q_p, kv_p, krb_p, batch=batch, seq=seq, n_heads=n_heads, v_dim=v_dim, scale=scale)
    cw, cb, cg, cbeta = conv_w[0], row2(conv_b[0]), row2(conv_ln_g[0]), row2(conv_ln_b[0])
    o_conv_p = _conv_prompt(u_p, cw, cb, cg, cbeta, batch=batch, seq=seq)

    o_lat = _attn_sample(page_table, qs_s.reshape(nb, n_heads, QS_HEAD), c_s.reshape(nb, 1, kv_lora),
                         kr_s.reshape(nb, 1, ROPE_DIM), cache_kv_latent[0],
                         jnp.transpose(cache_k_rope[0], (0, 2, 1)), scale=scale)
    o_att_s = _v_up(o_lat.reshape(nb, n_heads * kv_lora), wkv_p, n_heads=n_heads, kv_lora=kv_lora, v_dim=v_dim)
    state_t = jnp.transpose(state_conv[0], (1, 0, 2))
    o_conv_s, new_state_t = _conv_sample(state_t, u_s, cw, cb, cg, cbeta)

    x1, info, hp, counts = _out_proj_route(o_att_p, o_conv_p, xp2, o_att_s, o_conv_s, xs2, w_out_b,
                                           row2(ffn_norm_g[0]), w_r, b_r, tm=512, n_exp=n_exp, n_grp=n_grp,
                                           per_grp=per_grp)
    i32 = jnp.int32
    sub, item_rows = MOE_SUB, MOE_SUB * MOE_ITEM_SUBS
    n_assign = n_tok * TOP_K
    rows_alloc = (-(-n_assign // sub) + n_exp) * sub
    max_items = n_assign // item_rows + n_exp + 1
    cnt = counts[0, :n_exp].astype(i32)
    padded = (cnt + sub - 1) // sub * sub
    pad_end = jnp.cumsum(padded)
    pad_start = pad_end - padded
    e_flat = info[:, 0:TOP_K].astype(i32).reshape(-1)
    rank = info[:, TOP_K:2 * TOP_K].astype(i32).reshape(-1)
    is_e = e_flat[:, None] == jnp.arange(n_exp, dtype=i32)[None, :]
    dest = rank + jnp.sum(jnp.where(is_e, pad_start[None, :], 0), axis=1)
    items_per_e = (padded + item_rows - 1) // item_rows
    it_end = jnp.cumsum(items_per_e)
    it_start = it_end - items_per_e
    w_ids = jnp.arange(max_items, dtype=i32)
    item_e = jnp.minimum(jnp.sum((it_end[None, :] <= w_ids[:, None]).astype(i32), axis=1), n_exp - 1)
    item_i = w_ids - it_start[item_e]
    item_row = jnp.clip(pad_start[item_e] + item_rows * item_i, 0, rows_alloc - item_rows).astype(i32)
    item_nsub = jnp.clip((padded[item_e] - item_rows * item_i) // sub, 1, MOE_ITEM_SUBS).astype(i32)
    n_items = it_end[-1:].astype(i32)
    pend = pad_end[-1:].astype(i32)

    xs = _dispatch(dest, (pad_start + cnt).astype(i32), (padded - cnt).astype(i32), pend, hp, tm=TOK_TILE,
                   rows=rows_alloc, sub=sub)
    ys = _experts(item_e, item_row, item_nsub, n_items, pend, xs, w_exp_gate[0], w_exp_up[0], w_exp_down[0],
                  sub=sub, n_sub_max=MOE_ITEM_SUBS)
    y_p, y_s = _combine(dest, x1, info, row2(final_norm_g), ys, tm=CMB_TILE, n_prompt=n_p, n_sample=nb)

    hist = width - 1
    new_conv_p = u_p.reshape(batch, seq, conv_ch)[:, seq - hist:]
    new_conv_s = jnp.transpose(new_state_t, (1, 0, 2))
    return (y_p.reshape(batch, seq, d), y_s.reshape(nb, t_dec, d),
            c_p.reshape(1, batch, seq, kv_lora), kr_p.reshape(1, batch, seq, ROPE_DIM), new_conv_p[None],
            c_s.reshape(1, nb, t_dec, kv_lora), kr_s.reshape(1, nb, t_dec, ROPE_DIM), new_conv_s[None])
```

```python
import functools

import jax
import jax.numpy as jnp
from jax import lax
from jax.experimental import pallas as pl
from jax.experimental.pallas import tpu as pltpu

F32 = jnp.float32
BF16 = jnp.bfloat16

EPS = 1e-6
NEG_INF = -1e30
ROPE_BASE = 10000.0

LANES = 128
SUBLANES = 8
VMEM_LIMIT = 56 * 1024 * 1024

NOPE_DIM = 128
ROPE_DIM = 64
TOP_K = 2
HEAD_PAD = 256
QS_HEAD = 640

MOE_SUB = 128
MOE_ITEM_SUBS = 4
TOK_TILE = 416
CMB_TILE = 128


def _cparams(sem, vmem=VMEM_LIMIT):
    return pltpu.CompilerParams(dimension_semantics=sem, vmem_limit_bytes=vmem)


def _rms(x, g):
    return x * lax.rsqrt(jnp.mean(x * x, axis=-1, keepdims=True) + EPS) * g


def _rope128(v, c, a, b):
    return v * c + pltpu.roll(v, 96, 1) * a + pltpu.roll(v, 32, 1) * b


def _sigmoid(x):
    return 1.0 / (1.0 + jnp.exp(-x))


def _nt_dot(a, b):
    return lax.dot_general(a, b, (((1,), (1,)), ((), ())), preferred_element_type=F32)


def _in_proj_kernel(*refs, n_heads, q_lora, kv_lora, conv_ch, prompt):
    (x_ref, g_ref, w_ref, qg_ref, wq_ref, kvg_ref, wkv_ref, rc_ref, ra_ref, rb_ref) = refs[:10]
    outs = refs[10:]
    if prompt:
        q_ref, c_ref, kr_ref, u_ref, kv_ref, krb_ref = outs
    else:
        q_ref, c_ref, kr_ref, u_ref = outs
    o_kv = q_lora
    o_kr = q_lora + kv_lora
    o_g = o_kr + conv_ch
    n_in = o_kr + ROPE_DIM + 2 * conv_ch

    h = _rms(x_ref[...], g_ref[...]).astype(BF16)
    rc, ra, rb = rc_ref[...], ra_ref[...], rb_ref[...]

    zq = jnp.dot(h, w_ref[:, 0:q_lora], preferred_element_type=F32)
    qn = _rms(zq, qg_ref[...]).astype(BF16)
    qf = jnp.dot(qn, wq_ref[...], preferred_element_type=F32)

    zkv = jnp.dot(h, w_ref[:, o_kv:o_kr], preferred_element_type=F32)
    c = _rms(zkv, kvg_ref[...])
    c_ref[...] = c

    za = jnp.dot(h, w_ref[:, o_kr:o_g + LANES], preferred_element_type=F32)
    zg = jnp.dot(h, w_ref[:, o_g:n_in], preferred_element_type=F32)
    kr = _rope128(za[:, 0:LANES], rc, ra, rb)
    kr_ref[...] = kr[:, :ROPE_DIM]
    glu = za[:, 0:ROPE_DIM + conv_ch] * _sigmoid(zg)
    u_ref[...] = glu[:, ROPE_DIM:]

    if prompt:
        krb_ref[...] = kr.astype(BF16)
        kv_ref[...] = jnp.dot(c.astype(BF16), wkv_ref[...], preferred_element_type=F32).astype(BF16)
        for hd in range(n_heads):
            o = hd * HEAD_PAD
            q_ref[:, o:o + NOPE_DIM] = qf[:, o:o + NOPE_DIM].astype(BF16)
            q_ref[:, o + NOPE_DIM:o + HEAD_PAD] = _rope128(
                qf[:, o + NOPE_DIM:o + HEAD_PAD], rc, ra, rb).astype(BF16)
    else:
        for hd in range(n_heads):
            o = hd * HEAD_PAD
            qnope = qf[:, o:o + NOPE_DIM].astype(BF16)
            wkn = wkv_ref[:, o:o + NOPE_DIM]
            qo = hd * QS_HEAD
            q_ref[:, qo:qo + kv_lora] = _nt_dot(qnope, wkn).astype(BF16)
            q_ref[:, qo + kv_lora:qo + QS_HEAD] = _rope128(
                qf[:, o + NOPE_DIM:o + HEAD_PAD], rc, ra, rb).astype(BF16)


def _in_proj(x, attn_g, w_in_p, q_g, wq_p, kv_g, wkv_p, rc, ra, rb, *, tm, prompt, n_heads, q_lora, kv_lora,
             conv_ch):
    n, d = x.shape
    n_tab = rc.shape[0] // tm
    const = lambda i: (0, 0)
    row = lambda i: (i, 0)
    tab = lambda i: (i % n_tab, 0)
    single = pl.Buffered(1)
    in_specs = [
        pl.BlockSpec((tm, d), row),
        pl.BlockSpec((1, d), const),
        pl.BlockSpec(w_in_p.shape, const, pipeline_mode=single),
        pl.BlockSpec((1, q_lora), const),
        pl.BlockSpec(wq_p.shape, const, pipeline_mode=single),
        pl.BlockSpec((1, kv_lora), const),
        pl.BlockSpec(wkv_p.shape, const, pipeline_mode=single),
        pl.BlockSpec((tm, LANES), tab),
        pl.BlockSpec((tm, LANES), tab),
        pl.BlockSpec((tm, LANES), tab),
    ]
    q_cols = n_heads * (HEAD_PAD if prompt else QS_HEAD)
    out_shape = [
        jax.ShapeDtypeStruct((n, q_cols), BF16),
        jax.ShapeDtypeStruct((n, kv_lora), F32),
        jax.ShapeDtypeStruct((n, ROPE_DIM), F32),
        jax.ShapeDtypeStruct((n, conv_ch), F32),
    ]
    out_specs = [
        pl.BlockSpec((tm, q_cols), row),
        pl.BlockSpec((tm, kv_lora), row),
        pl.BlockSpec((tm, ROPE_DIM), row),
        pl.BlockSpec((tm, conv_ch), row),
    ]
    if prompt:
        out_shape += [jax.ShapeDtypeStruct((n, wkv_p.shape[1]), BF16), jax.ShapeDtypeStruct((n, LANES), BF16)]
        out_specs += [pl.BlockSpec((tm, wkv_p.shape[1]), row), pl.BlockSpec((tm, LANES), row)]
    kern = functools.partial(_in_proj_kernel, n_heads=n_heads, q_lora=q_lora, kv_lora=kv_lora, conv_ch=conv_ch,
                             prompt=prompt)
    return pl.pallas_call(
        kern, grid=(n // tm,), in_specs=in_specs, out_specs=out_specs, out_shape=out_shape,
        compiler_params=_cparams(("arbitrary",)),
        name="in_proj_prompt" if prompt else "in_proj_sample",
    )(x, attn_g, w_in_p, q_g, wq_p, kv_g, wkv_p, rc, ra, rb)


def _attn_prompt_kernel(q_ref, kv_ref, krb_ref, o_ref, kcat_ref, *, tq, tk, scale):
    i = pl.program_id(2)

    @pl.when(i == 0)
    def _():
        kcat_ref[:, :NOPE_DIM] = kv_ref[:, :NOPE_DIM]
        kcat_ref[:, NOPE_DIM:] = krb_ref[...]

    q = q_ref[...]
    v_dim = o_ref.shape[1]

    def scores(j):
        k = kcat_ref[pl.ds(pl.multiple_of(j * tk, tk), tk), :]
        return _nt_dot(q, k) * scale

    def update(j, s, carry):
        m, l, acc = carry
        m_new = jnp.maximum(m, jnp.max(s, axis=-1, keepdims=True))
        alpha = jnp.exp(m - m_new)
        p = jnp.exp(s - m_new)
        v = kv_ref[pl.ds(pl.multiple_of(j * tk, tk), tk), NOPE_DIM:]
        acc = acc * alpha + jnp.dot(p.astype(BF16), v, preferred_element_type=F32)
        return m_new, l * alpha + jnp.sum(p, axis=-1, keepdims=True), acc

    def body(j, carry):
        s_next = scores(j + 1)
        return update(j, carry[3], carry[:3]) + (s_next,)

    init = (jnp.full((tq, 1), NEG_INF, F32), jnp.zeros((tq, 1), F32), jnp.zeros((tq, v_dim), F32), scores(0))
    m, l, acc, s = lax.fori_loop(0, i, body, init)
    rowi = lax.broadcasted_iota(jnp.int32, (tq, tk), 0)
    coli = lax.broadcasted_iota(jnp.int32, (tq, tk), 1)
    m, l, acc = update(i, jnp.where(coli <= rowi, s, NEG_INF), (m, l, acc))
    o_ref[...] = (acc / l).astype(o_ref.dtype)


def _attn_prompt(q, kv, krb, *, batch, seq, n_heads, v_dim, scale, tq=512):
    n = q.shape[0]
    nq = seq // tq
    kern = functools.partial(_attn_prompt_kernel, tq=tq, tk=tq, scale=scale)
    return pl.pallas_call(
        kern, grid=(batch, n_heads, nq),
        in_specs=[
            pl.BlockSpec((tq, HEAD_PAD), lambda b, h, i: (b * nq + i, h)),
            pl.BlockSpec((seq, HEAD_PAD), lambda b, h, i: (b, h)),
            pl.BlockSpec((seq, LANES), lambda b, h, i: (b, 0)),
        ],
        out_specs=pl.BlockSpec((tq, v_dim), lambda b, h, i: (b * nq + i, h)),
        out_shape=jax.ShapeDtypeStruct((n, n_heads * v_dim), BF16),
        scratch_shapes=[pltpu.VMEM((seq, HEAD_PAD), BF16)],
        compiler_params=_cparams(("arbitrary", "arbitrary", "arbitrary")),
        name="attn_prompt",
    )(q, kv, krb)


def _ln_silu(y, g, b):
    yc = y - jnp.mean(y, axis=-1, keepdims=True)
    z = yc * lax.rsqrt(jnp.mean(yc * yc, axis=-1, keepdims=True) + EPS) * g + b
    return z * _sigmoid(z)


def _conv_prompt_kernel(halo_ref, u_ref, w_ref, b_ref, g_ref, beta_ref, o_ref, win_ref, y_ref, sh_ref, *, tt,
                        halo, width, rc):
    i = pl.program_id(1)

    @pl.when(i == 0)
    def _():
        win_ref[0:halo, :] = jnp.zeros((halo, win_ref.shape[1]), F32)

    @pl.when(i > 0)
    def _():
        win_ref[0:halo, :] = halo_ref[...]

    win_ref[halo:, :] = u_ref[...]
    base = halo - (width - 1)
    ch = win_ref.shape[1]
    n_sh = tt + halo - SUBLANES
    for lc in range(ch // LANES):
        ls = slice(lc * LANES, (lc + 1) * LANES)
        wl = w_ref[:, ls]
        for g in range(1, SUBLANES):
            sh_ref[g, 0:n_sh, :] = win_ref[g:g + n_sh, ls]

        def row_chunk(r, c, ls=ls, wl=wl):
            r0 = pl.multiple_of(r * rc, rc)
            acc = jnp.zeros((rc, LANES), F32)
            for k in range(width):
                g = (base + k) % SUBLANES
                s0 = r0 + ((base + k) - g)
                tap = win_ref[pl.ds(s0, rc), ls] if g == 0 else sh_ref[g, pl.ds(s0, rc), :]
                acc = acc + tap * wl[k:k + 1, :]
            y_ref[pl.ds(r0, rc), ls] = acc
            return c
        lax.fori_loop(0, tt // rc, row_chunk, 0)
    o_ref[...] = _ln_silu(y_ref[...] + b_ref[...], g_ref[...], beta_ref[...]).astype(o_ref.dtype)


def _conv_prompt(u, conv_w, conv_b, ln_g, ln_b, *, batch, seq, tt=256, halo=32, rc=64):
    n, ch = u.shape
    width = conv_w.shape[0]
    nt = seq // tt
    hb = tt // halo
    const = lambda b, i: (0, 0)
    kern = functools.partial(_conv_prompt_kernel, tt=tt, halo=halo, width=width, rc=rc)
    return pl.pallas_call(
        kern, grid=(batch, nt),
        in_specs=[
            pl.BlockSpec((halo, ch), lambda b, i: (jnp.maximum((b * nt + i) * hb - 1, 0), 0)),
            pl.BlockSpec((tt, ch), lambda b, i: (b * nt + i, 0)),
            pl.BlockSpec((width, ch), const),
            pl.BlockSpec((1, ch), const),
            pl.BlockSpec((1, ch), const),
            pl.BlockSpec((1, ch), const),
        ],
        out_specs=pl.BlockSpec((tt, ch), lambda b, i: (b * nt + i, 0)),
        out_shape=jax.ShapeDtypeStruct((n, ch), BF16),
        scratch_shapes=[pltpu.VMEM((halo + tt, ch), F32), pltpu.VMEM((tt, ch), F32),
                        pltpu.VMEM((SUBLANES, halo + tt, LANES), F32)],
        compiler_params=_cparams(("arbitrary", "arbitrary")),
        name="conv_prompt",
    )(u, u, conv_w, conv_b, ln_g, ln_b)


def _conv_sample_kernel(st_ref, u_ref, w_ref, b_ref, g_ref, beta_ref, o_ref, new_ref, *, width):
    w = w_ref[...]
    u = u_ref[...]
    y = u * w[width - 1:width, :] + b_ref[...]
    for k in range(width - 1):
        y = y + st_ref[k] * w[k:k + 1, :]
    o_ref[...] = _ln_silu(y, g_ref[...], beta_ref[...]).astype(o_ref.dtype)
    for k in range(width - 2):
        new_ref[k] = st_ref[k + 1]
    new_ref[width - 2] = u


def _conv_sample(state_t, u, conv_w, conv_b, ln_g, ln_b, *, tb=32):
    hist, nb, ch = state_t.shape
    width = conv_w.shape[0]
    const = lambda i: (0, 0)
    return pl.pallas_call(
        functools.partial(_conv_sample_kernel, width=width), grid=(nb // tb,),
        in_specs=[
            pl.BlockSpec((hist, tb, ch), lambda i: (0, i, 0)),
            pl.BlockSpec((tb, ch), lambda i: (i, 0)),
            pl.BlockSpec((width, ch), const),
            pl.BlockSpec((1, ch), const),
            pl.BlockSpec((1, ch), const),
            pl.BlockSpec((1, ch), const),
        ],
        out_specs=[pl.BlockSpec((tb, ch), lambda i: (i, 0)), pl.BlockSpec((hist, tb, ch), lambda i: (0, i, 0))],
        out_shape=[jax.ShapeDtypeStruct((nb, ch), BF16), jax.ShapeDtypeStruct((hist, nb, ch), F32)],
        compiler_params=_cparams(("arbitrary",)),
        name="conv_sample",
    )(state_t, u, conv_w, conv_b, ln_g, ln_b)


def _attn_sample_kernel(pt_ref, qs_ref, cn_ref, krn_ref, cache_c, cache_krt, o_ref, cbuf, rbuf, sem, m_ref, l_ref,
                        acc_ref, *, pages, n_chunks, page, kv_lora, sub, scale):
    b = pl.program_id(0)
    ch = pl.program_id(1)
    g = b * n_chunks + ch
    n_steps = pl.num_programs(0) * n_chunks
    slot = g % 2

    def copies(step, sl):
        out = []
        for p in range(pages):
            pid = pt_ref[step * pages + p]
            keys = pl.ds(p * page, page)
            out.append(pltpu.make_async_copy(cache_c.at[pid], cbuf.at[sl, keys, :], sem.at[0, sl]))
            out.append(pltpu.make_async_copy(cache_krt.at[pid], rbuf.at[sl, :, keys], sem.at[1, sl]))
        return out

    @pl.when(g == 0)
    def _():
        for cp in copies(0, 0):
            cp.start()

    @pl.when(g + 1 < n_steps)
    def _():
        for cp in copies(g + 1, 1 - slot):
            cp.start()

    @pl.when(ch == 0)
    def _():
        m_ref[...] = jnp.full(m_ref.shape, NEG_INF, F32)
        l_ref[...] = jnp.zeros(l_ref.shape, F32)
        acc_ref[...] = jnp.zeros(acc_ref.shape, F32)

    for cp in copies(g, slot):
        cp.wait()

    q = qs_ref[0]
    ql = q[:, :kv_lora]
    qr = q[:, kv_lora:kv_lora + ROPE_DIM]
    cbs, scores = [], []
    for k0 in range(0, pages * page, sub):
        cb = cbuf[slot, k0:k0 + sub, :].astype(BF16)
        rb = rbuf[slot, :, k0:k0 + sub].astype(BF16)
        cbs.append(cb)
        scores.append((_nt_dot(ql, cb) + jnp.dot(qr, rb, preferred_element_type=F32)) * scale)
    probs = []
    for s in scores:
        m_k = jnp.max(s, axis=-1, keepdims=True)
        p = jnp.exp(s - m_k)
        probs.append((m_k, jnp.sum(p, axis=-1, keepdims=True), p.astype(BF16)))
    parts = [(m_k, l_k, jnp.dot(p, cb, preferred_element_type=F32)) for (m_k, l_k, p), cb in zip(probs, cbs)]
    m_old = m_ref[...]
    m_new = m_old
    for m_k, _, _ in parts:
        m_new = jnp.maximum(m_new, m_k)
    alpha = jnp.exp(m_old - m_new)
    l_new = l_ref[...] * alpha
    acc = acc_ref[...] * alpha
    for m_k, l_k, a_k in parts:
        w_k = jnp.exp(m_k - m_new)
        l_new = l_new + l_k * w_k
        acc = acc + a_k * w_k
    l_ref[...] = l_new
    acc_ref[...] = acc
    m_ref[...] = m_new

    @pl.when(ch == n_chunks - 1)
    def _():
        cn = cn_ref[0].astype(BF16).astype(F32)
        krn = krn_ref[0].astype(BF16).astype(F32)
        s_new = (jnp.sum(ql.astype(F32) * cn, axis=-1, keepdims=True)
                 + jnp.sum(qr.astype(F32) * krn, axis=-1, keepdims=True)) * scale
        m_o = m_ref[...]
        m_n = jnp.maximum(m_o, s_new)
        al = jnp.exp(m_o - m_n)
        p_new = jnp.exp(s_new - m_n)
        l_fin = l_ref[...] * al + p_new
        acc = acc_ref[...] * al + p_new.astype(BF16).astype(F32) * cn
        o_ref[0] = acc / l_fin


def _attn_sample(page_table, qs3, c_new3, kr_new3, cache_c, cache_krt, *, scale, pages=32, sub=512):
    nb, n_heads, qw = qs3.shape
    n_pages = page_table.shape[1]
    page, kv_lora = cache_c.shape[1:]
    rope = cache_krt.shape[1]
    n_chunks = n_pages // pages
    keys = pages * page
    kern = functools.partial(_attn_sample_kernel, pages=pages, n_chunks=n_chunks, page=page, kv_lora=kv_lora,
                             sub=sub, scale=scale)
    grid_spec = pltpu.PrefetchScalarGridSpec(
        num_scalar_prefetch=1, grid=(nb, n_chunks),
        in_specs=[
            pl.BlockSpec((1, n_heads, qw), lambda b, c, pt: (b, 0, 0)),
            pl.BlockSpec((1, 1, kv_lora), lambda b, c, pt: (b, 0, 0)),
            pl.BlockSpec((1, 1, rope), lambda b, c, pt: (b, 0, 0)),
            pl.BlockSpec(memory_space=pl.ANY),
            pl.BlockSpec(memory_space=pl.ANY),
        ],
        out_specs=pl.BlockSpec((1, n_heads, kv_lora), lambda b, c, pt: (b, 0, 0)),
        scratch_shapes=[
            pltpu.VMEM((2, keys, kv_lora), F32),
            pltpu.VMEM((2, rope, keys), F32),
            pltpu.SemaphoreType.DMA((2, 2)),
            pltpu.VMEM((n_heads, 1), F32),
            pltpu.VMEM((n_heads, 1), F32),
            pltpu.VMEM((n_heads, kv_lora), F32),
        ],
    )
    return pl.pallas_call(
        kern, grid_spec=grid_spec,
        out_shape=jax.ShapeDtypeStruct((nb, n_heads, kv_lora), F32),
        compiler_params=_cparams(("arbitrary", "arbitrary")),
        name="attn_sample",
    )(page_table.reshape(-1), qs3, c_new3, kr_new3, cache_c, cache_krt)


def _v_up_kernel(ol_ref, wkv_ref, o_ref, *, n_heads, kv_lora, v_dim):
    for hd in range(n_heads):
        ol = ol_ref[:, hd * kv_lora:(hd + 1) * kv_lora].astype(BF16)
        wv = wkv_ref[:, hd * HEAD_PAD + NOPE_DIM:(hd + 1) * HEAD_PAD]
        o_ref[:, hd * v_dim:(hd + 1) * v_dim] = jnp.dot(ol, wv, preferred_element_type=F32).astype(o_ref.dtype)


def _v_up(o_lat2, wkv_p, *, n_heads, kv_lora, v_dim):
    nb = o_lat2.shape[0]
    return pl.pallas_call(
        functools.partial(_v_up_kernel, n_heads=n_heads, kv_lora=kv_lora, v_dim=v_dim),
        out_shape=jax.ShapeDtypeStruct((nb, n_heads * v_dim), BF16),
        compiler_params=_cparams(None),
        name="v_up_sample",
    )(o_lat2, wkv_p)


def _pack_bf16_pairs(h):
    w = h.shape[1] // 2
    hi = pltpu.bitcast(h[:, :w].astype(F32), jnp.uint32)
    lo = pltpu.bitcast(h[:, w:].astype(F32), jnp.uint32)
    return hi | (lo >> 16)


def _route_rows(x1, g, w_r, bias, run_ref, *, n_exp, n_grp, per_grp):
    tm = x1.shape[0]
    h = _rms(x1, g).astype(BF16)
    logits = jnp.dot(h, w_r, preferred_element_type=F32) + bias
    lane_i = lax.broadcasted_iota(jnp.int32, (tm, LANES), 1)
    lane = lane_i.astype(F32)
    big = float(LANES)

    def first_max(vals):
        mx = jnp.max(vals, axis=-1, keepdims=True)
        idx = jnp.min(jnp.where(vals == mx, lane, big), axis=-1, keepdims=True)
        return mx, idx

    is_grp = (lane_i >= n_exp) & (lane_i < n_exp + n_grp)
    lg = jnp.where(is_grp, logits, NEG_INF)
    g_max, g_lane = first_max(lg)
    g_p = 1.0 / jnp.sum(jnp.where(is_grp, jnp.exp(lg - g_max), 0.0), axis=-1, keepdims=True)
    g_idx = g_lane - float(n_exp)
    lo = g_idx * float(per_grp)
    in_grp = (lane >= lo) & (lane < lo + float(per_grp))
    le = jnp.where(in_grp, logits, NEG_INF)
    m1, i1 = first_max(le)
    m2, i2 = first_max(jnp.where(lane == i1, NEG_INF, le))
    e2 = jnp.exp(m2 - m1)
    w1 = g_p / (1.0 + e2)
    w2 = g_p * e2 / (1.0 + e2)

    oh1 = lane == i1
    oh2 = lane == i2
    onehot = jnp.where(oh1 | oh2, 1.0, 0.0)
    r_i = lax.broadcasted_iota(jnp.int32, (tm, tm), 0)
    c_i = lax.broadcasted_iota(jnp.int32, (tm, tm), 1)
    lower = jnp.where(c_i < r_i, 1.0, 0.0).astype(BF16)
    before = run_ref[...] + jnp.dot(lower, onehot.astype(BF16), preferred_element_type=F32)
    rank1 = jnp.sum(jnp.where(oh1, before, 0.0), axis=-1, keepdims=True)
    rank2 = jnp.sum(jnp.where(oh2, before, 0.0), axis=-1, keepdims=True)
    run_ref[...] = run_ref[...] + jnp.sum(onehot, axis=0, keepdims=True)

    info = jnp.where(lane_i == 0, i1, jnp.where(lane_i == 1, i2, jnp.where(lane_i == 2, rank1, jnp.where(
        lane_i == 3, rank2, jnp.where(lane_i == 4, w1, jnp.where(lane_i == 5, w2, 0.0))))))
    return info, _pack_bf16_pairs(h)


def _out_proj_route_kernel(ap_ref, bp_ref, xp_ref, as_ref, bs_ref, xs_ref, w_ref, g_ref, wr_ref, bias_ref,
                           o_ref, info_ref, hp_ref, cnt_ref, run_ref, *, half, n_prompt_tiles, route):
    i = pl.program_id(0)

    @pl.when(i == 0)
    def _():
        run_ref[...] = jnp.zeros(run_ref.shape, F32)

    def tile(a_ref, b_ref, x_ref):
        n = a_ref.shape[0]
        x1 = (x_ref[...] + jnp.dot(a_ref[...], w_ref[0:half, :], preferred_element_type=F32)
              + jnp.dot(b_ref[...], w_ref[half:, :], preferred_element_type=F32))
        info, hp = _route_rows(x1, g_ref[...], wr_ref[...], bias_ref[...], run_ref, **route)
        o_ref[0:n, :] = x1
        info_ref[0:n, :] = info
        hp_ref[0:n, :] = hp

    @pl.when(i < n_prompt_tiles)
    def _():
        tile(ap_ref, bp_ref, xp_ref)

    @pl.when(i == n_prompt_tiles)
    def _():
        tile(as_ref, bs_ref, xs_ref)

    cnt_ref[...] = run_ref[...]


def _out_proj_route(a_p, b_p, x_p, a_s, b_s, x_s, w, ffn_g, w_r, bias, *, tm, n_exp, n_grp, per_grp):
    n_p, half = a_p.shape
    n_s = a_s.shape[0]
    d = w.shape[1]
    n = n_p + n_s
    npt = n_p // tm
    assert n_s <= tm
    const = lambda i: (0, 0)
    prow = lambda i: (jnp.minimum(i, npt - 1), 0)
    row = lambda i: (i, 0)
    kern = functools.partial(_out_proj_route_kernel, half=half, n_prompt_tiles=npt,
                             route=dict(n_exp=n_exp, n_grp=n_grp, per_grp=per_grp))
    return pl.pallas_call(
        kern, grid=(npt + 1,),
        in_specs=[
            pl.BlockSpec((tm, half), prow),
            pl.BlockSpec((tm, half), prow),
            pl.BlockSpec((tm, d), prow),
            pl.BlockSpec((n_s, half), const),
            pl.BlockSpec((n_s, half), const),
            pl.BlockSpec((n_s, d), const),
            pl.BlockSpec(w.shape, const, pipeline_mode=pl.Buffered(1)),
            pl.BlockSpec((1, d), const),
            pl.BlockSpec((d, LANES), const),
            pl.BlockSpec((1, LANES), const),
        ],
        out_specs=[pl.BlockSpec((tm, d), row), pl.BlockSpec((tm, LANES), row), pl.BlockSpec((tm, d // 2), row),
                   pl.BlockSpec((1, LANES), const)],
        out_shape=[jax.ShapeDtypeStruct((n, d), F32), jax.ShapeDtypeStruct((n, LANES), F32),
                   jax.ShapeDtypeStruct((n, d // 2), jnp.uint32), jax.ShapeDtypeStruct((1, LANES), F32)],
        scratch_shapes=[pltpu.VMEM((1, LANES), F32)],
        compiler_params=_cparams(("arbitrary",)),
        name="out_proj_route",
    )(a_p, b_p, x_p, a_s, b_s, x_s, w, ffn_g, w_r, bias)


def _pow2_pieces(limit):
    p = limit // 2
    while p >= 1:
        yield p
        p //= 2


def _dispatch_kernel(dest_ref, zrow_ref, zcnt_ref, pend_ref, hp_ref, xs_ref, hbuf, zbuf, sem, zsem, *, tm, sub):
    i = pl.program_id(0)
    n = pl.num_programs(0)
    slot = i % 2
    n_exp = zrow_ref.shape[0]
    n_sub_blocks = xs_ref.shape[0] // sub

    def row_copy(sl, r, d):
        return pltpu.make_async_copy(hbuf.at[sl, pl.ds(r, 1), :], xs_ref.at[pl.ds(d, 1), :], sem.at[sl])

    def wait_all(sl):
        for _ in range(TOP_K):
            pltpu.make_async_copy(hbuf.at[sl], xs_ref.at[pl.ds(0, tm), :], sem.at[sl]).wait()

    def zero_copy(row, size):
        return pltpu.make_async_copy(zbuf.at[pl.ds(0, size), :], xs_ref.at[pl.ds(row, size), :], zsem)

    def zero_fill(start):
        def per_expert(e, c):
            row = zrow_ref[e]
            cnt = zcnt_ref[e]
            head = (-row) & (SUBLANES - 1)
            for r in range(SUBLANES - 1):
                @pl.when(r < jnp.minimum(head, cnt))
                def _():
                    cp = zero_copy(row + r, 1)
                    cp.start() if start else cp.wait()
            rest = jnp.maximum(cnt - head, 0)
            row = pl.multiple_of(row + head, SUBLANES)
            for piece in _pow2_pieces(sub):
                if piece < SUBLANES:
                    break
                @pl.when((rest & piece) != 0)
                def _():
                    cp = zero_copy(row, piece)
                    cp.start() if start else cp.wait()
                row = pl.multiple_of(row + (rest & piece), SUBLANES)
            return c
        lax.fori_loop(0, n_exp, per_expert, 0)

        def per_block(b, c):
            @pl.when(b * sub >= pend_ref[0])
            def _():
                cp = zero_copy(b * sub, sub)
                cp.start() if start else cp.wait()
            return c
        lax.fori_loop(0, n_sub_blocks, per_block, 0)

    @pl.when(i == 0)
    def _():
        zbuf[...] = jnp.zeros(zbuf.shape, zbuf.dtype)
        zero_fill(True)

    hbuf[slot] = hp_ref[...]

    def issue(r, c):
        a = (i * tm + r) * TOP_K
        for k in range(TOP_K):
            row_copy(slot, r, dest_ref[a + k]).start(priority=k % 2)
        return c
    lax.fori_loop(0, tm, issue, 0, unroll=8)

    @pl.when(i == 0)
    def _():
        zero_fill(False)

    @pl.when(i > 0)
    def _():
        wait_all(1 - slot)

    @pl.when(i == n - 1)
    def _():
        wait_all(slot)


def _dispatch(dest, zrow, zcnt, pend, hp, *, tm, rows, sub):
    n, dw = hp.shape
    grid_spec = pltpu.PrefetchScalarGridSpec(
        num_scalar_prefetch=4, grid=(n // tm,),
        in_specs=[pl.BlockSpec((tm, dw), lambda i, *_: (i, 0))],
        out_specs=pl.BlockSpec(memory_space=pl.ANY),
        scratch_shapes=[pltpu.VMEM((2, tm, dw), hp.dtype), pltpu.VMEM((sub, dw), hp.dtype),
                        pltpu.SemaphoreType.DMA((2,)), pltpu.SemaphoreType.DMA(())],
    )
    return pl.pallas_call(
        functools.partial(_dispatch_kernel, tm=tm, sub=sub), grid_spec=grid_spec,
        out_shape=jax.ShapeDtypeStruct((rows, dw), hp.dtype),
        compiler_params=_cparams(("arbitrary",)),
        name="moe_dispatch",
    )(dest, zrow, zcnt, pend, hp)


def _experts_kernel(ie_ref, ir_ref, ins_ref, nit_ref, pend_ref, xs_ref, wg_hbm, wu_hbm, wd_hbm, ys_ref,
                    gu_ring, d_ring, xbuf, gacc, uacc, hbuf, ybuf, zbuf,
                    gu_sem, d_sem, x_sem, y_sem, z_sem, *, sub, n_sub_max, kc, fcs):
    n_items = nit_ref[0]
    d = ys_ref.shape[1]
    n_kc = d // kc
    n_wc = n_kc // 2
    n_fc = len(fcs)
    assert n_fc == 2 and n_kc >= 2 and n_kc % 2 == 0
    item_rows = sub * n_sub_max
    f_offs = [sum(fcs[:j]) for j in range(n_fc)]

    def gu_copies(item, pos, slot):
        e = ie_ref[item]
        rows = pl.ds(pos * kc, kc)
        return (pltpu.make_async_copy(wg_hbm.at[e, rows, :], gu_ring.at[slot, 0], gu_sem.at[slot]),
                pltpu.make_async_copy(wu_hbm.at[e, rows, :], gu_ring.at[slot, 1], gu_sem.at[slot]))

    def d_copy(item, j):
        e = ie_ref[item]
        return pltpu.make_async_copy(wd_hbm.at[e, pl.ds(f_offs[j], fcs[j]), :], d_ring.at[j, pl.ds(0, fcs[j]), :],
                                     d_sem.at[j])

    def gu_start(item, pos, slot):
        e = ie_ref[item]
        for m, w_hbm in enumerate((wg_hbm, wu_hbm)):
            for r0 in range(0, kc, sub):
                pltpu.make_async_copy(w_hbm.at[e, pl.ds(pos * kc + r0, sub), :],
                                      gu_ring.at[slot, m, pl.ds(r0, sub), :], gu_sem.at[slot]).start()

    def d_start(item, j):
        e = ie_ref[item]
        for r0 in range(0, fcs[j], sub):
            pltpu.make_async_copy(wd_hbm.at[e, pl.ds(f_offs[j] + r0, sub), :], d_ring.at[j, pl.ds(r0, sub), :],
                                  d_sem.at[j]).start()

    def x_xfer(item, slot, start):
        n_sub = ins_ref[item]
        for sb in range(n_sub_max):
            @pl.when(sb < n_sub)
            def _():
                r = pl.multiple_of(ir_ref[item] + sb * sub, sub)
                for k in range(n_wc):
                    cp = pltpu.make_async_copy(xs_ref.at[pl.ds(r, sub), pl.ds(k * kc, kc)],
                                               xbuf.at[slot, k, pl.ds(sb * sub, sub), :], x_sem.at[slot])
                    cp.start() if start else cp.wait()

    def y_copy(item, sb):
        r = pl.multiple_of(ir_ref[item] + sb * sub, sub)
        return pltpu.make_async_copy(ybuf.at[pl.ds(sb * sub, sub), :], ys_ref.at[pl.ds(r, sub), :], y_sem)

    def for_rows(n_sub, fn):
        for ns in range(1, n_sub_max + 1):
            @pl.when(n_sub == ns)
            def _():
                fn(ns * sub)

    def y_writes(item, start):
        n_sub = ins_ref[item]
        for sb in range(n_sub_max):
            @pl.when(sb < n_sub)
            def _():
                cp = y_copy(item, sb)
                cp.start() if start else cp.wait()

    zbuf[...] = jnp.zeros(zbuf.shape, F32)
    n_out_blocks = ys_ref.shape[0] // sub

    def tail(start):
        def body(b, c):
            @pl.when(b * sub >= pend_ref[0])
            def _():
                cp = pltpu.make_async_copy(zbuf, ys_ref.at[pl.ds(b * sub, sub), :], z_sem)
                cp.start() if start else cp.wait()
            return c
        lax.fori_loop(0, n_out_blocks, body, 0)
    tail(True)

    @pl.when(n_items > 0)
    def _():
        x_xfer(0, 0, True)
        for pos in range(2):
            gu_start(0, pos, pos)

    def item_body(item, carry):
        xslot = item % 2
        n_sub = ins_ref[item]
        has_next = item + 1 < n_items

        @pl.when(has_next)
        def _():
            x_xfer(item + 1, 1 - xslot, True)

        x_xfer(item, xslot, False)

        def zero_acc(rows):
            gacc[0:rows, :] = jnp.zeros((rows, gacc.shape[1]), F32)
            uacc[0:rows, :] = jnp.zeros((rows, uacc.shape[1]), F32)
        for_rows(n_sub, zero_acc)

        def gu_step(pos, c):
            slot = pos % 2
            for cp in gu_copies(item, pos, slot):
                cp.wait()

            def mm(rows):
                xw = xbuf[xslot, pos % n_wc, 0:rows, :]
                shift = jnp.asarray((pos // n_wc) * 16, jnp.uint32)
                xk = pltpu.bitcast((xw << shift) & jnp.uint32(0xFFFF0000), F32).astype(BF16)
                gacc[0:rows, :] += jnp.dot(xk, gu_ring[slot, 0].astype(BF16), preferred_element_type=F32)
                uacc[0:rows, :] += jnp.dot(xk, gu_ring[slot, 1].astype(BF16), preferred_element_type=F32)
            for_rows(n_sub, mm)

            @pl.when(pos + 2 < n_kc)
            def _():
                gu_start(item, pos + 2, slot)
            for j in range(n_fc):
                @pl.when(pos + 2 == n_kc + j)
                def _():
                    d_start(item, j)
            return c
        lax.fori_loop(0, n_kc, gu_step, 0)

        def act(rows):
            g = gacc[0:rows, :]
            hbuf[0:rows, :] = (g * _sigmoid(g) * uacc[0:rows, :]).astype(BF16)
        for_rows(n_sub, act)

        @pl.when(item > 0)
        def _():
            y_writes(item - 1, False)

        for j in range(n_fc):
            d_copy(item, j).wait()

            def mm_down(rows, j=j):
                y = jnp.dot(hbuf[0:rows, f_offs[j]:f_offs[j] + fcs[j]], d_ring[j, 0:fcs[j], :].astype(BF16),
                            preferred_element_type=F32)
                if j == 0:
                    ybuf[0:rows, :] = y
                else:
                    ybuf[0:rows, :] += y
            for_rows(n_sub, mm_down)

            @pl.when(has_next)
            def _():
                gu_start(item + 1, j, j)
        y_writes(item, True)
        return carry

    lax.fori_loop(0, n_items, item_body, 0)

    @pl.when(n_items > 0)
    def _():
        y_writes(n_items - 1, False)
    tail(False)


def _experts(item_e, item_row, item_nsub, n_items, pend, xs, w_gate, w_up, w_down, *, sub, n_sub_max):
    rows = xs.shape[0]
    d, f = w_gate.shape[1:]
    kc = 512
    lanes_f = f // LANES
    fcs = ((lanes_f + 1) // 2 * LANES, lanes_f // 2 * LANES)
    item_rows = sub * n_sub_max
    any_spec = pl.BlockSpec(memory_space=pl.ANY)
    kern = functools.partial(_experts_kernel, sub=sub, n_sub_max=n_sub_max, kc=kc, fcs=fcs)
    return pl.pallas_call(
        kern,
        grid_spec=pltpu.PrefetchScalarGridSpec(
            num_scalar_prefetch=5, grid=(1,),
            in_specs=[any_spec, any_spec, any_spec, any_spec],
            out_specs=any_spec,
            scratch_shapes=[
                pltpu.VMEM((2, 2, kc, f), F32),
                pltpu.VMEM((2, fcs[0], d), F32),
                pltpu.VMEM((2, d // kc // 2, item_rows, kc), xs.dtype),
                pltpu.VMEM((item_rows, f), F32),
                pltpu.VMEM((item_rows, f), F32),
                pltpu.VMEM((item_rows, f), BF16),
                pltpu.VMEM((item_rows, d), F32),
                pltpu.VMEM((sub, d), F32),
                pltpu.SemaphoreType.DMA((2,)),
                pltpu.SemaphoreType.DMA((2,)),
                pltpu.SemaphoreType.DMA((2,)),
                pltpu.SemaphoreType.DMA(()),
                pltpu.SemaphoreType.DMA(()),
            ],
        ),
        out_shape=jax.ShapeDtypeStruct((rows, d), F32),
        compiler_params=_cparams(("arbitrary",), 60 * 1024 * 1024),
        name="moe_experts",
    )(item_e, item_row, item_nsub, n_items, pend, xs, w_gate, w_up, w_down)


def _combine_kernel(dest_ref, x_ref, info_ref, g_ref, ys_ref, yp_ref, ys_out_ref, rbuf, sem, *, tm, n_prompt_tiles):
    i = pl.program_id(0)
    n = pl.num_programs(0)
    slot = i % 2

    def row_copy(sl, k, r, d):
        return pltpu.make_async_copy(ys_ref.at[pl.ds(d, 1), :], rbuf.at[sl, k, pl.ds(r, 1), :], sem.at[sl])

    def issue(step, sl):
        def body(r, c):
            a = (step * tm + r) * TOP_K
            for k in range(TOP_K):
                row_copy(sl, k, r, dest_ref[a + k]).start(priority=k % 2)
            return c
        lax.fori_loop(0, tm, body, 0, unroll=8)

    @pl.when(i == 0)
    def _():
        issue(0, 0)

    @pl.when(i + 1 < n)
    def _():
        issue(i + 1, 1 - slot)

    for k in range(TOP_K):
        pltpu.make_async_copy(ys_ref.at[pl.ds(0, tm), :], rbuf.at[slot, k], sem.at[slot]).wait()

    info = info_ref[...]
    lane_i = lax.broadcasted_iota(jnp.int32, info.shape, 1)
    w1 = jnp.sum(jnp.where(lane_i == 4, info, 0.0), axis=-1, keepdims=True)
    w2 = jnp.sum(jnp.where(lane_i == 5, info, 0.0), axis=-1, keepdims=True)
    y = x_ref[...] + (rbuf[slot, 0] * w1 + rbuf[slot, 1] * w2)
    out = _rms(y, g_ref[...])

    @pl.when(i < n_prompt_tiles)
    def _():
        yp_ref[...] = out

    @pl.when(i >= n_prompt_tiles)
    def _():
        ys_out_ref[...] = out


def _combine(dest, x1, info, final_g, ys, *, tm, n_prompt, n_sample):
    n, d = x1.shape
    npt = n_prompt // tm
    grid_spec = pltpu.PrefetchScalarGridSpec(
        num_scalar_prefetch=1, grid=(n // tm,),
        in_specs=[
            pl.BlockSpec((tm, d), lambda i, ds: (i, 0)),
            pl.BlockSpec((tm, LANES), lambda i, ds: (i, 0)),
            pl.BlockSpec((1, d), lambda i, ds: (0, 0)),
            pl.BlockSpec(memory_space=pl.ANY),
        ],
        out_specs=[
            pl.BlockSpec((tm, d), lambda i, ds: (jnp.minimum(i, npt - 1), 0)),
            pl.BlockSpec((tm, d), lambda i, ds: (jnp.maximum(i - npt, 0), 0)),
        ],
        scratch_shapes=[pltpu.VMEM((2, TOP_K, tm, d), F32), pltpu.SemaphoreType.DMA((2,))],
    )
    return pl.pallas_call(
        functools.partial(_combine_kernel, tm=tm, n_prompt_tiles=npt), grid_spec=grid_spec,
        out_shape=[jax.ShapeDtypeStruct((n_prompt, d), F32), jax.ShapeDtypeStruct((n_sample, d), F32)],
        compiler_params=_cparams(("arbitrary",)),
        name="moe_combine",
    )(dest, x1, info, final_g, ys)


def _rope_tables(pos):
    half = ROPE_DIM // 2
    inv_freq = ROPE_BASE ** (-jnp.arange(half, dtype=F32) / half)
    ang = pos.astype(F32)[:, None] * inv_freq[None, :]
    cos, sin = jnp.cos(ang), jnp.sin(ang)
    z = jnp.zeros_like(cos)
    z2 = jnp.concatenate([z, z], axis=1)
    return (jnp.concatenate([cos, cos, z2], axis=1), jnp.concatenate([-sin, z, z2], axis=1),
            jnp.concatenate([z, sin, z2], axis=1))


def kernel(x_prompt, x_sample, cache_kv_latent, cache_k_rope, state_conv, page_table, attn_norm_g, w_in, q_norm_g,
           w_q_up, kv_norm_g, w_kv_up, conv_w, conv_b, conv_ln_g, conv_ln_b, w_out, ffn_norm_g, w_router_group,
           b_router_group, w_router_expert, b_router_expert, w_exp_gate, w_exp_up, w_exp_down, final_norm_g):
    batch, seq, d = x_prompt.shape
    nb, t_dec, _ = x_sample.shape
    depth, q_lora, n_heads, qk_dim = w_q_up.shape
    kv_lora = w_kv_up.shape[1]
    v_dim = w_kv_up.shape[3] - NOPE_DIM
    conv_ch = conv_w.shape[2]
    width = conv_w.shape[1]
    n_exp = w_exp_gate.shape[1]
    n_grp = w_router_group.shape[2]
    per_grp = n_exp // n_grp
    page = cache_kv_latent.shape[2]
    past = page_table.shape[1] * page
    assert depth == 1 and t_dec == 1 and qk_dim == NOPE_DIM + ROPE_DIM and v_dim == NOPE_DIM
    scale = float(qk_dim) ** -0.5
    n_p = batch * seq
    n_tok = n_p + nb

    w_in_p = w_in[0].astype(BF16)
    wq_p = jnp.pad(w_q_up[0], ((0, 0), (0, 0), (0, HEAD_PAD - qk_dim))).reshape(q_lora, n_heads * HEAD_PAD)
    wq_p = wq_p.astype(BF16)
    wkv_p = w_kv_up[0].reshape(kv_lora, n_heads * HEAD_PAD).astype(BF16)
    w_out_b = w_out[0].astype(BF16)
    w_r = jnp.concatenate([w_router_expert[0], w_router_group[0],
                           jnp.zeros((d, LANES - n_exp - n_grp), F32)], axis=1).astype(BF16)
    b_r = jnp.concatenate([b_router_expert[0], b_router_group[0], jnp.zeros((LANES - n_exp - n_grp,), F32)])[None]
    row2 = lambda v: v.reshape(1, -1)

    tm_p = 512
    rc_p, ra_p, rb_p = _rope_tables(jnp.arange(seq))
    rc_s, ra_s, rb_s = _rope_tables(jnp.full((nb,), past, jnp.int32))
    dims = dict(n_heads=n_heads, q_lora=q_lora, kv_lora=kv_lora, conv_ch=conv_ch)
    xp2 = x_prompt.reshape(n_p, d)
    xs2 = x_sample.reshape(nb, d)
    q_p, c_p, kr_p, u_p, kv_p, krb_p = _in_proj(
        xp2, row2(attn_norm_g[0]), w_in_p, row2(q_norm_g[0]), wq_p, row2(kv_norm_g[0]), wkv_p, rc_p, ra_p, rb_p,
        tm=tm_p, prompt=True, **dims)
    qs_s, c_s, kr_s, u_s = _in_proj(
        xs2, row2(attn_norm_g[0]), w_in_p, row2(q_norm_g[0]), wq_p, row2(kv_norm_g[0]), wkv_p, rc_s, ra_s, rb_s,
        tm=nb, prompt=False, **dims)

    o_att_p = _attn_prompt(q_p, kv_p, krb_p, batch=batch, seq=seq, n_heads=n_heads, v_dim=v_dim, scale=scale)
    cw, cb, cg, cbeta = conv_w[0], row2(conv_b[0]), row2(conv_ln_g[0]), row2(conv_ln_b[0])
    o_conv_p = _conv_prompt(u_p, cw, cb, cg, cbeta, batch=batch, seq=seq)

    o_lat = _attn_sample(page_table, qs_s.reshape(nb, n_heads, QS_HEAD), c_s.reshape(nb, 1, kv_lora),
                         kr_s.reshape(nb, 1, ROPE_DIM), cache_kv_latent[0],
                         jnp.transpose(cache_k_rope[0], (0, 2, 1)), scale=scale)
    o_att_s = _v_up(o_lat.reshape(nb, n_heads * kv_lora), wkv_p, n_heads=n_heads, kv_lora=kv_lora, v_dim=v_dim)
    state_t = jnp.transpose(state_conv[0], (1, 0, 2))
    o_conv_s, new_state_t = _conv_sample(state_t, u_s, cw, cb, cg, cbeta)

    x1, info, hp, counts = _out_proj_route(o_att_p, o_conv_p, xp2, o_att_s, o_conv_s, xs2, w_out_b,
                                           row2(ffn_norm_g[0]), w_r, b_r, tm=512, n_exp=n_exp, n_grp=n_grp,
                                           per_grp=per_grp)
    i32 = jnp.int32
    sub, item_rows = MOE_SUB, MOE_SUB * MOE_ITEM_SUBS
    n_assign = n_tok * TOP_K
    rows_alloc = (-(-n_assign // sub) + n_exp) * sub
    max_items = n_assign // item_rows + n_exp + 1
    cnt = counts[0, :n_exp].astype(i32)
    padded = (cnt + sub - 1) // sub * sub
    pad_end = jnp.cumsum(padded)
    pad_start = pad_end - padded
    e_flat = info[:, 0:TOP_K].astype(i32).reshape(-1)
    rank = info[:, TOP_K:2 * TOP_K].astype(i32).reshape(-1)
    is_e = e_flat[:, None] == jnp.arange(n_exp, dtype=i32)[None, :]
    dest = rank + jnp.sum(jnp.where(is_e, pad_start[None, :], 0), axis=1)
    items_per_e = (padded + item_rows - 1) // item_rows
    it_end = jnp.cumsum(items_per_e)
    it_start = it_end - items_per_e
    w_ids = jnp.arange(max_items, dtype=i32)
    item_e = jnp.minimum(jnp.sum((it_end[None, :] <= w_ids[:, None]).astype(i32), axis=1), n_exp - 1)
    item_i = w_ids - it_start[item_e]
    item_row = jnp.clip(pad_start[item_e] + item_rows * item_i, 0, rows_alloc - item_rows).astype(i32)
    item_nsub = jnp.clip((padded[item_e] - item_rows * item_i) // sub, 1, MOE_ITEM_SUBS).astype(i32)
    n_items = it_end[-1:].astype(i32)
    pend = pad_end[-1:].astype(i32)

    xs = _dispatch(dest, (pad_start + cnt).astype(i32), (padded - cnt).astype(i32), pend, hp, tm=TOK_TILE,
                   rows=rows_alloc, sub=sub)
    ys = _experts(item_e, item_row, item_nsub, n_items, pend, xs, w_exp_gate[0], w_exp_up[0], w_exp_down[0],
                  sub=sub, n_sub_max=MOE_ITEM_SUBS)
    y_p, y_s = _combine(dest, x1, info, row2(final_norm_g), ys, tm=CMB_TILE, n_prompt=n_p, n_sample=nb)

    hist = width - 1
    new_conv_p = u_p.reshape(batch, seq, conv_ch)[:, seq - hist:]
    new_conv_s = jnp.transpose(new_state_t, (1, 0, 2))
    return (y_p.reshape(batch, seq, d), y_s.reshape(nb, t_dec, d),
            c_p.reshape(1, batch, seq, kv_lora), kr_p.reshape(1, batch, seq, ROPE_DIM), new_conv_p[None],
            c_s.reshape(1, nb, t_dec, kv_lora), kr_s.reshape(1, nb, t_dec, ROPE_DIM), new_conv_s[None])
```

```python
import functools

import jax
import jax.numpy as jnp
from jax import lax
from jax.experimental import pallas as pl
from jax.experimental.pallas import tpu as pltpu

F32 = jnp.float32
BF16 = jnp.bfloat16

EPS = 1e-6
NEG_INF = -1e30
ROPE_BASE = 10000.0
LOG2_E = 1.4426950408889634

LANES = 128
SUBLANES = 8
VMEM_LIMIT = 56 * 1024 * 1024

NOPE_DIM = 128
ROPE_DIM = 64
TOP_K = 2
HEAD_PAD = 256
QS_HEAD = 640

MOE_SUB = 128
MOE_ITEM_SUBS = 4
TOK_TILE = 416
CMB_TILE = 128


def _cparams(sem, vmem=VMEM_LIMIT):
    return pltpu.CompilerParams(dimension_semantics=sem, vmem_limit_bytes=vmem)


def _rms(x, g):
    return x * lax.rsqrt(jnp.mean(x * x, axis=-1, keepdims=True) + EPS) * g


def _rope128(v, c, a, b):
    return v * c + pltpu.roll(v, 96, 1) * a + pltpu.roll(v, 32, 1) * b


def _sigmoid(x):
    return 1.0 / (1.0 + jnp.exp(-x))


def _nt_dot(a, b):
    return lax.dot_general(a, b, (((1,), (1,)), ((), ())), preferred_element_type=F32)


def _in_proj_kernel(*refs, n_heads, q_lora, kv_lora, conv_ch, prompt):
    (x_ref, g_ref, w_ref, qg_ref, wq_ref, kvg_ref, wkv_ref, rc_ref, ra_ref, rb_ref) = refs[:10]
    outs = refs[10:]
    if prompt:
        q_ref, c_ref, kr_ref, u_ref, kv_ref, krb_ref = outs
    else:
        q_ref, c_ref, kr_ref, u_ref = outs
    o_kv = q_lora
    o_kr = q_lora + kv_lora
    o_g = o_kr + conv_ch
    n_in = o_kr + ROPE_DIM + 2 * conv_ch

    h = _rms(x_ref[...], g_ref[...]).astype(BF16)
    rc, ra, rb = rc_ref[...], ra_ref[...], rb_ref[...]

    zq = jnp.dot(h, w_ref[:, 0:q_lora], preferred_element_type=F32)
    qn = _rms(zq, qg_ref[...]).astype(BF16)
    qf = jnp.dot(qn, wq_ref[...], preferred_element_type=F32)

    zkv = jnp.dot(h, w_ref[:, o_kv:o_kr], preferred_element_type=F32)
    c = _rms(zkv, kvg_ref[...])
    c_ref[...] = c

    za = jnp.dot(h, w_ref[:, o_kr:o_g + LANES], preferred_element_type=F32)
    zg = jnp.dot(h, w_ref[:, o_g:n_in], preferred_element_type=F32)
    kr = _rope128(za[:, 0:LANES], rc, ra, rb)
    kr_ref[...] = kr[:, :ROPE_DIM]
    glu = za[:, 0:ROPE_DIM + conv_ch] * _sigmoid(zg)
    u_ref[...] = glu[:, ROPE_DIM:]

    if prompt:
        krb_ref[...] = kr.astype(BF16)
        kv_ref[...] = jnp.dot(c.astype(BF16), wkv_ref[...], preferred_element_type=F32).astype(BF16)
        for hd in range(n_heads):
            o = hd * HEAD_PAD
            q_ref[:, o:o + NOPE_DIM] = qf[:, o:o + NOPE_DIM].astype(BF16)
            q_ref[:, o + NOPE_DIM:o + HEAD_PAD] = _rope128(
                qf[:, o + NOPE_DIM:o + HEAD_PAD], rc, ra, rb).astype(BF16)
    else:
        for hd in range(n_heads):
            o = hd * HEAD_PAD
            qnope = qf[:, o:o + NOPE_DIM].astype(BF16)
            wkn = wkv_ref[:, o:o + NOPE_DIM]
            qo = hd * QS_HEAD
            q_ref[:, qo:qo + kv_lora] = _nt_dot(qnope, wkn).astype(BF16)
            q_ref[:, qo + kv_lora:qo + QS_HEAD] = _rope128(
                qf[:, o + NOPE_DIM:o + HEAD_PAD], rc, ra, rb).astype(BF16)


def _in_proj(x, attn_g, w_in_p, q_g, wq_p, kv_g, wkv_p, rc, ra, rb, *, tm, prompt, n_heads, q_lora, kv_lora,
             conv_ch):
    n, d = x.shape
    n_tab = rc.shape[0] // tm
    const = lambda i: (0, 0)
    row = lambda i: (i, 0)
    tab = lambda i: (i % n_tab, 0)
    single = pl.Buffered(1)
    in_specs = [
        pl.BlockSpec((tm, d), row),
        pl.BlockSpec((1, d), const),
        pl.BlockSpec(w_in_p.shape, const, pipeline_mode=single),
        pl.BlockSpec((1, q_lora), const),
        pl.BlockSpec(wq_p.shape, const, pipeline_mode=single),
        pl.BlockSpec((1, kv_lora), const),
        pl.BlockSpec(wkv_p.shape, const, pipeline_mode=single),
        pl.BlockSpec((tm, LANES), tab),
        pl.BlockSpec((tm, LANES), tab),
        pl.BlockSpec((tm, LANES), tab),
    ]
    q_cols = n_heads * (HEAD_PAD if prompt else QS_HEAD)
    out_shape = [
        jax.ShapeDtypeStruct((n, q_cols), BF16),
        jax.ShapeDtypeStruct((n, kv_lora), F32),
        jax.ShapeDtypeStruct((n, ROPE_DIM), F32),
        jax.ShapeDtypeStruct((n, conv_ch), F32),
    ]
    out_specs = [
        pl.BlockSpec((tm, q_cols), row),
        pl.BlockSpec((tm, kv_lora), row),
        pl.BlockSpec((tm, ROPE_DIM), row),
        pl.BlockSpec((tm, conv_ch), row),
    ]
    if prompt:
        out_shape += [jax.ShapeDtypeStruct((n, wkv_p.shape[1]), BF16), jax.ShapeDtypeStruct((n, LANES), BF16)]
        out_specs += [pl.BlockSpec((tm, wkv_p.shape[1]), row), pl.BlockSpec((tm, LANES), row)]
    kern = functools.partial(_in_proj_kernel, n_heads=n_heads, q_lora=q_lora, kv_lora=kv_lora, conv_ch=conv_ch,
                             prompt=prompt)
    return pl.pallas_call(
        kern, grid=(n // tm,), in_specs=in_specs, out_specs=out_specs, out_shape=out_shape,
        compiler_params=_cparams(("arbitrary",)),
        name="in_proj_prompt" if prompt else "in_proj_sample",
    )(x, attn_g, w_in_p, q_g, wq_p, kv_g, wkv_p, rc, ra, rb)


def _attn_prompt_kernel(q_ref, kv_ref, krb_ref, o_ref, kcat_ref, *, tq, tk, scale):
    i = pl.program_id(2)

    @pl.when(i == 0)
    def _():
        kcat_ref[:, :NOPE_DIM] = kv_ref[:, :NOPE_DIM]
        kcat_ref[:, NOPE_DIM:] = krb_ref[...]

    q = q_ref[...]
    v_dim = o_ref.shape[1]

    c = scale * LOG2_E

    def scores(j):
        k = kcat_ref[pl.ds(pl.multiple_of(j * tk, tk), tk), :]
        return _nt_dot(q, k)

    def update(j, s, carry, kw=tk):
        m, l, acc = carry
        m_new = jnp.maximum(m, jnp.max(s, axis=-1, keepdims=True))
        alpha = jnp.exp2((m - m_new) * c)
        p = jnp.exp2((s - m_new) * c)
        v = kv_ref[pl.ds(pl.multiple_of(j * tk, tk), kw), NOPE_DIM:]
        acc = acc * alpha + jnp.dot(p.astype(BF16), v, preferred_element_type=F32)
        return m_new, l * alpha + jnp.sum(p, axis=-1, keepdims=True), acc

    def body(j, carry):
        s_next = scores(j + 1)
        return update(j, carry[3], carry[:3]) + (s_next,)

    init = (jnp.full((tq, 1), NEG_INF, F32), jnp.zeros((tq, 1), F32), jnp.zeros((tq, v_dim), F32), scores(0))
    m, l, acc, s = lax.fori_loop(0, i, body, init)

    def finish(r0, r1, kw):
        rows = slice(r0, r1)
        rowi = lax.broadcasted_iota(jnp.int32, (r1 - r0, kw), 0) + r0
        coli = lax.broadcasted_iota(jnp.int32, (r1 - r0, kw), 1)
        s_blk = jnp.where(coli <= rowi, s[rows, 0:kw], NEG_INF)
        _, l_f, acc_f = update(i, s_blk, (m[rows], l[rows], acc[rows]), kw)
        o_ref[rows, :] = (acc_f / l_f).astype(o_ref.dtype)
    finish(0, tq // 2, tk // 2)
    finish(tq // 2, tq, tk)


def _attn_prompt(q, kv, krb, *, batch, seq, n_heads, v_dim, scale, tq=512):
    n = q.shape[0]
    nq = seq // tq
    kern = functools.partial(_attn_prompt_kernel, tq=tq, tk=tq, scale=scale)
    return pl.pallas_call(
        kern, grid=(batch, n_heads, nq),
        in_specs=[
            pl.BlockSpec((tq, HEAD_PAD), lambda b, h, i: (b * nq + i, h)),
            pl.BlockSpec((seq, HEAD_PAD), lambda b, h, i: (b, h)),
            pl.BlockSpec((seq, LANES), lambda b, h, i: (b, 0)),
        ],
        out_specs=pl.BlockSpec((tq, v_dim), lambda b, h, i: (b * nq + i, h)),
        out_shape=jax.ShapeDtypeStruct((n, n_heads * v_dim), BF16),
        scratch_shapes=[pltpu.VMEM((seq, HEAD_PAD), BF16)],
        compiler_params=_cparams(("arbitrary", "arbitrary", "arbitrary")),
        name="attn_prompt",
    )(q, kv, krb)


def _ln_silu(y, g, b):
    yc = y - jnp.mean(y, axis=-1, keepdims=True)
    z = yc * lax.rsqrt(jnp.mean(yc * yc, axis=-1, keepdims=True) + EPS) * g + b
    return z * _sigmoid(z)


def _conv_prompt_kernel(halo_ref, u_ref, w_ref, b_ref, g_ref, beta_ref, o_ref, win_ref, y_ref, sh_ref, *, tt,
                        halo, width, rc):
    i = pl.program_id(1)

    @pl.when(i == 0)
    def _():
        win_ref[0:halo, :] = jnp.zeros((halo, win_ref.shape[1]), F32)

    @pl.when(i > 0)
    def _():
        win_ref[0:halo, :] = halo_ref[...]

    win_ref[halo:, :] = u_ref[...]
    base = halo - (width - 1)
    ch = win_ref.shape[1]
    n_sh = tt + halo - SUBLANES
    for lc in range(ch // LANES):
        ls = slice(lc * LANES, (lc + 1) * LANES)
        wl = w_ref[:, ls]
        for g in range(1, SUBLANES):
            sh_ref[g, 0:n_sh, :] = win_ref[g:g + n_sh, ls]

        def row_chunk(r, c, ls=ls, wl=wl):
            r0 = pl.multiple_of(r * rc, rc)
            acc = jnp.zeros((rc, LANES), F32)
            for k in range(width):
                g = (base + k) % SUBLANES
                s0 = r0 + ((base + k) - g)
                tap = win_ref[pl.ds(s0, rc), ls] if g == 0 else sh_ref[g, pl.ds(s0, rc), :]
                acc = acc + tap * wl[k:k + 1, :]
            y_ref[pl.ds(r0, rc), ls] = acc
            return c
        lax.fori_loop(0, tt // rc, row_chunk, 0)
    o_ref[...] = _ln_silu(y_ref[...] + b_ref[...], g_ref[...], beta_ref[...]).astype(o_ref.dtype)


def _conv_prompt(u, conv_w, conv_b, ln_g, ln_b, *, batch, seq, tt=256, halo=32, rc=64):
    n, ch = u.shape
    width = conv_w.shape[0]
    nt = seq // tt
    hb = tt // halo
    const = lambda b, i: (0, 0)
    kern = functools.partial(_conv_prompt_kernel, tt=tt, halo=halo, width=width, rc=rc)
    return pl.pallas_call(
        kern, grid=(batch, nt),
        in_specs=[
            pl.BlockSpec((halo, ch), lambda b, i: (jnp.maximum((b * nt + i) * hb - 1, 0), 0)),
            pl.BlockSpec((tt, ch), lambda b, i: (b * nt + i, 0)),
            pl.BlockSpec((width, ch), const),
            pl.BlockSpec((1, ch), const),
            pl.BlockSpec((1, ch), const),
            pl.BlockSpec((1, ch), const),
        ],
        out_specs=pl.BlockSpec((tt, ch), lambda b, i: (b * nt + i, 0)),
        out_shape=jax.ShapeDtypeStruct((n, ch), BF16),
        scratch_shapes=[pltpu.VMEM((halo + tt, ch), F32), pltpu.VMEM((tt, ch), F32),
                        pltpu.VMEM((SUBLANES, halo + tt, LANES), F32)],
        compiler_params=_cparams(("arbitrary", "arbitrary")),
        name="conv_prompt",
    )(u, u, conv_w, conv_b, ln_g, ln_b)


def _conv_sample_kernel(st_ref, u_ref, w_ref, b_ref, g_ref, beta_ref, o_ref, new_ref, *, width):
    w = w_ref[...]
    u = u_ref[...]
    y = u * w[width - 1:width, :] + b_ref[...]
    for k in range(width - 1):
        y = y + st_ref[k] * w[k:k + 1, :]
    o_ref[...] = _ln_silu(y, g_ref[...], beta_ref[...]).astype(o_ref.dtype)
    for k in range(width - 2):
        new_ref[k] = st_ref[k + 1]
    new_ref[width - 2] = u


def _conv_sample(state_t, u, conv_w, conv_b, ln_g, ln_b, *, tb=32):
    hist, nb, ch = state_t.shape
    width = conv_w.shape[0]
    const = lambda i: (0, 0)
    return pl.pallas_call(
        functools.partial(_conv_sample_kernel, width=width), grid=(nb // tb,),
        in_specs=[
            pl.BlockSpec((hist, tb, ch), lambda i: (0, i, 0)),
            pl.BlockSpec((tb, ch), lambda i: (i, 0)),
            pl.BlockSpec((width, ch), const),
            pl.BlockSpec((1, ch), const),
            pl.BlockSpec((1, ch), const),
            pl.BlockSpec((1, ch), const),
        ],
        out_specs=[pl.BlockSpec((tb, ch), lambda i: (i, 0)), pl.BlockSpec((hist, tb, ch), lambda i: (0, i, 0))],
        out_shape=[jax.ShapeDtypeStruct((nb, ch), BF16), jax.ShapeDtypeStruct((hist, nb, ch), F32)],
        compiler_params=_cparams(("arbitrary",)),
        name="conv_sample",
    )(state_t, u, conv_w, conv_b, ln_g, ln_b)


def _attn_sample_kernel(pt_ref, qs_ref, cn_ref, krn_ref, cache_c, cache_krt, o_ref, cbuf, rbuf, sem, m_ref, l_ref,
                        acc_ref, *, pages, n_chunks, page, kv_lora, sub, scale):
    b = pl.program_id(0)
    ch = pl.program_id(1)
    g = b * n_chunks + ch
    n_steps = pl.num_programs(0) * n_chunks
    slot = g % 2

    def copies(step, sl):
        out = []
        for p in range(pages):
            pid = pt_ref[step * pages + p]
            keys = pl.ds(p * page, page)
            out.append(pltpu.make_async_copy(cache_c.at[pid], cbuf.at[sl, keys, :], sem.at[0, sl]))
            out.append(pltpu.make_async_copy(cache_krt.at[pid], rbuf.at[sl, :, keys], sem.at[1, sl]))
        return out

    @pl.when(g == 0)
    def _():
        for cp in copies(0, 0):
            cp.start()

    @pl.when(g + 1 < n_steps)
    def _():
        for cp in copies(g + 1, 1 - slot):
            cp.start()

    @pl.when(ch == 0)
    def _():
        m_ref[...] = jnp.full(m_ref.shape, NEG_INF, F32)
        l_ref[...] = jnp.zeros(l_ref.shape, F32)
        acc_ref[...] = jnp.zeros(acc_ref.shape, F32)

    pltpu.make_async_copy(cbuf.at[slot], cbuf.at[slot], sem.at[0, slot]).wait()
    pltpu.make_async_copy(rbuf.at[slot], rbuf.at[slot], sem.at[1, slot]).wait()

    q = qs_ref[0]
    ql = q[:, :kv_lora]
    qr = q[:, kv_lora:kv_lora + ROPE_DIM]
    cbs, scores = [], []
    for k0 in range(0, pages * page, sub):
        cb = cbuf[slot, k0:k0 + sub, :].astype(BF16)
        rb = rbuf[slot, :, k0:k0 + sub].astype(BF16)
        cbs.append(cb)
        scores.append((_nt_dot(ql, cb) + jnp.dot(qr, rb, preferred_element_type=F32)) * scale)
    probs = []
    for s in scores:
        m_k = jnp.max(s, axis=-1, keepdims=True)
        p = jnp.exp(s - m_k)
        probs.append((m_k, jnp.sum(p, axis=-1, keepdims=True), p.astype(BF16)))
    parts = [(m_k, l_k, jnp.dot(p, cb, preferred_element_type=F32)) for (m_k, l_k, p), cb in zip(probs, cbs)]
    m_old = m_ref[...]
    m_new = m_old
    for m_k, _, _ in parts:
        m_new = jnp.maximum(m_new, m_k)
    alpha = jnp.exp(m_old - m_new)
    l_new = l_ref[...] * alpha
    acc = acc_ref[...] * alpha
    for m_k, l_k, a_k in parts:
        w_k = jnp.exp(m_k - m_new)
        l_new = l_new + l_k * w_k
        acc = acc + a_k * w_k
    l_ref[...] = l_new
    acc_ref[...] = acc
    m_ref[...] = m_new

    @pl.when(ch == n_chunks - 1)
    def _():
        cn = cn_ref[0].astype(BF16).astype(F32)
        krn = krn_ref[0].astype(BF16).astype(F32)
        s_new = (jnp.sum(ql.astype(F32) * cn, axis=-1, keepdims=True)
                 + jnp.sum(qr.astype(F32) * krn, axis=-1, keepdims=True)) * scale
        m_o = m_ref[...]
        m_n = jnp.maximum(m_o, s_new)
        al = jnp.exp(m_o - m_n)
        p_new = jnp.exp(s_new - m_n)
        l_fin = l_ref[...] * al + p_new
        acc = acc_ref[...] * al + p_new.astype(BF16).astype(F32) * cn
        o_ref[0] = acc / l_fin


def _attn_sample(page_table, qs3, c_new3, kr_new3, cache_c, cache_krt, *, scale, pages=32, sub=512):
    nb, n_heads, qw = qs3.shape
    n_pages = page_table.shape[1]
    page, kv_lora = cache_c.shape[1:]
    rope = cache_krt.shape[1]
    n_chunks = n_pages // pages
    keys = pages * page
    kern = functools.partial(_attn_sample_kernel, pages=pages, n_chunks=n_chunks, page=page, kv_lora=kv_lora,
                             sub=sub, scale=scale)
    grid_spec = pltpu.PrefetchScalarGridSpec(
        num_scalar_prefetch=1, grid=(nb, n_chunks),
        in_specs=[
            pl.BlockSpec((1, n_heads, qw), lambda b, c, pt: (b, 0, 0)),
            pl.BlockSpec((1, 1, kv_lora), lambda b, c, pt: (b, 0, 0)),
            pl.BlockSpec((1, 1, rope), lambda b, c, pt: (b, 0, 0)),
            pl.BlockSpec(memory_space=pl.ANY),
            pl.BlockSpec(memory_space=pl.ANY),
        ],
        out_specs=pl.BlockSpec((1, n_heads, kv_lora), lambda b, c, pt: (b, 0, 0)),
        scratch_shapes=[
            pltpu.VMEM((2, keys, kv_lora), F32),
            pltpu.VMEM((2, rope, keys), F32),
            pltpu.SemaphoreType.DMA((2, 2)),
            pltpu.VMEM((n_heads, 1), F32),
            pltpu.VMEM((n_heads, 1), F32),
            pltpu.VMEM((n_heads, kv_lora), F32),
        ],
    )
    return pl.pallas_call(
        kern, grid_spec=grid_spec,
        out_shape=jax.ShapeDtypeStruct((nb, n_heads, kv_lora), F32),
        compiler_params=_cparams(("arbitrary", "arbitrary")),
        name="attn_sample",
    )(page_table.reshape(-1), qs3, c_new3, kr_new3, cache_c, cache_krt)


def _v_up_kernel(ol_ref, wkv_ref, o_ref, *, n_heads, kv_lora, v_dim):
    for hd in range(n_heads):
        ol = ol_ref[:, hd * kv_lora:(hd + 1) * kv_lora].astype(BF16)
        wv = wkv_ref[:, hd * HEAD_PAD + NOPE_DIM:(hd + 1) * HEAD_PAD]
        o_ref[:, hd * v_dim:(hd + 1) * v_dim] = jnp.dot(ol, wv, preferred_element_type=F32).astype(o_ref.dtype)


def _v_up(o_lat2, wkv_p, *, n_heads, kv_lora, v_dim):
    nb = o_lat2.shape[0]
    return pl.pallas_call(
        functools.partial(_v_up_kernel, n_heads=n_heads, kv_lora=kv_lora, v_dim=v_dim),
        out_shape=jax.ShapeDtypeStruct((nb, n_heads * v_dim), BF16),
        compiler_params=_cparams(None),
        name="v_up_sample",
    )(o_lat2, wkv_p)


def _pack_bf16_pairs(h):
    w = h.shape[1] // 2
    hi = pltpu.bitcast(h[:, :w].astype(F32), jnp.uint32)
    lo = pltpu.bitcast(h[:, w:].astype(F32), jnp.uint32)
    return hi | (lo >> 16)


def _route_rows(x1, g, w_r, bias, run_ref, *, n_exp, n_grp, per_grp):
    tm = x1.shape[0]
    h = _rms(x1, g).astype(BF16)
    logits = jnp.dot(h, w_r, preferred_element_type=F32) + bias
    lane_i = lax.broadcasted_iota(jnp.int32, (tm, LANES), 1)
    lane = lane_i.astype(F32)
    big = float(LANES)

    def first_max(vals):
        mx = jnp.max(vals, axis=-1, keepdims=True)
        idx = jnp.min(jnp.where(vals == mx, lane, big), axis=-1, keepdims=True)
        return mx, idx

    is_grp = (lane_i >= n_exp) & (lane_i < n_exp + n_grp)
    lg = jnp.where(is_grp, logits, NEG_INF)
    g_max, g_lane = first_max(lg)
    g_p = 1.0 / jnp.sum(jnp.where(is_grp, jnp.exp(lg - g_max), 0.0), axis=-1, keepdims=True)
    g_idx = g_lane - float(n_exp)
    lo = g_idx * float(per_grp)
    in_grp = (lane >= lo) & (lane < lo + float(per_grp))
    le = jnp.where(in_grp, logits, NEG_INF)
    m1, i1 = first_max(le)
    m2, i2 = first_max(jnp.where(lane == i1, NEG_INF, le))
    e2 = jnp.exp(m2 - m1)
    w1 = g_p / (1.0 + e2)
    w2 = g_p * e2 / (1.0 + e2)

    oh1 = lane == i1
    oh2 = lane == i2
    onehot = jnp.where(oh1 | oh2, 1.0, 0.0)
    r_i = lax.broadcasted_iota(jnp.int32, (tm, tm), 0)
    c_i = lax.broadcasted_iota(jnp.int32, (tm, tm), 1)
    lower = jnp.where(c_i < r_i, 1.0, 0.0).astype(BF16)
    before = run_ref[...] + jnp.dot(lower, onehot.astype(BF16), preferred_element_type=F32)
    rank1 = jnp.sum(jnp.where(oh1, before, 0.0), axis=-1, keepdims=True)
    rank2 = jnp.sum(jnp.where(oh2, before, 0.0), axis=-1, keepdims=True)
    run_ref[...] = run_ref[...] + jnp.sum(onehot, axis=0, keepdims=True)

    info = jnp.where(lane_i == 0, i1, jnp.where(lane_i == 1, i2, jnp.where(lane_i == 2, rank1, jnp.where(
        lane_i == 3, rank2, jnp.where(lane_i == 4, w1, jnp.where(lane_i == 5, w2, 0.0))))))
    return info, _pack_bf16_pairs(h)


def _out_proj_route_kernel(ap_ref, bp_ref, xp_ref, as_ref, bs_ref, xs_ref, w_ref, g_ref, wr_ref, bias_ref,
                           o_ref, info_ref, hp_ref, cnt_ref, run_ref, *, half, n_prompt_tiles, route):
    i = pl.program_id(0)

    @pl.when(i == 0)
    def _():
        run_ref[...] = jnp.zeros(run_ref.shape, F32)

    def tile(a_ref, b_ref, x_ref):
        n = a_ref.shape[0]
        x1 = (x_ref[...] + jnp.dot(a_ref[...], w_ref[0:half, :], preferred_element_type=F32)
              + jnp.dot(b_ref[...], w_ref[half:, :], preferred_element_type=F32))
        info, hp = _route_rows(x1, g_ref[...], wr_ref[...], bias_ref[...], run_ref, **route)
        o_ref[0:n, :] = x1
        info_ref[0:n, :] = info
        hp_ref[0:n, :] = hp

    @pl.when(i < n_prompt_tiles)
    def _():
        tile(ap_ref, bp_ref, xp_ref)

    @pl.when(i == n_prompt_tiles)
    def _():
        tile(as_ref, bs_ref, xs_ref)

    cnt_ref[...] = run_ref[...]


def _out_proj_route(a_p, b_p, x_p, a_s, b_s, x_s, w, ffn_g, w_r, bias, *, tm, n_exp, n_grp, per_grp):
    n_p, half = a_p.shape
    n_s = a_s.shape[0]
    d = w.shape[1]
    n = n_p + n_s
    npt = n_p // tm
    assert n_s <= tm
    const = lambda i: (0, 0)
    prow = lambda i: (jnp.minimum(i, npt - 1), 0)
    row = lambda i: (i, 0)
    kern = functools.partial(_out_proj_route_kernel, half=half, n_prompt_tiles=npt,
                             route=dict(n_exp=n_exp, n_grp=n_grp, per_grp=per_grp))
    return pl.pallas_call(
        kern, grid=(npt + 1,),
        in_specs=[
            pl.BlockSpec((tm, half), prow),
            pl.BlockSpec((tm, half), prow),
            pl.BlockSpec((tm, d), prow),
            pl.BlockSpec((n_s, half), const),
            pl.BlockSpec((n_s, half), const),
            pl.BlockSpec((n_s, d), const),
            pl.BlockSpec(w.shape, const, pipeline_mode=pl.Buffered(1)),
            pl.BlockSpec((1, d), const),
            pl.BlockSpec((d, LANES), const),
            pl.BlockSpec((1, LANES), const),
        ],
        out_specs=[pl.BlockSpec((tm, d), row), pl.BlockSpec((tm, LANES), row), pl.BlockSpec((tm, d // 2), row),
                   pl.BlockSpec((1, LANES), const)],
        out_shape=[jax.ShapeDtypeStruct((n, d), F32), jax.ShapeDtypeStruct((n, LANES), F32),
                   jax.ShapeDtypeStruct((n, d // 2), jnp.uint32), jax.ShapeDtypeStruct((1, LANES), F32)],
        scratch_shapes=[pltpu.VMEM((1, LANES), F32)],
        compiler_params=_cparams(("arbitrary",)),
        name="out_proj_route",
    )(a_p, b_p, x_p, a_s, b_s, x_s, w, ffn_g, w_r, bias)


def _pow2_pieces(limit):
    p = limit // 2
    while p >= 1:
        yield p
        p //= 2


def _dispatch_kernel(dest_ref, zrow_ref, zcnt_ref, pend_ref, hp_ref, xs_ref, hbuf, zbuf, sem, zsem, *, tm, sub):
    i = pl.program_id(0)
    n = pl.num_programs(0)
    slot = i % 2
    n_exp = zrow_ref.shape[0]
    n_sub_blocks = xs_ref.shape[0] // sub

    def row_copy(sl, r, d):
        return pltpu.make_async_copy(hbuf.at[sl, pl.ds(r, 1), :], xs_ref.at[pl.ds(d, 1), :], sem.at[sl])

    def wait_all(sl):
        for _ in range(TOP_K):
            pltpu.make_async_copy(hbuf.at[sl], xs_ref.at[pl.ds(0, tm), :], sem.at[sl]).wait()

    def zero_copy(row, size):
        return pltpu.make_async_copy(zbuf.at[pl.ds(0, size), :], xs_ref.at[pl.ds(row, size), :], zsem)

    def zero_fill(start):
        def per_expert(e, c):
            row = zrow_ref[e]
            cnt = zcnt_ref[e]
            head = (-row) & (SUBLANES - 1)
            for r in range(SUBLANES - 1):
                @pl.when(r < jnp.minimum(head, cnt))
                def _():
                    cp = zero_copy(row + r, 1)
                    cp.start() if start else cp.wait()
            rest = jnp.maximum(cnt - head, 0)
            row = pl.multiple_of(row + head, SUBLANES)
            for piece in _pow2_pieces(sub):
                if piece < SUBLANES:
                    break
                @pl.when((rest & piece) != 0)
                def _():
                    cp = zero_copy(row, piece)
                    cp.start() if start else cp.wait()
                row = pl.multiple_of(row + (rest & piece), SUBLANES)
            return c
        lax.fori_loop(0, n_exp, per_expert, 0)

        def per_block(b, c):
            @pl.when(b * sub >= pend_ref[0])
            def _():
                cp = zero_copy(b * sub, sub)
                cp.start() if start else cp.wait()
            return c
        lax.fori_loop(0, n_sub_blocks, per_block, 0)

    @pl.when(i == 0)
    def _():
        zbuf[...] = jnp.zeros(zbuf.shape, zbuf.dtype)
        zero_fill(True)

    hbuf[slot] = hp_ref[...]

    def issue(r, c):
        a = (i * tm + r) * TOP_K
        for k in range(TOP_K):
            row_copy(slot, r, dest_ref[a + k]).start(priority=k % 2)
        return c
    lax.fori_loop(0, tm, issue, 0, unroll=8)

    @pl.when(i == 0)
    def _():
        zero_fill(False)

    @pl.when(i > 0)
    def _():
        wait_all(1 - slot)

    @pl.when(i == n - 1)
    def _():
        wait_all(slot)


def _dispatch(dest, zrow, zcnt, pend, hp, *, tm, rows, sub):
    n, dw = hp.shape
    grid_spec = pltpu.PrefetchScalarGridSpec(
        num_scalar_prefetch=4, grid=(n // tm,),
        in_specs=[pl.BlockSpec((tm, dw), lambda i, *_: (i, 0))],
        out_specs=pl.BlockSpec(memory_space=pl.ANY),
        scratch_shapes=[pltpu.VMEM((2, tm, dw), hp.dtype), pltpu.VMEM((sub, dw), hp.dtype),
                        pltpu.SemaphoreType.DMA((2,)), pltpu.SemaphoreType.DMA(())],
    )
    return pl.pallas_call(
        functools.partial(_dispatch_kernel, tm=tm, sub=sub), grid_spec=grid_spec,
        out_shape=jax.ShapeDtypeStruct((rows, dw), hp.dtype),
        compiler_params=_cparams(("arbitrary",)),
        name="moe_dispatch",
    )(dest, zrow, zcnt, pend, hp)


def _experts_kernel(ie_ref, ir_ref, ins_ref, nit_ref, pend_ref, xs_ref, wg_hbm, wu_hbm, wd_hbm, ys_ref,
                    gu_ring, d_ring, xbuf, gacc, uacc, hbuf, ybuf, zbuf,
                    gu_sem, d_sem, x_sem, y_sem, z_sem, *, sub, n_sub_max, kc, fcs):
    n_items = nit_ref[0]
    d = ys_ref.shape[1]
    n_kc = d // kc
    n_wc = n_kc // 2
    n_fc = len(fcs)
    assert n_fc == 2 and n_kc >= 2 and n_kc % 2 == 0
    item_rows = sub * n_sub_max
    f_offs = [sum(fcs[:j]) for j in range(n_fc)]

    def gu_copies(item, pos, slot):
        e = ie_ref[item]
        rows = pl.ds(pos * kc, kc)
        return (pltpu.make_async_copy(wg_hbm.at[e, rows, :], gu_ring.at[slot, 0], gu_sem.at[slot]),
                pltpu.make_async_copy(wu_hbm.at[e, rows, :], gu_ring.at[slot, 1], gu_sem.at[slot]))

    def d_copy(item, j):
        e = ie_ref[item]
        return pltpu.make_async_copy(wd_hbm.at[e, pl.ds(f_offs[j], fcs[j]), :], d_ring.at[j, pl.ds(0, fcs[j]), :],
                                     d_sem.at[j])

    def gu_start(item, pos, slot):
        e = ie_ref[item]
        for m, w_hbm in enumerate((wg_hbm, wu_hbm)):
            for r0 in range(0, kc, sub):
                pltpu.make_async_copy(w_hbm.at[e, pl.ds(pos * kc + r0, sub), :],
                                      gu_ring.at[slot, m, pl.ds(r0, sub), :], gu_sem.at[slot]).start()

    def d_start(item, j):
        e = ie_ref[item]
        for r0 in range(0, fcs[j], sub):
            pltpu.make_async_copy(wd_hbm.at[e, pl.ds(f_offs[j] + r0, sub), :], d_ring.at[j, pl.ds(r0, sub), :],
                                  d_sem.at[j]).start()

    def x_xfer(item, slot, start):
        n_sub = ins_ref[item]
        for sb in range(n_sub_max):
            @pl.when(sb < n_sub)
            def _():
                r = pl.multiple_of(ir_ref[item] + sb * sub, sub)
                for k in range(n_wc):
                    cp = pltpu.make_async_copy(xs_ref.at[pl.ds(r, sub), pl.ds(k * kc, kc)],
                                               xbuf.at[slot, k, pl.ds(sb * sub, sub), :], x_sem.at[slot])
                    cp.start() if start else cp.wait()

    def y_copy(item, sb):
        r = pl.multiple_of(ir_ref[item] + sb * sub, sub)
        return pltpu.make_async_copy(ybuf.at[pl.ds(sb * sub, sub), :], ys_ref.at[pl.ds(r, sub), :], y_sem)

    def for_rows(n_sub, fn):
        for ns in range(1, n_sub_max + 1):
            @pl.when(n_sub == ns)
            def _():
                fn(ns * sub)

    def y_writes(item, start):
        n_sub = ins_ref[item]
        for sb in range(n_sub_max):
            @pl.when(sb < n_sub)
            def _():
                cp = y_copy(item, sb)
                cp.start() if start else cp.wait()

    zbuf[...] = jnp.zeros(zbuf.shape, F32)
    n_out_blocks = ys_ref.shape[0] // sub

    def tail(start):
        def body(b, c):
            @pl.when(b * sub >= pend_ref[0])
            def _():
                cp = pltpu.make_async_copy(zbuf, ys_ref.at[pl.ds(b * sub, sub), :], z_sem)
                cp.start() if start else cp.wait()
            return c
        lax.fori_loop(0, n_out_blocks, body, 0)
    tail(True)

    @pl.when(n_items > 0)
    def _():
        x_xfer(0, 0, True)
        for pos in range(2):
            gu_start(0, pos, pos)

    def item_body(item, carry):
        xslot = item % 2
        n_sub = ins_ref[item]
        has_next = item + 1 < n_items

        @pl.when(has_next)
        def _():
            x_xfer(item + 1, 1 - xslot, True)

        x_xfer(item, xslot, False)

        def zero_acc(rows):
            gacc[0:rows, :] = jnp.zeros((rows, gacc.shape[1]), F32)
            uacc[0:rows, :] = jnp.zeros((rows, uacc.shape[1]), F32)
        for_rows(n_sub, zero_acc)

        def gu_step(pos, c):
            slot = pos % 2
            for cp in gu_copies(item, pos, slot):
                cp.wait()

            def mm(rows):
                xw = xbuf[xslot, pos % n_wc, 0:rows, :]
                shift = jnp.asarray((pos // n_wc) * 16, jnp.uint32)
                xk = pltpu.bitcast((xw << shift) & jnp.uint32(0xFFFF0000), F32).astype(BF16)
                gacc[0:rows, :] += jnp.dot(xk, gu_ring[slot, 0].astype(BF16), preferred_element_type=F32)
                uacc[0:rows, :] += jnp.dot(xk, gu_ring[slot, 1].astype(BF16), preferred_element_type=F32)
            for_rows(n_sub, mm)

            @pl.when(pos + 2 < n_kc)
            def _():
                gu_start(item, pos + 2, slot)
            for j in range(n_fc):
                @pl.when(pos + 2 == n_kc + j)
                def _():
                    d_start(item, j)
            return c
        lax.fori_loop(0, n_kc, gu_step, 0)

        def act(rows):
            g = gacc[0:rows, :]
            hbuf[0:rows, :] = (g * _sigmoid(g) * uacc[0:rows, :]).astype(BF16)
        for_rows(n_sub, act)

        @pl.when(item > 0)
        def _():
            y_writes(item - 1, False)

        for j in range(n_fc):
            d_copy(item, j).wait()

            def mm_down(rows, j=j):
                y = jnp.dot(hbuf[0:rows, f_offs[j]:f_offs[j] + fcs[j]], d_ring[j, 0:fcs[j], :].astype(BF16),
                            preferred_element_type=F32)
                if j == 0:
                    ybuf[0:rows, :] = y
                else:
                    ybuf[0:rows, :] += y
            for_rows(n_sub, mm_down)

            @pl.when(has_next)
            def _():
                gu_start(item + 1, j, j)
        y_writes(item, True)
        return carry

    lax.fori_loop(0, n_items, item_body, 0)

    @pl.when(n_items > 0)
    def _():
        y_writes(n_items - 1, False)
    tail(False)


def _experts(item_e, item_row, item_nsub, n_items, pend, xs, w_gate, w_up, w_down, *, sub, n_sub_max):
    rows = xs.shape[0]
    d, f = w_gate.shape[1:]
    kc = 512
    lanes_f = f // LANES
    fcs = ((lanes_f + 1) // 2 * LANES, lanes_f // 2 * LANES)
    item_rows = sub * n_sub_max
    any_spec = pl.BlockSpec(memory_space=pl.ANY)
    kern = functools.partial(_experts_kernel, sub=sub, n_sub_max=n_sub_max, kc=kc, fcs=fcs)
    return pl.pallas_call(
        kern,
        grid_spec=pltpu.PrefetchScalarGridSpec(
            num_scalar_prefetch=5, grid=(1,),
            in_specs=[any_spec, any_spec, any_spec, any_spec],
            out_specs=any_spec,
            scratch_shapes=[
                pltpu.VMEM((2, 2, kc, f), F32),
                pltpu.VMEM((2, fcs[0], d), F32),
                pltpu.VMEM((2, d // kc // 2, item_rows, kc), xs.dtype),
                pltpu.VMEM((item_rows, f), F32),
                pltpu.VMEM((item_rows, f), F32),
                pltpu.VMEM((item_rows, f), BF16),
                pltpu.VMEM((item_rows, d), F32),
                pltpu.VMEM((sub, d), F32),
                pltpu.SemaphoreType.DMA((2,)),
                pltpu.SemaphoreType.DMA((2,)),
                pltpu.SemaphoreType.DMA((2,)),
                pltpu.SemaphoreType.DMA(()),
                pltpu.SemaphoreType.DMA(()),
            ],
        ),
        out_shape=jax.ShapeDtypeStruct((rows, d), F32),
        compiler_params=_cparams(("arbitrary",), 60 * 1024 * 1024),
        name="moe_experts",
    )(item_e, item_row, item_nsub, n_items, pend, xs, w_gate, w_up, w_down)


def _combine_kernel(dest_ref, x_ref, info_ref, g_ref, ys_ref, yp_ref, ys_out_ref, rbuf, sem, *, tm, n_prompt_tiles):
    i = pl.program_id(0)
    n = pl.num_programs(0)
    slot = i % 2

    def row_copy(sl, k, r, d):
        return pltpu.make_async_copy(ys_ref.at[pl.ds(d, 1), :], rbuf.at[sl, k, pl.ds(r, 1), :], sem.at[sl])

    def issue(step, sl):
        def body(r, c):
            a = (step * tm + r) * TOP_K
            for k in range(TOP_K):
                row_copy(sl, k, r, dest_ref[a + k]).start(priority=k % 2)
            return c
        lax.fori_loop(0, tm, body, 0, unroll=8)

    @pl.when(i == 0)
    def _():
        issue(0, 0)

    @pl.when(i + 1 < n)
    def _():
        issue(i + 1, 1 - slot)

    for k in range(TOP_K):
        pltpu.make_async_copy(ys_ref.at[pl.ds(0, tm), :], rbuf.at[slot, k], sem.at[slot]).wait()

    info = info_ref[...]
    lane_i = lax.broadcasted_iota(jnp.int32, info.shape, 1)
    w1 = jnp.sum(jnp.where(lane_i == 4, info, 0.0), axis=-1, keepdims=True)
    w2 = jnp.sum(jnp.where(lane_i == 5, info, 0.0), axis=-1, keepdims=True)
    y = x_ref[...] + (rbuf[slot, 0] * w1 + rbuf[slot, 1] * w2)
    out = _rms(y, g_ref[...])

    @pl.when(i < n_prompt_tiles)
    def _():
        yp_ref[...] = out

    @pl.when(i >= n_prompt_tiles)
    def _():
        ys_out_ref[...] = out


def _combine(dest, x1, info, final_g, ys, *, tm, n_prompt, n_sample):
    n, d = x1.shape
    npt = n_prompt // tm
    grid_spec = pltpu.PrefetchScalarGridSpec(
        num_scalar_prefetch=1, grid=(n // tm,),
        in_specs=[
            pl.BlockSpec((tm, d), lambda i, ds: (i, 0)),
            pl.BlockSpec((tm, LANES), lambda i, ds: (i, 0)),
            pl.BlockSpec((1, d), lambda i, ds: (0, 0)),
            pl.BlockSpec(memory_space=pl.ANY),
        ],
        out_specs=[
            pl.BlockSpec((tm, d), lambda i, ds: (jnp.minimum(i, npt - 1), 0)),
            pl.BlockSpec((tm, d), lambda i, ds: (jnp.maximum(i - npt, 0), 0)),
        ],
        scratch_shapes=[pltpu.VMEM((2, TOP_K, tm, d), F32), pltpu.SemaphoreType.DMA((2,))],
    )
    return pl.pallas_call(
        functools.partial(_combine_kernel, tm=tm, n_prompt_tiles=npt), grid_spec=grid_spec,
        out_shape=[jax.ShapeDtypeStruct((n_prompt, d), F32), jax.ShapeDtypeStruct((n_sample, d), F32)],
        compiler_params=_cparams(("arbitrary",)),
        name="moe_combine",
    )(dest, x1, info, final_g, ys)


def _rope_tables(pos):
    half = ROPE_DIM // 2
    inv_freq = ROPE_BASE ** (-jnp.arange(half, dtype=F32) / half)
    ang = pos.astype(F32)[:, None] * inv_freq[None, :]
    cos, sin = jnp.cos(ang), jnp.sin(ang)
    z = jnp.zeros_like(cos)
    z2 = jnp.concatenate([z, z], axis=1)
    return (jnp.concatenate([cos, cos, z2], axis=1), jnp.concatenate([-sin, z, z2], axis=1),
            jnp.concatenate([z, sin, z2], axis=1))


def kernel(x_prompt, x_sample, cache_kv_latent, cache_k_rope, state_conv, page_table, attn_norm_g, w_in, q_norm_g,
           w_q_up, kv_norm_g, w_kv_up, conv_w, conv_b, conv_ln_g, conv_ln_b, w_out, ffn_norm_g, w_router_group,
           b_router_group, w_router_expert, b_router_expert, w_exp_gate, w_exp_up, w_exp_down, final_norm_g):
    batch, seq, d = x_prompt.shape
    nb, t_dec, _ = x_sample.shape
    depth, q_lora, n_heads, qk_dim = w_q_up.shape
    kv_lora = w_kv_up.shape[1]
    v_dim = w_kv_up.shape[3] - NOPE_DIM
    conv_ch = conv_w.shape[2]
    width = conv_w.shape[1]
    n_exp = w_exp_gate.shape[1]
    n_grp = w_router_group.shape[2]
    per_grp = n_exp // n_grp
    page = cache_kv_latent.shape[2]
    past = page_table.shape[1] * page
    assert depth == 1 and t_dec == 1 and qk_dim == NOPE_DIM + ROPE_DIM and v_dim == NOPE_DIM
    scale = float(qk_dim) ** -0.5
    n_p = batch * seq
    n_tok = n_p + nb

    w_in_p = w_in[0].astype(BF16)
    wq_p = jnp.pad(w_q_up[0], ((0, 0), (0, 0), (0, HEAD_PAD - qk_dim))).reshape(q_lora, n_heads * HEAD_PAD)
    wq_p = wq_p.astype(BF16)
    wkv_p = w_kv_up[0].reshape(kv_lora, n_heads * HEAD_PAD).astype(BF16)
    w_out_b = w_out[0].astype(BF16)
    w_r = jnp.concatenate([w_router_expert[0], w_router_group[0],
                           jnp.zeros((d, LANES - n_exp - n_grp), F32)], axis=1).astype(BF16)
    b_r = jnp.concatenate([b_router_expert[0], b_router_group[0], jnp.zeros((LANES - n_exp - n_grp,), F32)])[None]
    row2 = lambda v: v.reshape(1, -1)

    tm_p = 512
    rc_p, ra_p, rb_p = _rope_tables(jnp.arange(seq))
    rc_s, ra_s, rb_s = _rope_tables(jnp.full((nb,), past, jnp.int32))
    dims = dict(n_heads=n_heads, q_lora=q_lora, kv_lora=kv_lora, conv_ch=conv_ch)
    xp2 = x_prompt.reshape(n_p, d)
    xs2 = x_sample.reshape(nb, d)
    q_p, c_p, kr_p, u_p, kv_p, krb_p = _in_proj(
        xp2, row2(attn_norm_g[0]), w_in_p, row2(q_norm_g[0]), wq_p, row2(kv_norm_g[0]), wkv_p, rc_p, ra_p, rb_p,
        tm=tm_p, prompt=True, **dims)
    qs_s, c_s, kr_s, u_s = _in_proj(
        xs2, row2(attn_norm_g[0]), w_in_p, row2(q_norm_g[0]), wq_p, row2(kv_norm_g[0]), wkv_p, rc_s, ra_s, rb_s,
        tm=nb, prompt=False, **dims)

    o_att_p = _attn_prompt(q_p, kv_p, krb_p, batch=batch, seq=seq, n_heads=n_heads, v_dim=v_dim, scale=scale)
    cw, cb, cg, cbeta = conv_w[0], row2(conv_b[0]), row2(conv_ln_g[0]), row2(conv_ln_b[0])
    o_conv_p = _conv_prompt(u_p, cw, cb, cg, cbeta, batch=batch, seq=seq)

    o_lat = _attn_sample(page_table, qs_s.reshape(nb, n_heads, QS_HEAD), c_s.reshape(nb, 1, kv_lora),
                         kr_s.reshape(nb, 1, ROPE_DIM), cache_kv_latent[0],
                         jnp.transpose(cache_k_rope[0], (0, 2, 1)), scale=scale)
    o_att_s = _v_up(o_lat.reshape(nb, n_heads * kv_lora), wkv_p, n_heads=n_heads, kv_lora=kv_lora, v_dim=v_dim)
    state_t = jnp.transpose(state_conv[0], (1, 0, 2))
    o_conv_s, new_state_t = _conv_sample(state_t, u_s, cw, cb, cg, cbeta)

    x1, info, hp, counts = _out_proj_route(o_att_p, o_conv_p, xp2, o_att_s, o_conv_s, xs2, w_out_b,
                                           row2(ffn_norm_g[0]), w_r, b_r, tm=512, n_exp=n_exp, n_grp=n_grp,
                                           per_grp=per_grp)
    i32 = jnp.int32
    sub, item_rows = MOE_SUB, MOE_SUB * MOE_ITEM_SUBS
    n_assign = n_tok * TOP_K
    rows_alloc = (-(-n_assign // sub) + n_exp) * sub
    max_items = n_assign // item_rows + n_exp + 1
    cnt = counts[0, :n_exp].astype(i32)
    padded = (cnt + sub - 1) // sub * sub
    pad_end = jnp.cumsum(padded)
    pad_start = pad_end - padded
    e_flat = info[:, 0:TOP_K].astype(i32).reshape(-1)
    rank = info[:, TOP_K:2 * TOP_K].astype(i32).reshape(-1)
    is_e = e_flat[:, None] == jnp.arange(n_exp, dtype=i32)[None, :]
    dest = rank + jnp.sum(jnp.where(is_e, pad_start[None, :], 0), axis=1)
    items_per_e = (padded + item_rows - 1) // item_rows
    it_end = jnp.cumsum(items_per_e)
    it_start = it_end - items_per_e
    w_ids = jnp.arange(max_items, dtype=i32)
    item_e = jnp.minimum(jnp.sum((it_end[None, :] <= w_ids[:, None]).astype(i32), axis=1), n_exp - 1)
    item_i = w_ids - it_start[item_e]
    item_row = jnp.clip(pad_start[item_e] + item_rows * item_i, 0, rows_alloc - item_rows).astype(i32)
    item_nsub = jnp.clip((padded[item_e] - item_rows * item_i) // sub, 1, MOE_ITEM_SUBS).astype(i32)
    n_items = it_end[-1:].astype(i32)
    pend = pad_end[-1:].astype(i32)

    xs = _dispatch(dest, (pad_start + cnt).astype(i32), (padded - cnt).astype(i32), pend, hp, tm=TOK_TILE,
                   rows=rows_alloc, sub=sub)
    ys = _experts(item_e, item_row, item_nsub, n_items, pend, xs, w_exp_gate[0], w_exp_up[0], w_exp_down[0],
                  sub=sub, n_sub_max=MOE_ITEM_SUBS)
    y_p, y_s = _combine(dest, x1, info, row2(final_norm_g), ys, tm=CMB_TILE, n_prompt=n_p, n_sample=nb)

    hist = width - 1
    new_conv_p = u_p.reshape(batch, seq, conv_ch)[:, seq - hist:]
    new_conv_s = jnp.transpose(new_state_t, (1, 0, 2))
    return (y_p.reshape(batch, seq, d), y_s.reshape(nb, t_dec, d),
            c_p.reshape(1, batch, seq, kv_lora), kr_p.reshape(1, batch, seq, ROPE_DIM), new_conv_p[None],
            c_s.reshape(1, nb, t_dec, kv_lora), kr_s.reshape(1, nb, t_dec, ROPE_DIM), new_conv_s[None])
```

```python
import functools

import jax
import jax.numpy as jnp
from jax import lax
from jax.experimental import pallas as pl
from jax.experimental.pallas import tpu as pltpu

F32 = jnp.float32
BF16 = jnp.bfloat16

EPS = 1e-6
NEG_INF = -1e30
ROPE_BASE = 10000.0
LOG2_E = 1.4426950408889634

LANES = 128
SUBLANES = 8
VMEM_LIMIT = 56 * 1024 * 1024

NOPE_DIM = 128
ROPE_DIM = 64
TOP_K = 2
HEAD_PAD = 256
QS_HEAD = 640

MOE_SUB = 128
MOE_ITEM_SUBS = 4
TOK_TILE = 416
CMB_TILE = 128


def _cparams(sem, vmem=VMEM_LIMIT):
    return pltpu.CompilerParams(dimension_semantics=sem, vmem_limit_bytes=vmem)


def _rms(x, g):
    return x * lax.rsqrt(jnp.mean(x * x, axis=-1, keepdims=True) + EPS) * g


def _rope128(v, c, a, b):
    return v * c + pltpu.roll(v, 96, 1) * a + pltpu.roll(v, 32, 1) * b


def _sigmoid(x):
    return 1.0 / (1.0 + jnp.exp(-x))


def _nt_dot(a, b):
    return lax.dot_general(a, b, (((1,), (1,)), ((), ())), preferred_element_type=F32)


def _in_proj_kernel(*refs, n_heads, q_lora, kv_lora, conv_ch, prompt):
    (x_ref, g_ref, w_ref, qg_ref, wq_ref, kvg_ref, wkv_ref, rc_ref, ra_ref, rb_ref) = refs[:10]
    outs = refs[10:]
    if prompt:
        q_ref, c_ref, kr_ref, u_ref, kv_ref, krb_ref = outs
    else:
        q_ref, c_ref, kr_ref, u_ref = outs
    o_kv = q_lora
    o_kr = q_lora + kv_lora
    o_g = o_kr + conv_ch
    n_in = o_kr + ROPE_DIM + 2 * conv_ch

    h = _rms(x_ref[...], g_ref[...]).astype(BF16)
    rc, ra, rb = rc_ref[...], ra_ref[...], rb_ref[...]

    zq = jnp.dot(h, w_ref[:, 0:q_lora], preferred_element_type=F32)
    qn = _rms(zq, qg_ref[...]).astype(BF16)
    qf = jnp.dot(qn, wq_ref[...], preferred_element_type=F32)

    zkv = jnp.dot(h, w_ref[:, o_kv:o_kr], preferred_element_type=F32)
    c = _rms(zkv, kvg_ref[...])
    c_ref[...] = c

    za = jnp.dot(h, w_ref[:, o_kr:o_g + LANES], preferred_element_type=F32)
    zg = jnp.dot(h, w_ref[:, o_g:n_in], preferred_element_type=F32)
    kr = _rope128(za[:, 0:LANES], rc, ra, rb)
    kr_ref[...] = kr[:, :ROPE_DIM]
    glu = za[:, 0:ROPE_DIM + conv_ch] * _sigmoid(zg)
    u_ref[...] = glu[:, ROPE_DIM:]

    if prompt:
        krb_ref[...] = kr.astype(BF16)
        kv_ref[...] = jnp.dot(c.astype(BF16), wkv_ref[...], preferred_element_type=F32).astype(BF16)
        for hd in range(n_heads):
            o = hd * HEAD_PAD
            q_ref[:, o:o + NOPE_DIM] = qf[:, o:o + NOPE_DIM].astype(BF16)
            q_ref[:, o + NOPE_DIM:o + HEAD_PAD] = _rope128(
                qf[:, o + NOPE_DIM:o + HEAD_PAD], rc, ra, rb).astype(BF16)
    else:
        for hd in range(n_heads):
            o = hd * HEAD_PAD
            qnope = qf[:, o:o + NOPE_DIM].astype(BF16)
            wkn = wkv_ref[:, o:o + NOPE_DIM]
            qo = hd * QS_HEAD
            q_ref[:, qo:qo + kv_lora] = _nt_dot(qnope, wkn).astype(BF16)
            q_ref[:, qo + kv_lora:qo + QS_HEAD] = _rope128(
                qf[:, o + NOPE_DIM:o + HEAD_PAD], rc, ra, rb).astype(BF16)


def _in_proj(x, attn_g, w_in_p, q_g, wq_p, kv_g, wkv_p, rc, ra, rb, *, tm, prompt, n_heads, q_lora, kv_lora,
             conv_ch):
    n, d = x.shape
    n_tab = rc.shape[0] // tm
    const = lambda i: (0, 0)
    row = lambda i: (i, 0)
    tab = lambda i: (i % n_tab, 0)
    single = pl.Buffered(1)
    in_specs = [
        pl.BlockSpec((tm, d), row),
        pl.BlockSpec((1, d), const),
        pl.BlockSpec(w_in_p.shape, const, pipeline_mode=single),
        pl.BlockSpec((1, q_lora), const),
        pl.BlockSpec(wq_p.shape, const, pipeline_mode=single),
        pl.BlockSpec((1, kv_lora), const),
        pl.BlockSpec(wkv_p.shape, const, pipeline_mode=single),
        pl.BlockSpec((tm, LANES), tab),
        pl.BlockSpec((tm, LANES), tab),
        pl.BlockSpec((tm, LANES), tab),
    ]
    q_cols = n_heads * (HEAD_PAD if prompt else QS_HEAD)
    out_shape = [
        jax.ShapeDtypeStruct((n, q_cols), BF16),
        jax.ShapeDtypeStruct((n, kv_lora), F32),
        jax.ShapeDtypeStruct((n, ROPE_DIM), F32),
        jax.ShapeDtypeStruct((n, conv_ch), F32),
    ]
    out_specs = [
        pl.BlockSpec((tm, q_cols), row),
        pl.BlockSpec((tm, kv_lora), row),
        pl.BlockSpec((tm, ROPE_DIM), row),
        pl.BlockSpec((tm, conv_ch), row),
    ]
    if prompt:
        out_shape += [jax.ShapeDtypeStruct((n, wkv_p.shape[1]), BF16), jax.ShapeDtypeStruct((n, LANES), BF16)]
        out_specs += [pl.BlockSpec((tm, wkv_p.shape[1]), row), pl.BlockSpec((tm, LANES), row)]
    kern = functools.partial(_in_proj_kernel, n_heads=n_heads, q_lora=q_lora, kv_lora=kv_lora, conv_ch=conv_ch,
                             prompt=prompt)
    return pl.pallas_call(
        kern, grid=(n // tm,), in_specs=in_specs, out_specs=out_specs, out_shape=out_shape,
        compiler_params=_cparams(("arbitrary",)),
        name="in_proj_prompt" if prompt else "in_proj_sample",
    )(x, attn_g, w_in_p, q_g, wq_p, kv_g, wkv_p, rc, ra, rb)


def _attn_prompt_kernel(q_ref, kv_ref, krb_ref, o_ref, kcat_ref, *, tq, tk, scale):
    i = pl.program_id(2)

    @pl.when(i == 0)
    def _():
        kcat_ref[:, :NOPE_DIM] = kv_ref[:, :NOPE_DIM]
        kcat_ref[:, NOPE_DIM:] = krb_ref[...]

    q = q_ref[...]
    v_dim = o_ref.shape[1]

    c = scale * LOG2_E

    def scores(j):
        k = kcat_ref[pl.ds(pl.multiple_of(j * tk, tk), tk), :]
        return _nt_dot(q, k)

    def update(j, s, carry, kw=tk):
        m, l, acc = carry
        m_new = jnp.maximum(m, jnp.max(s, axis=-1, keepdims=True))
        alpha = jnp.exp2((m - m_new) * c)
        p = jnp.exp2((s - m_new) * c)
        v = kv_ref[pl.ds(pl.multiple_of(j * tk, tk), kw), NOPE_DIM:]
        acc = acc * alpha + jnp.dot(p.astype(BF16), v, preferred_element_type=F32)
        return m_new, l * alpha + jnp.sum(p, axis=-1, keepdims=True), acc

    def body(j, carry):
        s_next = scores(j + 1)
        return update(j, carry[3], carry[:3]) + (s_next,)

    init = (jnp.full((tq, 1), NEG_INF, F32), jnp.zeros((tq, 1), F32), jnp.zeros((tq, v_dim), F32), scores(0))
    m, l, acc, s = lax.fori_loop(0, i, body, init)

    def finish(r0, r1, kw):
        rows = slice(r0, r1)
        rowi = lax.broadcasted_iota(jnp.int32, (r1 - r0, kw), 0) + r0
        coli = lax.broadcasted_iota(jnp.int32, (r1 - r0, kw), 1)
        s_blk = jnp.where(coli <= rowi, s[rows, 0:kw], NEG_INF)
        _, l_f, acc_f = update(i, s_blk, (m[rows], l[rows], acc[rows]), kw)
        o_ref[rows, :] = (acc_f / l_f).astype(o_ref.dtype)
    finish(0, tq // 2, tk // 2)
    finish(tq // 2, tq, tk)


def _attn_prompt(q, kv, krb, *, batch, seq, n_heads, v_dim, scale, tq=512):
    n = q.shape[0]
    nq = seq // tq
    kern = functools.partial(_attn_prompt_kernel, tq=tq, tk=tq, scale=scale)
    return pl.pallas_call(
        kern, grid=(batch, n_heads, nq),
        in_specs=[
            pl.BlockSpec((tq, HEAD_PAD), lambda b, h, i: (b * nq + i, h)),
            pl.BlockSpec((seq, HEAD_PAD), lambda b, h, i: (b, h)),
            pl.BlockSpec((seq, LANES), lambda b, h, i: (b, 0)),
        ],
        out_specs=pl.BlockSpec((tq, v_dim), lambda b, h, i: (b * nq + i, h)),
        out_shape=jax.ShapeDtypeStruct((n, n_heads * v_dim), BF16),
        scratch_shapes=[pltpu.VMEM((seq, HEAD_PAD), BF16)],
        compiler_params=_cparams(("arbitrary", "arbitrary", "arbitrary")),
        name="attn_prompt",
    )(q, kv, krb)


def _ln_silu(y, g, b):
    yc = y - jnp.mean(y, axis=-1, keepdims=True)
    z = yc * lax.rsqrt(jnp.mean(yc * yc, axis=-1, keepdims=True) + EPS) * g + b
    return z * _sigmoid(z)


def _conv_prompt_kernel(halo_ref, u_ref, w_ref, b_ref, g_ref, beta_ref, o_ref, win_ref, y_ref, sh_ref, *, tt,
                        halo, width, rc):
    i = pl.program_id(1)

    @pl.when(i == 0)
    def _():
        win_ref[0:halo, :] = jnp.zeros((halo, win_ref.shape[1]), F32)

    @pl.when(i > 0)
    def _():
        win_ref[0:halo, :] = halo_ref[...]

    win_ref[halo:, :] = u_ref[...]
    base = halo - (width - 1)
    ch = win_ref.shape[1]
    n_sh = tt + halo - SUBLANES
    for lc in range(ch // LANES):
        ls = slice(lc * LANES, (lc + 1) * LANES)
        wl = w_ref[:, ls]
        for g in range(1, SUBLANES):
            sh_ref[g, 0:n_sh, :] = win_ref[g:g + n_sh, ls]

        def row_chunk(r, c, ls=ls, wl=wl):
            r0 = pl.multiple_of(r * rc, rc)
            acc = jnp.zeros((rc, LANES), F32)
            for k in range(width):
                g = (base + k) % SUBLANES
                s0 = r0 + ((base + k) - g)
                tap = win_ref[pl.ds(s0, rc), ls] if g == 0 else sh_ref[g, pl.ds(s0, rc), :]
                acc = acc + tap * wl[k:k + 1, :]
            y_ref[pl.ds(r0, rc), ls] = acc
            return c
        lax.fori_loop(0, tt // rc, row_chunk, 0)
    o_ref[...] = _ln_silu(y_ref[...] + b_ref[...], g_ref[...], beta_ref[...]).astype(o_ref.dtype)


def _conv_prompt(u, conv_w, conv_b, ln_g, ln_b, *, batch, seq, tt=256, halo=32, rc=128):
    n, ch = u.shape
    width = conv_w.shape[0]
    nt = seq // tt
    hb = tt // halo
    const = lambda b, i: (0, 0)
    kern = functools.partial(_conv_prompt_kernel, tt=tt, halo=halo, width=width, rc=rc)
    return pl.pallas_call(
        kern, grid=(batch, nt),
        in_specs=[
            pl.BlockSpec((halo, ch), lambda b, i: (jnp.maximum((b * nt + i) * hb - 1, 0), 0)),
            pl.BlockSpec((tt, ch), lambda b, i: (b * nt + i, 0)),
            pl.BlockSpec((width, ch), const),
            pl.BlockSpec((1, ch), const),
            pl.BlockSpec((1, ch), const),
            pl.BlockSpec((1, ch), const),
        ],
        out_specs=pl.BlockSpec((tt, ch), lambda b, i: (b * nt + i, 0)),
        out_shape=jax.ShapeDtypeStruct((n, ch), BF16),
        scratch_shapes=[pltpu.VMEM((halo + tt, ch), F32), pltpu.VMEM((tt, ch), F32),
                        pltpu.VMEM((SUBLANES, halo + tt, LANES), F32)],
        compiler_params=_cparams(("arbitrary", "arbitrary")),
        name="conv_prompt",
    )(u, u, conv_w, conv_b, ln_g, ln_b)


def _conv_sample_kernel(st_ref, u_ref, w_ref, b_ref, g_ref, beta_ref, o_ref, new_ref, *, width):
    w = w_ref[...]
    u = u_ref[...]
    y = u * w[width - 1:width, :] + b_ref[...]
    for k in range(width - 1):
        y = y + st_ref[k] * w[k:k + 1, :]
    o_ref[...] = _ln_silu(y, g_ref[...], beta_ref[...]).astype(o_ref.dtype)
    for k in range(width - 2):
        new_ref[k] = st_ref[k + 1]
    new_ref[width - 2] = u


def _conv_sample(state_t, u, conv_w, conv_b, ln_g, ln_b, *, tb=32):
    hist, nb, ch = state_t.shape
    width = conv_w.shape[0]
    const = lambda i: (0, 0)
    return pl.pallas_call(
        functools.partial(_conv_sample_kernel, width=width), grid=(nb // tb,),
        in_specs=[
            pl.BlockSpec((hist, tb, ch), lambda i: (0, i, 0)),
            pl.BlockSpec((tb, ch), lambda i: (i, 0)),
            pl.BlockSpec((width, ch), const),
            pl.BlockSpec((1, ch), const),
            pl.BlockSpec((1, ch), const),
            pl.BlockSpec((1, ch), const),
        ],
        out_specs=[pl.BlockSpec((tb, ch), lambda i: (i, 0)), pl.BlockSpec((hist, tb, ch), lambda i: (0, i, 0))],
        out_shape=[jax.ShapeDtypeStruct((nb, ch), BF16), jax.ShapeDtypeStruct((hist, nb, ch), F32)],
        compiler_params=_cparams(("arbitrary",)),
        name="conv_sample",
    )(state_t, u, conv_w, conv_b, ln_g, ln_b)


def _attn_sample_kernel(pt_ref, qs_ref, cn_ref, krn_ref, cache_c, cache_krt, o_ref, cbuf, rbuf, sem, m_ref, l_ref,
                        acc_ref, *, pages, n_chunks, page, kv_lora, sub, scale):
    b = pl.program_id(0)
    ch = pl.program_id(1)
    g = b * n_chunks + ch
    n_steps = pl.num_programs(0) * n_chunks
    slot = g % 2

    def copies(step, sl):
        out = []
        for p in range(pages):
            pid = pt_ref[step * pages + p]
            keys = pl.ds(p * page, page)
            out.append(pltpu.make_async_copy(cache_c.at[pid], cbuf.at[sl, keys, :], sem.at[0, sl]))
            out.append(pltpu.make_async_copy(cache_krt.at[pid], rbuf.at[sl, :, keys], sem.at[1, sl]))
        return out

    @pl.when(g == 0)
    def _():
        for cp in copies(0, 0):
            cp.start()

    @pl.when(g + 1 < n_steps)
    def _():
        for cp in copies(g + 1, 1 - slot):
            cp.start()

    @pl.when(ch == 0)
    def _():
        m_ref[...] = jnp.full(m_ref.shape, NEG_INF, F32)
        l_ref[...] = jnp.zeros(l_ref.shape, F32)
        acc_ref[...] = jnp.zeros(acc_ref.shape, F32)

    pltpu.make_async_copy(cbuf.at[slot], cbuf.at[slot], sem.at[0, slot]).wait()
    pltpu.make_async_copy(rbuf.at[slot], rbuf.at[slot], sem.at[1, slot]).wait()

    q = qs_ref[0]
    ql = q[:, :kv_lora]
    qr = q[:, kv_lora:kv_lora + ROPE_DIM]
    cbs, scores = [], []
    for k0 in range(0, pages * page, sub):
        cb = cbuf[slot, k0:k0 + sub, :].astype(BF16)
        rb = rbuf[slot, :, k0:k0 + sub].astype(BF16)
        cbs.append(cb)
        scores.append((_nt_dot(ql, cb) + jnp.dot(qr, rb, preferred_element_type=F32)) * scale)
    probs = []
    for s in scores:
        m_k = jnp.max(s, axis=-1, keepdims=True)
        p = jnp.exp(s - m_k)
        probs.append((m_k, jnp.sum(p, axis=-1, keepdims=True), p.astype(BF16)))
    parts = [(m_k, l_k, jnp.dot(p, cb, preferred_element_type=F32)) for (m_k, l_k, p), cb in zip(probs, cbs)]
    m_old = m_ref[...]
    m_new = m_old
    for m_k, _, _ in parts:
        m_new = jnp.maximum(m_new, m_k)
    alpha = jnp.exp(m_old - m_new)
    l_new = l_ref[...] * alpha
    acc = acc_ref[...] * alpha
    for m_k, l_k, a_k in parts:
        w_k = jnp.exp(m_k - m_new)
        l_new = l_new + l_k * w_k
        acc = acc + a_k * w_k
    l_ref[...] = l_new
    acc_ref[...] = acc
    m_ref[...] = m_new

    @pl.when(ch == n_chunks - 1)
    def _():
        cn = cn_ref[0].astype(BF16).astype(F32)
        krn = krn_ref[0].astype(BF16).astype(F32)
        s_new = (jnp.sum(ql.astype(F32) * cn, axis=-1, keepdims=True)
                 + jnp.sum(qr.astype(F32) * krn, axis=-1, keepdims=True)) * scale
        m_o = m_ref[...]
        m_n = jnp.maximum(m_o, s_new)
        al = jnp.exp(m_o - m_n)
        p_new = jnp.exp(s_new - m_n)
        l_fin = l_ref[...] * al + p_new
        acc = acc_ref[...] * al + p_new.astype(BF16).astype(F32) * cn
        o_ref[0] = acc / l_fin


def _attn_sample(page_table, qs3, c_new3, kr_new3, cache_c, cache_krt, *, scale, pages=32, sub=512):
    nb, n_heads, qw = qs3.shape
    n_pages = page_table.shape[1]
    page, kv_lora = cache_c.shape[1:]
    rope = cache_krt.shape[1]
    n_chunks = n_pages // pages
    keys = pages * page
    kern = functools.partial(_attn_sample_kernel, pages=pages, n_chunks=n_chunks, page=page, kv_lora=kv_lora,
                             sub=sub, scale=scale)
    grid_spec = pltpu.PrefetchScalarGridSpec(
        num_scalar_prefetch=1, grid=(nb, n_chunks),
        in_specs=[
            pl.BlockSpec((1, n_heads, qw), lambda b, c, pt: (b, 0, 0)),
            pl.BlockSpec((1, 1, kv_lora), lambda b, c, pt: (b, 0, 0)),
            pl.BlockSpec((1, 1, rope), lambda b, c, pt: (b, 0, 0)),
            pl.BlockSpec(memory_space=pl.ANY),
            pl.BlockSpec(memory_space=pl.ANY),
        ],
        out_specs=pl.BlockSpec((1, n_heads, kv_lora), lambda b, c, pt: (b, 0, 0)),
        scratch_shapes=[
            pltpu.VMEM((2, keys, kv_lora), F32),
            pltpu.VMEM((2, rope, keys), F32),
            pltpu.SemaphoreType.DMA((2, 2)),
            pltpu.VMEM((n_heads, 1), F32),
            pltpu.VMEM((n_heads, 1), F32),
            pltpu.VMEM((n_heads, kv_lora), F32),
        ],
    )
    return pl.pallas_call(
        kern, grid_spec=grid_spec,
        out_shape=jax.ShapeDtypeStruct((nb, n_heads, kv_lora), F32),
        compiler_params=_cparams(("arbitrary", "arbitrary")),
        name="attn_sample",
    )(page_table.reshape(-1), qs3, c_new3, kr_new3, cache_c, cache_krt)


def _v_up_kernel(ol_ref, wkv_ref, o_ref, *, n_heads, kv_lora, v_dim):
    for hd in range(n_heads):
        ol = ol_ref[:, hd * kv_lora:(hd + 1) * kv_lora].astype(BF16)
        wv = wkv_ref[:, hd * HEAD_PAD + NOPE_DIM:(hd + 1) * HEAD_PAD]
        o_ref[:, hd * v_dim:(hd + 1) * v_dim] = jnp.dot(ol, wv, preferred_element_type=F32).astype(o_ref.dtype)


def _v_up(o_lat2, wkv_p, *, n_heads, kv_lora, v_dim):
    nb = o_lat2.shape[0]
    return pl.pallas_call(
        functools.partial(_v_up_kernel, n_heads=n_heads, kv_lora=kv_lora, v_dim=v_dim),
        out_shape=jax.ShapeDtypeStruct((nb, n_heads * v_dim), BF16),
        compiler_params=_cparams(None),
        name="v_up_sample",
    )(o_lat2, wkv_p)


def _pack_bf16_pairs(h):
    w = h.shape[1] // 2
    hi = pltpu.bitcast(h[:, :w].astype(F32), jnp.uint32)
    lo = pltpu.bitcast(h[:, w:].astype(F32), jnp.uint32)
    return hi | (lo >> 16)


def _route_rows(x1, g, w_r, bias, run_ref, *, n_exp, n_grp, per_grp):
    tm = x1.shape[0]
    h = _rms(x1, g).astype(BF16)
    logits = jnp.dot(h, w_r, preferred_element_type=F32) + bias
    lane_i = lax.broadcasted_iota(jnp.int32, (tm, LANES), 1)
    lane = lane_i.astype(F32)
    big = float(LANES)

    def first_max(vals):
        mx = jnp.max(vals, axis=-1, keepdims=True)
        idx = jnp.min(jnp.where(vals == mx, lane, big), axis=-1, keepdims=True)
        return mx, idx

    is_grp = (lane_i >= n_exp) & (lane_i < n_exp + n_grp)
    lg = jnp.where(is_grp, logits, NEG_INF)
    g_max, g_lane = first_max(lg)
    g_p = 1.0 / jnp.sum(jnp.where(is_grp, jnp.exp(lg - g_max), 0.0), axis=-1, keepdims=True)
    g_idx = g_lane - float(n_exp)
    lo = g_idx * float(per_grp)
    in_grp = (lane >= lo) & (lane < lo + float(per_grp))
    le = jnp.where(in_grp, logits, NEG_INF)
    m1, i1 = first_max(le)
    m2, i2 = first_max(jnp.where(lane == i1, NEG_INF, le))
    e2 = jnp.exp(m2 - m1)
    w1 = g_p / (1.0 + e2)
    w2 = g_p * e2 / (1.0 + e2)

    oh1 = lane == i1
    oh2 = lane == i2
    onehot = jnp.where(oh1 | oh2, 1.0, 0.0)
    r_i = lax.broadcasted_iota(jnp.int32, (tm, tm), 0)
    c_i = lax.broadcasted_iota(jnp.int32, (tm, tm), 1)
    lower = jnp.where(c_i < r_i, 1.0, 0.0).astype(BF16)
    before = run_ref[...] + jnp.dot(lower, onehot.astype(BF16), preferred_element_type=F32)
    rank1 = jnp.sum(jnp.where(oh1, before, 0.0), axis=-1, keepdims=True)
    rank2 = jnp.sum(jnp.where(oh2, before, 0.0), axis=-1, keepdims=True)
    run_ref[...] = run_ref[...] + jnp.sum(onehot, axis=0, keepdims=True)

    info = jnp.where(lane_i == 0, i1, jnp.where(lane_i == 1, i2, jnp.where(lane_i == 2, rank1, jnp.where(
        lane_i == 3, rank2, jnp.where(lane_i == 4, w1, jnp.where(lane_i == 5, w2, 0.0))))))
    return info, _pack_bf16_pairs(h)


def _out_proj_route_kernel(ap_ref, bp_ref, xp_ref, as_ref, bs_ref, xs_ref, w_ref, g_ref, wr_ref, bias_ref,
                           o_ref, info_ref, hp_ref, cnt_ref, run_ref, *, half, n_prompt_tiles, route):
    i = pl.program_id(0)

    @pl.when(i == 0)
    def _():
        run_ref[...] = jnp.zeros(run_ref.shape, F32)

    def tile(a_ref, b_ref, x_ref):
        n = a_ref.shape[0]
        x1 = (x_ref[...] + jnp.dot(a_ref[...], w_ref[0:half, :], preferred_element_type=F32)
              + jnp.dot(b_ref[...], w_ref[half:, :], preferred_element_type=F32))
        info, hp = _route_rows(x1, g_ref[...], wr_ref[...], bias_ref[...], run_ref, **route)
        o_ref[0:n, :] = x1
        info_ref[0:n, :] = info
        hp_ref[0:n, :] = hp

    @pl.when(i < n_prompt_tiles)
    def _():
        tile(ap_ref, bp_ref, xp_ref)

    @pl.when(i == n_prompt_tiles)
    def _():
        tile(as_ref, bs_ref, xs_ref)

    cnt_ref[...] = run_ref[...]


def _out_proj_route(a_p, b_p, x_p, a_s, b_s, x_s, w, ffn_g, w_r, bias, *, tm, n_exp, n_grp, per_grp):
    n_p, half = a_p.shape
    n_s = a_s.shape[0]
    d = w.shape[1]
    n = n_p + n_s
    npt = n_p // tm
    assert n_s <= tm
    const = lambda i: (0, 0)
    prow = lambda i: (jnp.minimum(i, npt - 1), 0)
    row = lambda i: (i, 0)
    kern = functools.partial(_out_proj_route_kernel, half=half, n_prompt_tiles=npt,
                             route=dict(n_exp=n_exp, n_grp=n_grp, per_grp=per_grp))
    return pl.pallas_call(
        kern, grid=(npt + 1,),
        in_specs=[
            pl.BlockSpec((tm, half), prow),
            pl.BlockSpec((tm, half), prow),
            pl.BlockSpec((tm, d), prow),
            pl.BlockSpec((n_s, half), const),
            pl.BlockSpec((n_s, half), const),
            pl.BlockSpec((n_s, d), const),
            pl.BlockSpec(w.shape, const, pipeline_mode=pl.Buffered(1)),
            pl.BlockSpec((1, d), const),
            pl.BlockSpec((d, LANES), const),
            pl.BlockSpec((1, LANES), const),
        ],
        out_specs=[pl.BlockSpec((tm, d), row), pl.BlockSpec((tm, LANES), row), pl.BlockSpec((tm, d // 2), row),
                   pl.BlockSpec((1, LANES), const)],
        out_shape=[jax.ShapeDtypeStruct((n, d), F32), jax.ShapeDtypeStruct((n, LANES), F32),
                   jax.ShapeDtypeStruct((n, d // 2), jnp.uint32), jax.ShapeDtypeStruct((1, LANES), F32)],
        scratch_shapes=[pltpu.VMEM((1, LANES), F32)],
        compiler_params=_cparams(("arbitrary",)),
        name="out_proj_route",
    )(a_p, b_p, x_p, a_s, b_s, x_s, w, ffn_g, w_r, bias)


def _pow2_pieces(limit):
    p = limit // 2
    while p >= 1:
        yield p
        p //= 2


def _dispatch_kernel(dest_ref, zrow_ref, zcnt_ref, pend_ref, hp_ref, xs_ref, hbuf, zbuf, sem, zsem, *, tm, sub):
    i = pl.program_id(0)
    n = pl.num_programs(0)
    slot = i % 2
    n_exp = zrow_ref.shape[0]
    n_sub_blocks = xs_ref.shape[0] // sub

    def row_copy(sl, r, d):
        return pltpu.make_async_copy(hbuf.at[sl, pl.ds(r, 1), :], xs_ref.at[pl.ds(d, 1), :], sem.at[sl])

    def wait_all(sl):
        for _ in range(TOP_K):
            pltpu.make_async_copy(hbuf.at[sl], xs_ref.at[pl.ds(0, tm), :], sem.at[sl]).wait()

    def zero_copy(row, size):
        return pltpu.make_async_copy(zbuf.at[pl.ds(0, size), :], xs_ref.at[pl.ds(row, size), :], zsem)

    def zero_fill(start):
        def per_expert(e, c):
            row = zrow_ref[e]
            cnt = zcnt_ref[e]
            head = (-row) & (SUBLANES - 1)
            for r in range(SUBLANES - 1):
                @pl.when(r < jnp.minimum(head, cnt))
                def _():
                    cp = zero_copy(row + r, 1)
                    cp.start() if start else cp.wait()
            rest = jnp.maximum(cnt - head, 0)
            row = pl.multiple_of(row + head, SUBLANES)
            for piece in _pow2_pieces(sub):
                if piece < SUBLANES:
                    break
                @pl.when((rest & piece) != 0)
                def _():
                    cp = zero_copy(row, piece)
                    cp.start() if start else cp.wait()
                row = pl.multiple_of(row + (rest & piece), SUBLANES)
            return c
        lax.fori_loop(0, n_exp, per_expert, 0)

        def per_block(b, c):
            @pl.when(b * sub >= pend_ref[0])
            def _():
                cp = zero_copy(b * sub, sub)
                cp.start() if start else cp.wait()
            return c
        lax.fori_loop(0, n_sub_blocks, per_block, 0)

    @pl.when(i == 0)
    def _():
        zbuf[...] = jnp.zeros(zbuf.shape, zbuf.dtype)
        zero_fill(True)

    hbuf[slot] = hp_ref[...]

    def issue(r, c):
        a = (i * tm + r) * TOP_K
        for k in range(TOP_K):
            row_copy(slot, r, dest_ref[a + k]).start(priority=k % 2)
        return c
    lax.fori_loop(0, tm, issue, 0, unroll=8)

    @pl.when(i == 0)
    def _():
        zero_fill(False)

    @pl.when(i > 0)
    def _():
        wait_all(1 - slot)

    @pl.when(i == n - 1)
    def _():
        wait_all(slot)


def _dispatch(dest, zrow, zcnt, pend, hp, *, tm, rows, sub):
    n, dw = hp.shape
    grid_spec = pltpu.PrefetchScalarGridSpec(
        num_scalar_prefetch=4, grid=(n // tm,),
        in_specs=[pl.BlockSpec((tm, dw), lambda i, *_: (i, 0))],
        out_specs=pl.BlockSpec(memory_space=pl.ANY),
        scratch_shapes=[pltpu.VMEM((2, tm, dw), hp.dtype), pltpu.VMEM((sub, dw), hp.dtype),
                        pltpu.SemaphoreType.DMA((2,)), pltpu.SemaphoreType.DMA(())],
    )
    return pl.pallas_call(
        functools.partial(_dispatch_kernel, tm=tm, sub=sub), grid_spec=grid_spec,
        out_shape=jax.ShapeDtypeStruct((rows, dw), hp.dtype),
        compiler_params=_cparams(("arbitrary",)),
        name="moe_dispatch",
    )(dest, zrow, zcnt, pend, hp)


def _experts_kernel(ie_ref, ir_ref, ins_ref, nit_ref, pend_ref, xs_ref, wg_hbm, wu_hbm, wd_hbm, ys_ref,
                    gu_ring, d_ring, xbuf, gacc, uacc, hbuf, ybuf, zbuf,
                    gu_sem, d_sem, x_sem, y_sem, z_sem, *, sub, n_sub_max, kc, fcs):
    n_items = nit_ref[0]
    d = ys_ref.shape[1]
    n_kc = d // kc
    n_wc = n_kc // 2
    n_fc = len(fcs)
    assert n_fc == 2 and n_kc >= 2 and n_kc % 2 == 0
    item_rows = sub * n_sub_max
    f_offs = [sum(fcs[:j]) for j in range(n_fc)]

    def gu_copies(item, pos, slot):
        e = ie_ref[item]
        rows = pl.ds(pos * kc, kc)
        return (pltpu.make_async_copy(wg_hbm.at[e, rows, :], gu_ring.at[slot, 0], gu_sem.at[slot]),
                pltpu.make_async_copy(wu_hbm.at[e, rows, :], gu_ring.at[slot, 1], gu_sem.at[slot]))

    def d_copy(item, j):
        e = ie_ref[item]
        return pltpu.make_async_copy(wd_hbm.at[e, pl.ds(f_offs[j], fcs[j]), :], d_ring.at[j, pl.ds(0, fcs[j]), :],
                                     d_sem.at[j])

    def gu_start(item, pos, slot):
        e = ie_ref[item]
        for m, w_hbm in enumerate((wg_hbm, wu_hbm)):
            for r0 in range(0, kc, sub):
                pltpu.make_async_copy(w_hbm.at[e, pl.ds(pos * kc + r0, sub), :],
                                      gu_ring.at[slot, m, pl.ds(r0, sub), :], gu_sem.at[slot]).start()

    def d_start(item, j):
        e = ie_ref[item]
        for r0 in range(0, fcs[j], sub):
            pltpu.make_async_copy(wd_hbm.at[e, pl.ds(f_offs[j] + r0, sub), :], d_ring.at[j, pl.ds(r0, sub), :],
                                  d_sem.at[j]).start()

    def x_xfer(item, slot, start):
        n_sub = ins_ref[item]
        for sb in range(n_sub_max):
            @pl.when(sb < n_sub)
            def _():
                r = pl.multiple_of(ir_ref[item] + sb * sub, sub)
                for k in range(n_wc):
                    cp = pltpu.make_async_copy(xs_ref.at[pl.ds(r, sub), pl.ds(k * kc, kc)],
                                               xbuf.at[slot, k, pl.ds(sb * sub, sub), :], x_sem.at[slot])
                    cp.start() if start else cp.wait()

    def y_copy(item, sb):
        r = pl.multiple_of(ir_ref[item] + sb * sub, sub)
        return pltpu.make_async_copy(ybuf.at[pl.ds(sb * sub, sub), :], ys_ref.at[pl.ds(r, sub), :], y_sem)

    def for_rows(n_sub, fn):
        for ns in range(1, n_sub_max + 1):
            @pl.when(n_sub == ns)
            def _():
                fn(ns * sub)

    def y_writes(item, start):
        n_sub = ins_ref[item]
        for sb in range(n_sub_max):
            @pl.when(sb < n_sub)
            def _():
                cp = y_copy(item, sb)
                cp.start() if start else cp.wait()

    zbuf[...] = jnp.zeros(zbuf.shape, F32)
    n_out_blocks = ys_ref.shape[0] // sub

    def tail(start):
        def body(b, c):
            @pl.when(b * sub >= pend_ref[0])
            def _():
                cp = pltpu.make_async_copy(zbuf, ys_ref.at[pl.ds(b * sub, sub), :], z_sem)
                cp.start() if start else cp.wait()
            return c
        lax.fori_loop(0, n_out_blocks, body, 0)
    tail(True)

    @pl.when(n_items > 0)
    def _():
        x_xfer(0, 0, True)
        for pos in range(2):
            gu_start(0, pos, pos)

    def item_body(item, carry):
        xslot = item % 2
        n_sub = ins_ref[item]
        has_next = item + 1 < n_items

        @pl.when(has_next)
        def _():
            x_xfer(item + 1, 1 - xslot, True)

        x_xfer(item, xslot, False)

        def zero_acc(rows):
            gacc[0:rows, :] = jnp.zeros((rows, gacc.shape[1]), F32)
            uacc[0:rows, :] = jnp.zeros((rows, uacc.shape[1]), F32)
        for_rows(n_sub, zero_acc)

        def gu_step(pos, c):
            slot = pos % 2
            for cp in gu_copies(item, pos, slot):
                cp.wait()

            def mm(rows):
                xw = xbuf[xslot, pos % n_wc, 0:rows, :]
                shift = jnp.asarray((pos // n_wc) * 16, jnp.uint32)
                xk = pltpu.bitcast((xw << shift) & jnp.uint32(0xFFFF0000), F32).astype(BF16)
                gacc[0:rows, :] += jnp.dot(xk, gu_ring[slot, 0].astype(BF16), preferred_element_type=F32)
                uacc[0:rows, :] += jnp.dot(xk, gu_ring[slot, 1].astype(BF16), preferred_element_type=F32)
            for_rows(n_sub, mm)

            @pl.when(pos + 2 < n_kc)
            def _():
                gu_start(item, pos + 2, slot)
            for j in range(n_fc):
                @pl.when(pos + 2 == n_kc + j)
                def _():
                    d_start(item, j)
            return c
        lax.fori_loop(0, n_kc, gu_step, 0)

        def act(rows):
            g = gacc[0:rows, :]
            hbuf[0:rows, :] = (g * _sigmoid(g) * uacc[0:rows, :]).astype(BF16)
        for_rows(n_sub, act)

        @pl.when(item > 0)
        def _():
            y_writes(item - 1, False)

        for j in range(n_fc):
            d_copy(item, j).wait()

            def mm_down(rows, j=j):
                y = jnp.dot(hbuf[0:rows, f_offs[j]:f_offs[j] + fcs[j]], d_ring[j, 0:fcs[j], :].astype(BF16),
                            preferred_element_type=F32)
                if j == 0:
                    ybuf[0:rows, :] = y
                else:
                    ybuf[0:rows, :] += y
            for_rows(n_sub, mm_down)

            @pl.when(has_next)
            def _():
                gu_start(item + 1, j, j)
        y_writes(item, True)
        return carry

    lax.fori_loop(0, n_items, item_body, 0)

    @pl.when(n_items > 0)
    def _():
        y_writes(n_items - 1, False)
    tail(False)


def _experts(item_e, item_row, item_nsub, n_items, pend, xs, w_gate, w_up, w_down, *, sub, n_sub_max):
    rows = xs.shape[0]
    d, f = w_gate.shape[1:]
    kc = 512
    lanes_f = f // LANES
    fcs = ((lanes_f + 1) // 2 * LANES, lanes_f // 2 * LANES)
    item_rows = sub * n_sub_max
    any_spec = pl.BlockSpec(memory_space=pl.ANY)
    kern = functools.partial(_experts_kernel, sub=sub, n_sub_max=n_sub_max, kc=kc, fcs=fcs)
    return pl.pallas_call(
        kern,
        grid_spec=pltpu.PrefetchScalarGridSpec(
            num_scalar_prefetch=5, grid=(1,),
            in_specs=[any_spec, any_spec, any_spec, any_spec],
            out_specs=any_spec,
            scratch_shapes=[
                pltpu.VMEM((2, 2, kc, f), F32),
                pltpu.VMEM((2, fcs[0], d), F32),
                pltpu.VMEM((2, d // kc // 2, item_rows, kc), xs.dtype),
                pltpu.VMEM((item_rows, f), F32),
                pltpu.VMEM((item_rows, f), F32),
                pltpu.VMEM((item_rows, f), BF16),
                pltpu.VMEM((item_rows, d), F32),
                pltpu.VMEM((sub, d), F32),
                pltpu.SemaphoreType.DMA((2,)),
                pltpu.SemaphoreType.DMA((2,)),
                pltpu.SemaphoreType.DMA((2,)),
                pltpu.SemaphoreType.DMA(()),
                pltpu.SemaphoreType.DMA(()),
            ],
        ),
        out_shape=jax.ShapeDtypeStruct((rows, d), F32),
        compiler_params=_cparams(("arbitrary",), 60 * 1024 * 1024),
        name="moe_experts",
    )(item_e, item_row, item_nsub, n_items, pend, xs, w_gate, w_up, w_down)


def _combine_kernel(dest_ref, x_ref, info_ref, g_ref, ys_ref, yp_ref, ys_out_ref, rbuf, sem, *, tm, n_prompt_tiles):
    i = pl.program_id(0)
    n = pl.num_programs(0)
    slot = i % 2

    def row_copy(sl, k, r, d):
        return pltpu.make_async_copy(ys_ref.at[pl.ds(d, 1), :], rbuf.at[sl, k, pl.ds(r, 1), :], sem.at[sl])

    def issue(step, sl):
        def body(r, c):
            a = (step * tm + r) * TOP_K
            for k in range(TOP_K):
                row_copy(sl, k, r, dest_ref[a + k]).start(priority=k % 2)
            return c
        lax.fori_loop(0, tm, body, 0, unroll=8)

    @pl.when(i == 0)
    def _():
        issue(0, 0)

    @pl.when(i + 1 < n)
    def _():
        issue(i + 1, 1 - slot)

    for k in range(TOP_K):
        pltpu.make_async_copy(ys_ref.at[pl.ds(0, tm), :], rbuf.at[slot, k], sem.at[slot]).wait()

    info = info_ref[...]
    lane_i = lax.broadcasted_iota(jnp.int32, info.shape, 1)
    w1 = jnp.sum(jnp.where(lane_i == 4, info, 0.0), axis=-1, keepdims=True)
    w2 = jnp.sum(jnp.where(lane_i == 5, info, 0.0), axis=-1, keepdims=True)
    y = x_ref[...] + (rbuf[slot, 0] * w1 + rbuf[slot, 1] * w2)
    out = _rms(y, g_ref[...])

    @pl.when(i < n_prompt_tiles)
    def _():
        yp_ref[...] = out

    @pl.when(i >= n_prompt_tiles)
    def _():
        ys_out_ref[...] = out


def _combine(dest, x1, info, final_g, ys, *, tm, n_prompt, n_sample):
    n, d = x1.shape
    npt = n_prompt // tm
    grid_spec = pltpu.PrefetchScalarGridSpec(
        num_scalar_prefetch=1, grid=(n // tm,),
        in_specs=[
            pl.BlockSpec((tm, d), lambda i, ds: (i, 0)),
            pl.BlockSpec((tm, LANES), lambda i, ds: (i, 0)),
            pl.BlockSpec((1, d), lambda i, ds: (0, 0)),
            pl.BlockSpec(memory_space=pl.ANY),
        ],
        out_specs=[
            pl.BlockSpec((tm, d), lambda i, ds: (jnp.minimum(i, npt - 1), 0)),
            pl.BlockSpec((tm, d), lambda i, ds: (jnp.maximum(i - npt, 0), 0)),
        ],
        scratch_shapes=[pltpu.VMEM((2, TOP_K, tm, d), F32), pltpu.SemaphoreType.DMA((2,))],
    )
    return pl.pallas_call(
        functools.partial(_combine_kernel, tm=tm, n_prompt_tiles=npt), grid_spec=grid_spec,
        out_shape=[jax.ShapeDtypeStruct((n_prompt, d), F32), jax.ShapeDtypeStruct((n_sample, d), F32)],
        compiler_params=_cparams(("arbitrary",)),
        name="moe_combine",
    )(dest, x1, info, final_g, ys)


def _rope_tables(pos):
    half = ROPE_DIM // 2
    inv_freq = ROPE_BASE ** (-jnp.arange(half, dtype=F32) / half)
    ang = pos.astype(F32)[:, None] * inv_freq[None, :]
    cos, sin = jnp.cos(ang), jnp.sin(ang)
    z = jnp.zeros_like(cos)
    z2 = jnp.concatenate([z, z], axis=1)
    return (jnp.concatenate([cos, cos, z2], axis=1), jnp.concatenate([-sin, z, z2], axis=1),
            jnp.concatenate([z, sin, z2], axis=1))


def kernel(x_prompt, x_sample, cache_kv_latent, cache_k_rope, state_conv, page_table, attn_norm_g, w_in, q_norm_g,
           w_q_up, kv_norm_g, w_kv_up, conv_w, conv_b, conv_ln_g, conv_ln_b, w_out, ffn_norm_g, w_router_group,
           b_router_group, w_router_expert, b_router_expert, w_exp_gate, w_exp_up, w_exp_down, final_norm_g):
    batch, seq, d = x_prompt.shape
    nb, t_dec, _ = x_sample.shape
    depth, q_lora, n_heads, qk_dim = w_q_up.shape
    kv_lora = w_kv_up.shape[1]
    v_dim = w_kv_up.shape[3] - NOPE_DIM
    conv_ch = conv_w.shape[2]
    width = conv_w.shape[1]
    n_exp = w_exp_gate.shape[1]
    n_grp = w_router_group.shape[2]
    per_grp = n_exp // n_grp
    page = cache_kv_latent.shape[2]
    past = page_table.shape[1] * page
    assert depth == 1 and t_dec == 1 and qk_dim == NOPE_DIM + ROPE_DIM and v_dim == NOPE_DIM
    scale = float(qk_dim) ** -0.5
    n_p = batch * seq
    n_tok = n_p + nb

    w_in_p = w_in[0].astype(BF16)
    wq_p = jnp.pad(w_q_up[0], ((0, 0), (0, 0), (0, HEAD_PAD - qk_dim))).reshape(q_lora, n_heads * HEAD_PAD)
    wq_p = wq_p.astype(BF16)
    wkv_p = w_kv_up[0].reshape(kv_lora, n_heads * HEAD_PAD).astype(BF16)
    w_out_b = w_out[0].astype(BF16)
    w_r = jnp.concatenate([w_router_expert[0], w_router_group[0],
                           jnp.zeros((d, LANES - n_exp - n_grp), F32)], axis=1).astype(BF16)
    b_r = jnp.concatenate([b_router_expert[0], b_router_group[0], jnp.zeros((LANES - n_exp - n_grp,), F32)])[None]
    row2 = lambda v: v.reshape(1, -1)

    tm_p = 512
    rc_p, ra_p, rb_p = _rope_tables(jnp.arange(seq))
    rc_s, ra_s, rb_s = _rope_tables(jnp.full((nb,), past, jnp.int32))
    dims = dict(n_heads=n_heads, q_lora=q_lora, kv_lora=kv_lora, conv_ch=conv_ch)
    xp2 = x_prompt.reshape(n_p, d)
    xs2 = x_sample.reshape(nb, d)
    q_p, c_p, kr_p, u_p, kv_p, krb_p = _in_proj(
        xp2, row2(attn_norm_g[0]), w_in_p, row2(q_norm_g[0]), wq_p, row2(kv_norm_g[0]), wkv_p, rc_p, ra_p, rb_p,
        tm=tm_p, prompt=True, **dims)
    qs_s, c_s, kr_s, u_s = _in_proj(
        xs2, row2(attn_norm_g[0]), w_in_p, row2(q_norm_g[0]), wq_p, row2(kv_norm_g[0]), wkv_p, rc_s, ra_s, rb_s,
        tm=nb, prompt=False, **dims)

    o_att_p = _attn_prompt(q_p, kv_p, krb_p, batch=batch, seq=seq, n_heads=n_heads, v_dim=v_dim, scale=scale)
    cw, cb, cg, cbeta = conv_w[0], row2(conv_b[0]), row2(conv_ln_g[0]), row2(conv_ln_b[0])
    o_conv_p = _conv_prompt(u_p, cw, cb, cg, cbeta, batch=batch, seq=seq)

    o_lat = _attn_sample(page_table, qs_s.reshape(nb, n_heads, QS_HEAD), c_s.reshape(nb, 1, kv_lora),
                         kr_s.reshape(nb, 1, ROPE_DIM), cache_kv_latent[0],
                         jnp.transpose(cache_k_rope[0], (0, 2, 1)), scale=scale)
    o_att_s = _v_up(o_lat.reshape(nb, n_heads * kv_lora), wkv_p, n_heads=n_heads, kv_lora=kv_lora, v_dim=v_dim)
    state_t = jnp.transpose(state_conv[0], (1, 0, 2))
    o_conv_s, new_state_t = _conv_sample(state_t, u_s, cw, cb, cg, cbeta)

    x1, info, hp, counts = _out_proj_route(o_att_p, o_conv_p, xp2, o_att_s, o_conv_s, xs2, w_out_b,
                                           row2(ffn_norm_g[0]), w_r, b_r, tm=512, n_exp=n_exp, n_grp=n_grp,
                                           per_grp=per_grp)
    i32 = jnp.int32
    sub, item_rows = MOE_SUB, MOE_SUB * MOE_ITEM_SUBS
    n_assign = n_tok * TOP_K
    rows_alloc = (-(-n_assign // sub) + n_exp) * sub
    max_items = n_assign // item_rows + n_exp + 1
    cnt = counts[0, :n_exp].astype(i32)
    padded = (cnt + sub - 1) // sub * sub
    pad_end = jnp.cumsum(padded)
    pad_start = pad_end - padded
    e_flat = info[:, 0:TOP_K].astype(i32).reshape(-1)
    rank = info[:, TOP_K:2 * TOP_K].astype(i32).reshape(-1)
    is_e = e_flat[:, None] == jnp.arange(n_exp, dtype=i32)[None, :]
    dest = rank + jnp.sum(jnp.where(is_e, pad_start[None, :], 0), axis=1)
    items_per_e = (padded + item_rows - 1) // item_rows
    it_end = jnp.cumsum(items_per_e)
    it_start = it_end - items_per_e
    w_ids = jnp.arange(max_items, dtype=i32)
    item_e = jnp.minimum(jnp.sum((it_end[None, :] <= w_ids[:, None]).astype(i32), axis=1), n_exp - 1)
    item_i = w_ids - it_start[item_e]
    item_row = jnp.clip(pad_start[item_e] + item_rows * item_i, 0, rows_alloc - item_rows).astype(i32)
    item_nsub = jnp.clip((padded[item_e] - item_rows * item_i) // sub, 1, MOE_ITEM_SUBS).astype(i32)
    n_items = it_end[-1:].astype(i32)
    pend = pad_end[-1:].astype(i32)

    xs = _dispatch(dest, (pad_start + cnt).astype(i32), (padded - cnt).astype(i32), pend, hp, tm=TOK_TILE,
                   rows=rows_alloc, sub=sub)
    ys = _experts(item_e, item_row, item_nsub, n_items, pend, xs, w_exp_gate[0], w_exp_up[0], w_exp_down[0],
                  sub=sub, n_sub_max=MOE_ITEM_SUBS)
    y_p, y_s = _combine(dest, x1, info, row2(final_norm_g), ys, tm=CMB_TILE, n_prompt=n_p, n_sample=nb)

    hist = width - 1
    new_conv_p = u_p.reshape(batch, seq, conv_ch)[:, seq - hist:]
    new_conv_s = jnp.transpose(new_state_t, (1, 0, 2))
    return (y_p.reshape(batch, seq, d), y_s.reshape(nb, t_dec, d),
            c_p.reshape(1, batch, seq, kv_lora), kr_p.reshape(1, batch, seq, ROPE_DIM), new_conv_p[None],
            c_s.reshape(1, nb, t_dec, kv_lora), kr_s.reshape(1, nb, t_dec, ROPE_DIM), new_conv_s[None])
```

```python
import functools

import jax
import jax.numpy as jnp
from jax import lax
from jax.experimental import pallas as pl
from jax.experimental.pallas import tpu as pltpu

F32 = jnp.float32
BF16 = jnp.bfloat16

EPS = 1e-6
NEG_INF = -1e30
ROPE_BASE = 10000.0
LOG2_E = 1.4426950408889634

LANES = 128
SUBLANES = 8
VMEM_LIMIT = 56 * 1024 * 1024

NOPE_DIM = 128
ROPE_DIM = 64
TOP_K = 2
HEAD_PAD = 256
QS_HEAD = 640

MOE_SUB = 128
MOE_ITEM_SUBS = 4
GU_SLOTS = 3
TOK_TILE = 416
CMB_TILE = 128


def _cparams(sem, vmem=VMEM_LIMIT):
    return pltpu.CompilerParams(dimension_semantics=sem, vmem_limit_bytes=vmem)


def _rms(x, g):
    return x * lax.rsqrt(jnp.mean(x * x, axis=-1, keepdims=True) + EPS) * g


def _rope128(v, c, a, b):
    return v * c + pltpu.roll(v, 96, 1) * a + pltpu.roll(v, 32, 1) * b


def _sigmoid(x):
    return 1.0 / (1.0 + jnp.exp(-x))


def _nt_dot(a, b):
    return lax.dot_general(a, b, (((1,), (1,)), ((), ())), preferred_element_type=F32)


def _in_proj_kernel(*refs, n_heads, q_lora, kv_lora, conv_ch, prompt):
    (x_ref, g_ref, w_ref, qg_ref, wq_ref, kvg_ref, wkv_ref, rc_ref, ra_ref, rb_ref) = refs[:10]
    outs = refs[10:]
    if prompt:
        q_ref, c_ref, kr_ref, u_ref, kv_ref, krb_ref = outs
    else:
        q_ref, c_ref, kr_ref, u_ref = outs
    o_kv = q_lora
    o_kr = q_lora + kv_lora
    o_g = o_kr + conv_ch
    n_in = o_kr + ROPE_DIM + 2 * conv_ch

    h = _rms(x_ref[...], g_ref[...]).astype(BF16)
    rc, ra, rb = rc_ref[...], ra_ref[...], rb_ref[...]

    zq = jnp.dot(h, w_ref[:, 0:q_lora], preferred_element_type=F32)
    qn = _rms(zq, qg_ref[...]).astype(BF16)
    qf = jnp.dot(qn, wq_ref[...], preferred_element_type=F32)

    zkv = jnp.dot(h, w_ref[:, o_kv:o_kr], preferred_element_type=F32)
    c = _rms(zkv, kvg_ref[...])
    c_ref[...] = c

    za = jnp.dot(h, w_ref[:, o_kr:o_g + LANES], preferred_element_type=F32)
    zg = jnp.dot(h, w_ref[:, o_g:n_in], preferred_element_type=F32)
    kr = _rope128(za[:, 0:LANES], rc, ra, rb)
    kr_ref[...] = kr[:, :ROPE_DIM]
    glu = za[:, 0:ROPE_DIM + conv_ch] * _sigmoid(zg)
    u_ref[...] = glu[:, ROPE_DIM:]

    if prompt:
        krb_ref[...] = kr.astype(BF16)
        kv_ref[...] = jnp.dot(c.astype(BF16), wkv_ref[...], preferred_element_type=F32).astype(BF16)
        for hd in range(n_heads):
            o = hd * HEAD_PAD
            q_ref[:, o:o + NOPE_DIM] = qf[:, o:o + NOPE_DIM].astype(BF16)
            q_ref[:, o + NOPE_DIM:o + HEAD_PAD] = _rope128(
                qf[:, o + NOPE_DIM:o + HEAD_PAD], rc, ra, rb).astype(BF16)
    else:
        for hd in range(n_heads):
            o = hd * HEAD_PAD
            qnope = qf[:, o:o + NOPE_DIM].astype(BF16)
            wkn = wkv_ref[:, o:o + NOPE_DIM]
            qo = hd * QS_HEAD
            q_ref[:, qo:qo + kv_lora] = _nt_dot(qnope, wkn).astype(BF16)
            q_ref[:, qo + kv_lora:qo + QS_HEAD] = _rope128(
                qf[:, o + NOPE_DIM:o + HEAD_PAD], rc, ra, rb).astype(BF16)


def _in_proj(x, attn_g, w_in_p, q_g, wq_p, kv_g, wkv_p, rc, ra, rb, *, tm, prompt, n_heads, q_lora, kv_lora,
             conv_ch):
    n, d = x.shape
    n_tab = rc.shape[0] // tm
    const = lambda i: (0, 0)
    row = lambda i: (i, 0)
    tab = lambda i: (i % n_tab, 0)
    single = pl.Buffered(1)
    in_specs = [
        pl.BlockSpec((tm, d), row),
        pl.BlockSpec((1, d), const),
        pl.BlockSpec(w_in_p.shape, const, pipeline_mode=single),
        pl.BlockSpec((1, q_lora), const),
        pl.BlockSpec(wq_p.shape, const, pipeline_mode=single),
        pl.BlockSpec((1, kv_lora), const),
        pl.BlockSpec(wkv_p.shape, const, pipeline_mode=single),
        pl.BlockSpec((tm, LANES), tab),
        pl.BlockSpec((tm, LANES), tab),
        pl.BlockSpec((tm, LANES), tab),
    ]
    q_cols = n_heads * (HEAD_PAD if prompt else QS_HEAD)
    out_shape = [
        jax.ShapeDtypeStruct((n, q_cols), BF16),
        jax.ShapeDtypeStruct((n, kv_lora), F32),
        jax.ShapeDtypeStruct((n, ROPE_DIM), F32),
        jax.ShapeDtypeStruct((n, conv_ch), F32),
    ]
    out_specs = [
        pl.BlockSpec((tm, q_cols), row),
        pl.BlockSpec((tm, kv_lora), row),
        pl.BlockSpec((tm, ROPE_DIM), row),
        pl.BlockSpec((tm, conv_ch), row),
    ]
    if prompt:
        out_shape += [jax.ShapeDtypeStruct((n, wkv_p.shape[1]), BF16), jax.ShapeDtypeStruct((n, LANES), BF16)]
        out_specs += [pl.BlockSpec((tm, wkv_p.shape[1]), row), pl.BlockSpec((tm, LANES), row)]
    kern = functools.partial(_in_proj_kernel, n_heads=n_heads, q_lora=q_lora, kv_lora=kv_lora, conv_ch=conv_ch,
                             prompt=prompt)
    return pl.pallas_call(
        kern, grid=(n // tm,), in_specs=in_specs, out_specs=out_specs, out_shape=out_shape,
        compiler_params=_cparams(("arbitrary",)),
        name="in_proj_prompt" if prompt else "in_proj_sample",
    )(x, attn_g, w_in_p, q_g, wq_p, kv_g, wkv_p, rc, ra, rb)


def _attn_prompt_kernel(q_ref, kv_ref, krb_ref, o_ref, kcat_ref, *, tq, tk, scale):
    i = pl.program_id(2)

    @pl.when(i == 0)
    def _():
        kcat_ref[:, :NOPE_DIM] = kv_ref[:, :NOPE_DIM]
        kcat_ref[:, NOPE_DIM:] = krb_ref[...]

    q = q_ref[...]
    v_dim = o_ref.shape[1]

    c = scale * LOG2_E

    def scores(j):
        k = kcat_ref[pl.ds(pl.multiple_of(j * tk, tk), tk), :]
        return _nt_dot(q, k)

    def update(j, s, carry, kw=tk):
        m, l, acc = carry
        m_new = jnp.maximum(m, jnp.max(s, axis=-1, keepdims=True))
        alpha = jnp.exp2((m - m_new) * c)
        p = jnp.exp2((s - m_new) * c)
        v = kv_ref[pl.ds(pl.multiple_of(j * tk, tk), kw), NOPE_DIM:]
        acc = acc * alpha + jnp.dot(p.astype(BF16), v, preferred_element_type=F32)
        return m_new, l * alpha + jnp.sum(p, axis=-1, keepdims=True), acc

    def body(j, carry):
        s_next = scores(j + 1)
        return update(j, carry[3], carry[:3]) + (s_next,)

    init = (jnp.full((tq, 1), NEG_INF, F32), jnp.zeros((tq, 1), F32), jnp.zeros((tq, v_dim), F32), scores(0))
    m, l, acc, s = lax.fori_loop(0, i, body, init)

    def finish(r0, r1, kw):
        rows = slice(r0, r1)
        rowi = lax.broadcasted_iota(jnp.int32, (r1 - r0, kw), 0) + r0
        coli = lax.broadcasted_iota(jnp.int32, (r1 - r0, kw), 1)
        s_blk = jnp.where(coli <= rowi, s[rows, 0:kw], NEG_INF)
        _, l_f, acc_f = update(i, s_blk, (m[rows], l[rows], acc[rows]), kw)
        o_ref[rows, :] = (acc_f / l_f).astype(o_ref.dtype)
    finish(0, tq // 2, tk // 2)
    finish(tq // 2, tq, tk)


def _attn_prompt(q, kv, krb, *, batch, seq, n_heads, v_dim, scale, tq=512):
    n = q.shape[0]
    nq = seq // tq
    kern = functools.partial(_attn_prompt_kernel, tq=tq, tk=tq, scale=scale)
    return pl.pallas_call(
        kern, grid=(batch, n_heads, nq),
        in_specs=[
            pl.BlockSpec((tq, HEAD_PAD), lambda b, h, i: (b * nq + i, h)),
            pl.BlockSpec((seq, HEAD_PAD), lambda b, h, i: (b, h)),
            pl.BlockSpec((seq, LANES), lambda b, h, i: (b, 0)),
        ],
        out_specs=pl.BlockSpec((tq, v_dim), lambda b, h, i: (b * nq + i, h)),
        out_shape=jax.ShapeDtypeStruct((n, n_heads * v_dim), BF16),
        scratch_shapes=[pltpu.VMEM((seq, HEAD_PAD), BF16)],
        compiler_params=_cparams(("arbitrary", "arbitrary", "arbitrary")),
        name="attn_prompt",
    )(q, kv, krb)


def _ln_silu(y, g, b):
    yc = y - jnp.mean(y, axis=-1, keepdims=True)
    z = yc * lax.rsqrt(jnp.mean(yc * yc, axis=-1, keepdims=True) + EPS) * g + b
    return z * _sigmoid(z)


def _conv_prompt_kernel(halo_ref, u_ref, w_ref, b_ref, g_ref, beta_ref, o_ref, win_ref, y_ref, sh_ref, *, tt,
                        halo, width, rc):
    i = pl.program_id(1)

    @pl.when(i == 0)
    def _():
        win_ref[0:halo, :] = jnp.zeros((halo, win_ref.shape[1]), F32)

    @pl.when(i > 0)
    def _():
        win_ref[0:halo, :] = halo_ref[...]

    win_ref[halo:, :] = u_ref[...]
    base = halo - (width - 1)
    ch = win_ref.shape[1]
    n_sh = tt + halo - SUBLANES
    for lc in range(ch // LANES):
        ls = slice(lc * LANES, (lc + 1) * LANES)
        wl = w_ref[:, ls]
        for g in range(1, SUBLANES):
            sh_ref[g, 0:n_sh, :] = win_ref[g:g + n_sh, ls]

        def row_chunk(r, c, ls=ls, wl=wl):
            r0 = pl.multiple_of(r * rc, rc)
            acc = jnp.zeros((rc, LANES), F32)
            for k in range(width):
                g = (base + k) % SUBLANES
                s0 = r0 + ((base + k) - g)
                tap = win_ref[pl.ds(s0, rc), ls] if g == 0 else sh_ref[g, pl.ds(s0, rc), :]
                acc = acc + tap * wl[k:k + 1, :]
            y_ref[pl.ds(r0, rc), ls] = acc
            return c
        lax.fori_loop(0, tt // rc, row_chunk, 0)
    o_ref[...] = _ln_silu(y_ref[...] + b_ref[...], g_ref[...], beta_ref[...]).astype(o_ref.dtype)


def _conv_prompt(u, conv_w, conv_b, ln_g, ln_b, *, batch, seq, tt=256, halo=32, rc=128):
    n, ch = u.shape
    width = conv_w.shape[0]
    nt = seq // tt
    hb = tt // halo
    const = lambda b, i: (0, 0)
    kern = functools.partial(_conv_prompt_kernel, tt=tt, halo=halo, width=width, rc=rc)
    return pl.pallas_call(
        kern, grid=(batch, nt),
        in_specs=[
            pl.BlockSpec((halo, ch), lambda b, i: (jnp.maximum((b * nt + i) * hb - 1, 0), 0)),
            pl.BlockSpec((tt, ch), lambda b, i: (b * nt + i, 0)),
            pl.BlockSpec((width, ch), const),
            pl.BlockSpec((1, ch), const),
            pl.BlockSpec((1, ch), const),
            pl.BlockSpec((1, ch), const),
        ],
        out_specs=pl.BlockSpec((tt, ch), lambda b, i: (b * nt + i, 0)),
        out_shape=jax.ShapeDtypeStruct((n, ch), BF16),
        scratch_shapes=[pltpu.VMEM((halo + tt, ch), F32), pltpu.VMEM((tt, ch), F32),
                        pltpu.VMEM((SUBLANES, halo + tt, LANES), F32)],
        compiler_params=_cparams(("arbitrary", "arbitrary")),
        name="conv_prompt",
    )(u, u, conv_w, conv_b, ln_g, ln_b)


def _conv_sample_kernel(st_ref, u_ref, w_ref, b_ref, g_ref, beta_ref, o_ref, new_ref, *, width):
    w = w_ref[...]
    u = u_ref[...]
    y = u * w[width - 1:width, :] + b_ref[...]
    for k in range(width - 1):
        y = y + st_ref[k] * w[k:k + 1, :]
    o_ref[...] = _ln_silu(y, g_ref[...], beta_ref[...]).astype(o_ref.dtype)
    for k in range(width - 2):
        new_ref[k] = st_ref[k + 1]
    new_ref[width - 2] = u


def _conv_sample(state_t, u, conv_w, conv_b, ln_g, ln_b, *, tb=32):
    hist, nb, ch = state_t.shape
    width = conv_w.shape[0]
    const = lambda i: (0, 0)
    return pl.pallas_call(
        functools.partial(_conv_sample_kernel, width=width), grid=(nb // tb,),
        in_specs=[
            pl.BlockSpec((hist, tb, ch), lambda i: (0, i, 0)),
            pl.BlockSpec((tb, ch), lambda i: (i, 0)),
            pl.BlockSpec((width, ch), const),
            pl.BlockSpec((1, ch), const),
            pl.BlockSpec((1, ch), const),
            pl.BlockSpec((1, ch), const),
        ],
        out_specs=[pl.BlockSpec((tb, ch), lambda i: (i, 0)), pl.BlockSpec((hist, tb, ch), lambda i: (0, i, 0))],
        out_shape=[jax.ShapeDtypeStruct((nb, ch), BF16), jax.ShapeDtypeStruct((hist, nb, ch), F32)],
        compiler_params=_cparams(("arbitrary",)),
        name="conv_sample",
    )(state_t, u, conv_w, conv_b, ln_g, ln_b)


def _attn_sample_kernel(pt_ref, qs_ref, cn_ref, krn_ref, cache_c, cache_krt, o_ref, cbuf, rbuf, sem, m_ref, l_ref,
                        acc_ref, *, pages, n_chunks, page, kv_lora, sub, scale):
    b = pl.program_id(0)
    ch = pl.program_id(1)
    g = b * n_chunks + ch
    n_steps = pl.num_programs(0) * n_chunks
    slot = g % 2

    def copies(step, sl):
        out = []
        for p in range(pages):
            pid = pt_ref[step * pages + p]
            keys = pl.ds(p * page, page)
            out.append(pltpu.make_async_copy(cache_c.at[pid], cbuf.at[sl, keys, :], sem.at[0, sl]))
            out.append(pltpu.make_async_copy(cache_krt.at[pid], rbuf.at[sl, :, keys], sem.at[1, sl]))
        return out

    @pl.when(g == 0)
    def _():
        for cp in copies(0, 0):
            cp.start()

    @pl.when(g + 1 < n_steps)
    def _():
        for cp in copies(g + 1, 1 - slot):
            cp.start()

    @pl.when(ch == 0)
    def _():
        m_ref[...] = jnp.full(m_ref.shape, NEG_INF, F32)
        l_ref[...] = jnp.zeros(l_ref.shape, F32)
        acc_ref[...] = jnp.zeros(acc_ref.shape, F32)

    pltpu.make_async_copy(cbuf.at[slot], cbuf.at[slot], sem.at[0, slot]).wait()
    pltpu.make_async_copy(rbuf.at[slot], rbuf.at[slot], sem.at[1, slot]).wait()

    q = qs_ref[0]
    ql = q[:, :kv_lora]
    qr = q[:, kv_lora:kv_lora + ROPE_DIM]
    cbs, scores = [], []
    for k0 in range(0, pages * page, sub):
        cb = cbuf[slot, k0:k0 + sub, :].astype(BF16)
        rb = rbuf[slot, :, k0:k0 + sub].astype(BF16)
        cbs.append(cb)
        scores.append((_nt_dot(ql, cb) + jnp.dot(qr, rb, preferred_element_type=F32)) * scale)
    probs = []
    for s in scores:
        m_k = jnp.max(s, axis=-1, keepdims=True)
        p = jnp.exp(s - m_k)
        probs.append((m_k, jnp.sum(p, axis=-1, keepdims=True), p.astype(BF16)))
    parts = [(m_k, l_k, jnp.dot(p, cb, preferred_element_type=F32)) for (m_k, l_k, p), cb in zip(probs, cbs)]
    m_old = m_ref[...]
    m_new = m_old
    for m_k, _, _ in parts:
        m_new = jnp.maximum(m_new, m_k)
    alpha = jnp.exp(m_old - m_new)
    l_new = l_ref[...] * alpha
    acc = acc_ref[...] * alpha
    for m_k, l_k, a_k in parts:
        w_k = jnp.exp(m_k - m_new)
        l_new = l_new + l_k * w_k
        acc = acc + a_k * w_k
    l_ref[...] = l_new
    acc_ref[...] = acc
    m_ref[...] = m_new

    @pl.when(ch == n_chunks - 1)
    def _():
        cn = cn_ref[0].astype(BF16).astype(F32)
        krn = krn_ref[0].astype(BF16).astype(F32)
        s_new = (jnp.sum(ql.astype(F32) * cn, axis=-1, keepdims=True)
                 + jnp.sum(qr.astype(F32) * krn, axis=-1, keepdims=True)) * scale
        m_o = m_ref[...]
        m_n = jnp.maximum(m_o, s_new)
        al = jnp.exp(m_o - m_n)
        p_new = jnp.exp(s_new - m_n)
        l_fin = l_ref[...] * al + p_new
        acc = acc_ref[...] * al + p_new.astype(BF16).astype(F32) * cn
        o_ref[0] = acc / l_fin


def _attn_sample(page_table, qs3, c_new3, kr_new3, cache_c, cache_krt, *, scale, pages=32, sub=512):
    nb, n_heads, qw = qs3.shape
    n_pages = page_table.shape[1]
    page, kv_lora = cache_c.shape[1:]
    rope = cache_krt.shape[1]
    n_chunks = n_pages // pages
    keys = pages * page
    kern = functools.partial(_attn_sample_kernel, pages=pages, n_chunks=n_chunks, page=page, kv_lora=kv_lora,
                             sub=sub, scale=scale)
    grid_spec = pltpu.PrefetchScalarGridSpec(
        num_scalar_prefetch=1, grid=(nb, n_chunks),
        in_specs=[
            pl.BlockSpec((1, n_heads, qw), lambda b, c, pt: (b, 0, 0)),
            pl.BlockSpec((1, 1, kv_lora), lambda b, c, pt: (b, 0, 0)),
            pl.BlockSpec((1, 1, rope), lambda b, c, pt: (b, 0, 0)),
            pl.BlockSpec(memory_space=pl.ANY),
            pl.BlockSpec(memory_space=pl.ANY),
        ],
        out_specs=pl.BlockSpec((1, n_heads, kv_lora), lambda b, c, pt: (b, 0, 0)),
        scratch_shapes=[
            pltpu.VMEM((2, keys, kv_lora), F32),
            pltpu.VMEM((2, rope, keys), F32),
            pltpu.SemaphoreType.DMA((2, 2)),
            pltpu.VMEM((n_heads, 1), F32),
            pltpu.VMEM((n_heads, 1), F32),
            pltpu.VMEM((n_heads, kv_lora), F32),
        ],
    )
    return pl.pallas_call(
        kern, grid_spec=grid_spec,
        out_shape=jax.ShapeDtypeStruct((nb, n_heads, kv_lora), F32),
        compiler_params=_cparams(("arbitrary", "arbitrary")),
        name="attn_sample",
    )(page_table.reshape(-1), qs3, c_new3, kr_new3, cache_c, cache_krt)


def _v_up_kernel(ol_ref, wkv_ref, o_ref, *, n_heads, kv_lora, v_dim):
    for hd in range(n_heads):
        ol = ol_ref[:, hd * kv_lora:(hd + 1) * kv_lora].astype(BF16)
        wv = wkv_ref[:, hd * HEAD_PAD + NOPE_DIM:(hd + 1) * HEAD_PAD]
        o_ref[:, hd * v_dim:(hd + 1) * v_dim] = jnp.dot(ol, wv, preferred_element_type=F32).astype(o_ref.dtype)


def _v_up(o_lat2, wkv_p, *, n_heads, kv_lora, v_dim):
    nb = o_lat2.shape[0]
    return pl.pallas_call(
        functools.partial(_v_up_kernel, n_heads=n_heads, kv_lora=kv_lora, v_dim=v_dim),
        out_shape=jax.ShapeDtypeStruct((nb, n_heads * v_dim), BF16),
        compiler_params=_cparams(None),
        name="v_up_sample",
    )(o_lat2, wkv_p)


def _pack_bf16_pairs(h):
    w = h.shape[1] // 2
    hi = pltpu.bitcast(h[:, :w].astype(F32), jnp.uint32)
    lo = pltpu.bitcast(h[:, w:].astype(F32), jnp.uint32)
    return hi | (lo >> 16)


def _route_rows(x1, g, w_r, bias, run_ref, *, n_exp, n_grp, per_grp):
    tm = x1.shape[0]
    h = _rms(x1, g).astype(BF16)
    logits = jnp.dot(h, w_r, preferred_element_type=F32) + bias
    lane_i = lax.broadcasted_iota(jnp.int32, (tm, LANES), 1)
    lane = lane_i.astype(F32)
    big = float(LANES)

    def first_max(vals):
        mx = jnp.max(vals, axis=-1, keepdims=True)
        idx = jnp.min(jnp.where(vals == mx, lane, big), axis=-1, keepdims=True)
        return mx, idx

    is_grp = (lane_i >= n_exp) & (lane_i < n_exp + n_grp)
    lg = jnp.where(is_grp, logits, NEG_INF)
    g_max, g_lane = first_max(lg)
    g_p = 1.0 / jnp.sum(jnp.where(is_grp, jnp.exp(lg - g_max), 0.0), axis=-1, keepdims=True)
    g_idx = g_lane - float(n_exp)
    lo = g_idx * float(per_grp)
    in_grp = (lane >= lo) & (lane < lo + float(per_grp))
    le = jnp.where(in_grp, logits, NEG_INF)
    m1, i1 = first_max(le)
    m2, i2 = first_max(jnp.where(lane == i1, NEG_INF, le))
    e2 = jnp.exp(m2 - m1)
    w1 = g_p / (1.0 + e2)
    w2 = g_p * e2 / (1.0 + e2)

    oh1 = lane == i1
    oh2 = lane == i2
    onehot = jnp.where(oh1 | oh2, 1.0, 0.0)
    r_i = lax.broadcasted_iota(jnp.int32, (tm, tm), 0)
    c_i = lax.broadcasted_iota(jnp.int32, (tm, tm), 1)
    lower = jnp.where(c_i < r_i, 1.0, 0.0).astype(BF16)
    before = run_ref[...] + jnp.dot(lower, onehot.astype(BF16), preferred_element_type=F32)
    rank1 = jnp.sum(jnp.where(oh1, before, 0.0), axis=-1, keepdims=True)
    rank2 = jnp.sum(jnp.where(oh2, before, 0.0), axis=-1, keepdims=True)
    run_ref[...] = run_ref[...] + jnp.sum(onehot, axis=0, keepdims=True)

    info = jnp.where(lane_i == 0, i1, jnp.where(lane_i == 1, i2, jnp.where(lane_i == 2, rank1, jnp.where(
        lane_i == 3, rank2, jnp.where(lane_i == 4, w1, jnp.where(lane_i == 5, w2, 0.0))))))
    return info, _pack_bf16_pairs(h)


def _out_proj_route_kernel(ap_ref, bp_ref, xp_ref, as_ref, bs_ref, xs_ref, w_ref, g_ref, wr_ref, bias_ref,
                           o_ref, info_ref, hp_ref, cnt_ref, run_ref, *, half, n_prompt_tiles, route):
    i = pl.program_id(0)

    @pl.when(i == 0)
    def _():
        run_ref[...] = jnp.zeros(run_ref.shape, F32)

    def tile(a_ref, b_ref, x_ref):
        n = a_ref.shape[0]
        x1 = (x_ref[...] + jnp.dot(a_ref[...], w_ref[0:half, :], preferred_element_type=F32)
              + jnp.dot(b_ref[...], w_ref[half:, :], preferred_element_type=F32))
        info, hp = _route_rows(x1, g_ref[...], wr_ref[...], bias_ref[...], run_ref, **route)
        o_ref[0:n, :] = x1
        info_ref[0:n, :] = info
        hp_ref[0:n, :] = hp

    @pl.when(i < n_prompt_tiles)
    def _():
        tile(ap_ref, bp_ref, xp_ref)

    @pl.when(i == n_prompt_tiles)
    def _():
        tile(as_ref, bs_ref, xs_ref)

    cnt_ref[...] = run_ref[...]


def _out_proj_route(a_p, b_p, x_p, a_s, b_s, x_s, w, ffn_g, w_r, bias, *, tm, n_exp, n_grp, per_grp):
    n_p, half = a_p.shape
    n_s = a_s.shape[0]
    d = w.shape[1]
    n = n_p + n_s
    npt = n_p // tm
    assert n_s <= tm
    const = lambda i: (0, 0)
    prow = lambda i: (jnp.minimum(i, npt - 1), 0)
    row = lambda i: (i, 0)
    kern = functools.partial(_out_proj_route_kernel, half=half, n_prompt_tiles=npt,
                             route=dict(n_exp=n_exp, n_grp=n_grp, per_grp=per_grp))
    return pl.pallas_call(
        kern, grid=(npt + 1,),
        in_specs=[
            pl.BlockSpec((tm, half), prow),
            pl.BlockSpec((tm, half), prow),
            pl.BlockSpec((tm, d), prow),
            pl.BlockSpec((n_s, half), const),
            pl.BlockSpec((n_s, half), const),
            pl.BlockSpec((n_s, d), const),
            pl.BlockSpec(w.shape, const, pipeline_mode=pl.Buffered(1)),
            pl.BlockSpec((1, d), const),
            pl.BlockSpec((d, LANES), const),
            pl.BlockSpec((1, LANES), const),
        ],
        out_specs=[pl.BlockSpec((tm, d), row), pl.BlockSpec((tm, LANES), row), pl.BlockSpec((tm, d // 2), row),
                   pl.BlockSpec((1, LANES), const)],
        out_shape=[jax.ShapeDtypeStruct((n, d), F32), jax.ShapeDtypeStruct((n, LANES), F32),
                   jax.ShapeDtypeStruct((n, d // 2), jnp.uint32), jax.ShapeDtypeStruct((1, LANES), F32)],
        scratch_shapes=[pltpu.VMEM((1, LANES), F32)],
        compiler_params=_cparams(("arbitrary",)),
        name="out_proj_route",
    )(a_p, b_p, x_p, a_s, b_s, x_s, w, ffn_g, w_r, bias)


def _pow2_pieces(limit):
    p = limit // 2
    while p >= 1:
        yield p
        p //= 2


def _dispatch_kernel(dest_ref, zrow_ref, zcnt_ref, pend_ref, hp_ref, xs_ref, hbuf, zbuf, sem, zsem, *, tm, sub):
    i = pl.program_id(0)
    n = pl.num_programs(0)
    slot = i % 2
    n_exp = zrow_ref.shape[0]
    n_sub_blocks = xs_ref.shape[0] // sub

    def row_copy(sl, r, d):
        return pltpu.make_async_copy(hbuf.at[sl, pl.ds(r, 1), :], xs_ref.at[pl.ds(d, 1), :], sem.at[sl])

    def wait_all(sl):
        for _ in range(TOP_K):
            pltpu.make_async_copy(hbuf.at[sl], xs_ref.at[pl.ds(0, tm), :], sem.at[sl]).wait()

    def zero_copy(row, size):
        return pltpu.make_async_copy(zbuf.at[pl.ds(0, size), :], xs_ref.at[pl.ds(row, size), :], zsem)

    def zero_fill(start):
        def per_expert(e, c):
            row = zrow_ref[e]
            cnt = zcnt_ref[e]
            head = (-row) & (SUBLANES - 1)
            for r in range(SUBLANES - 1):
                @pl.when(r < jnp.minimum(head, cnt))
                def _():
                    cp = zero_copy(row + r, 1)
                    cp.start() if start else cp.wait()
            rest = jnp.maximum(cnt - head, 0)
            row = pl.multiple_of(row + head, SUBLANES)
            for piece in _pow2_pieces(sub):
                if piece < SUBLANES:
                    break
                @pl.when((rest & piece) != 0)
                def _():
                    cp = zero_copy(row, piece)
                    cp.start() if start else cp.wait()
                row = pl.multiple_of(row + (rest & piece), SUBLANES)
            return c
        lax.fori_loop(0, n_exp, per_expert, 0)

        def per_block(b, c):
            @pl.when(b * sub >= pend_ref[0])
            def _():
                cp = zero_copy(b * sub, sub)
                cp.start() if start else cp.wait()
            return c
        lax.fori_loop(0, n_sub_blocks, per_block, 0)

    @pl.when(i == 0)
    def _():
        zbuf[...] = jnp.zeros(zbuf.shape, zbuf.dtype)
        zero_fill(True)

    hbuf[slot] = hp_ref[...]

    def issue(r, c):
        a = (i * tm + r) * TOP_K
        for k in range(TOP_K):
            row_copy(slot, r, dest_ref[a + k]).start(priority=k % 2)
        return c
    lax.fori_loop(0, tm, issue, 0, unroll=8)

    @pl.when(i == 0)
    def _():
        zero_fill(False)

    @pl.when(i > 0)
    def _():
        wait_all(1 - slot)

    @pl.when(i == n - 1)
    def _():
        wait_all(slot)


def _dispatch(dest, zrow, zcnt, pend, hp, *, tm, rows, sub):
    n, dw = hp.shape
    grid_spec = pltpu.PrefetchScalarGridSpec(
        num_scalar_prefetch=4, grid=(n // tm,),
        in_specs=[pl.BlockSpec((tm, dw), lambda i, *_: (i, 0))],
        out_specs=pl.BlockSpec(memory_space=pl.ANY),
        scratch_shapes=[pltpu.VMEM((2, tm, dw), hp.dtype), pltpu.VMEM((sub, dw), hp.dtype),
                        pltpu.SemaphoreType.DMA((2,)), pltpu.SemaphoreType.DMA(())],
    )
    return pl.pallas_call(
        functools.partial(_dispatch_kernel, tm=tm, sub=sub), grid_spec=grid_spec,
        out_shape=jax.ShapeDtypeStruct((rows, dw), hp.dtype),
        compiler_params=_cparams(("arbitrary",)),
        name="moe_dispatch",
    )(dest, zrow, zcnt, pend, hp)


def _experts_kernel(ie_ref, ir_ref, ins_ref, nit_ref, pend_ref, xs_ref, wg_hbm, wu_hbm, wd_hbm, ys_ref,
                    gu_ring, d_ring, xbuf, gacc, uacc, hbuf, ybuf, zbuf,
                    gu_sem, d_sem, x_sem, y_sem, z_sem, *, sub, n_sub_max, kc, fcs):
    n_items = nit_ref[0]
    d = ys_ref.shape[1]
    n_kc = d // kc
    n_wc = n_kc // 2
    n_fc = len(fcs)
    assert n_fc == 2 and n_kc >= 2 and n_kc % 2 == 0
    item_rows = sub * n_sub_max
    f_offs = [sum(fcs[:j]) for j in range(n_fc)]

    def gu_copies(item, pos, slot):
        e = ie_ref[item]
        rows = pl.ds(pos * kc, kc)
        return (pltpu.make_async_copy(wg_hbm.at[e, rows, :], gu_ring.at[slot, 0], gu_sem.at[slot]),
                pltpu.make_async_copy(wu_hbm.at[e, rows, :], gu_ring.at[slot, 1], gu_sem.at[slot]))

    def d_copy(item, j):
        e = ie_ref[item]
        return pltpu.make_async_copy(wd_hbm.at[e, pl.ds(f_offs[j], fcs[j]), :], d_ring.at[j, pl.ds(0, fcs[j]), :],
                                     d_sem.at[j])

    def gu_start(item, pos, slot):
        e = ie_ref[item]
        for m, w_hbm in enumerate((wg_hbm, wu_hbm)):
            for r0 in range(0, kc, sub):
                pltpu.make_async_copy(w_hbm.at[e, pl.ds(pos * kc + r0, sub), :],
                                      gu_ring.at[slot, m, pl.ds(r0, sub), :], gu_sem.at[slot]).start()

    def d_start(item, j):
        e = ie_ref[item]
        for r0 in range(0, fcs[j], sub):
            pltpu.make_async_copy(wd_hbm.at[e, pl.ds(f_offs[j] + r0, sub), :], d_ring.at[j, pl.ds(r0, sub), :],
                                  d_sem.at[j]).start()

    def x_xfer(item, slot, start):
        n_sub = ins_ref[item]
        for sb in range(n_sub_max):
            @pl.when(sb < n_sub)
            def _():
                r = pl.multiple_of(ir_ref[item] + sb * sub, sub)
                for k in range(n_wc):
                    cp = pltpu.make_async_copy(xs_ref.at[pl.ds(r, sub), pl.ds(k * kc, kc)],
                                               xbuf.at[slot, k, pl.ds(sb * sub, sub), :], x_sem.at[slot])
                    cp.start() if start else cp.wait()

    def y_copy(item, sb):
        r = pl.multiple_of(ir_ref[item] + sb * sub, sub)
        return pltpu.make_async_copy(ybuf.at[pl.ds(sb * sub, sub), :], ys_ref.at[pl.ds(r, sub), :], y_sem)

    def for_rows(n_sub, fn):
        for ns in range(1, n_sub_max + 1):
            @pl.when(n_sub == ns)
            def _():
                fn(ns * sub)

    def y_writes(item, start):
        n_sub = ins_ref[item]
        for sb in range(n_sub_max):
            @pl.when(sb < n_sub)
            def _():
                cp = y_copy(item, sb)
                cp.start() if start else cp.wait()

    zbuf[...] = jnp.zeros(zbuf.shape, F32)
    n_out_blocks = ys_ref.shape[0] // sub

    def tail(start):
        def body(b, c):
            @pl.when(b * sub >= pend_ref[0])
            def _():
                cp = pltpu.make_async_copy(zbuf, ys_ref.at[pl.ds(b * sub, sub), :], z_sem)
                cp.start() if start else cp.wait()
            return c
        lax.fori_loop(0, n_out_blocks, body, 0)
    tail(True)

    @pl.when(n_items > 0)
    def _():
        x_xfer(0, 0, True)
        for pos in range(GU_SLOTS):
            gu_start(0, pos, pos)

    def item_body(item, carry):
        xslot = item % 2
        n_sub = ins_ref[item]
        has_next = item + 1 < n_items

        @pl.when(has_next)
        def _():
            x_xfer(item + 1, 1 - xslot, True)

        x_xfer(item, xslot, False)

        def zero_acc(rows):
            gacc[0:rows, :] = jnp.zeros((rows, gacc.shape[1]), F32)
            uacc[0:rows, :] = jnp.zeros((rows, uacc.shape[1]), F32)
        for_rows(n_sub, zero_acc)

        def gu_step(pos, c):
            slot = (item * n_kc + pos) % GU_SLOTS
            for cp in gu_copies(item, pos, slot):
                cp.wait()

            def mm(rows):
                xw = xbuf[xslot, pos % n_wc, 0:rows, :]
                shift = jnp.asarray((pos // n_wc) * 16, jnp.uint32)
                xk = pltpu.bitcast((xw << shift) & jnp.uint32(0xFFFF0000), F32).astype(BF16)
                gacc[0:rows, :] += jnp.dot(xk, gu_ring[slot, 0].astype(BF16), preferred_element_type=F32)
                uacc[0:rows, :] += jnp.dot(xk, gu_ring[slot, 1].astype(BF16), preferred_element_type=F32)
            for_rows(n_sub, mm)

            @pl.when(pos + GU_SLOTS < n_kc)
            def _():
                gu_start(item, pos + GU_SLOTS, slot)

            @pl.when((pos + GU_SLOTS >= n_kc) & has_next)
            def _():
                gu_start(item + 1, pos + GU_SLOTS - n_kc, slot)
            for j in range(n_fc):
                @pl.when(pos + 2 == n_kc + j)
                def _():
                    d_start(item, j)
            return c
        lax.fori_loop(0, n_kc, gu_step, 0)

        def act(rows):
            g = gacc[0:rows, :]
            hbuf[0:rows, :] = (g * _sigmoid(g) * uacc[0:rows, :]).astype(BF16)
        for_rows(n_sub, act)

        @pl.when(item > 0)
        def _():
            y_writes(item - 1, False)

        for j in range(n_fc):
            d_copy(item, j).wait()

            def mm_down(rows, j=j):
                y = jnp.dot(hbuf[0:rows, f_offs[j]:f_offs[j] + fcs[j]], d_ring[j, 0:fcs[j], :].astype(BF16),
                            preferred_element_type=F32)
                if j == 0:
                    ybuf[0:rows, :] = y
                else:
                    ybuf[0:rows, :] += y
            for_rows(n_sub, mm_down)
        y_writes(item, True)
        return carry

    lax.fori_loop(0, n_items, item_body, 0)

    @pl.when(n_items > 0)
    def _():
        y_writes(n_items - 1, False)
    tail(False)


def _experts(item_e, item_row, item_nsub, n_items, pend, xs, w_gate, w_up, w_down, *, sub, n_sub_max):
    rows = xs.shape[0]
    d, f = w_gate.shape[1:]
    kc = 512
    lanes_f = f // LANES
    fcs = ((lanes_f + 1) // 2 * LANES, lanes_f // 2 * LANES)
    item_rows = sub * n_sub_max
    any_spec = pl.BlockSpec(memory_space=pl.ANY)
    kern = functools.partial(_experts_kernel, sub=sub, n_sub_max=n_sub_max, kc=kc, fcs=fcs)
    return pl.pallas_call(
        kern,
        grid_spec=pltpu.PrefetchScalarGridSpec(
            num_scalar_prefetch=5, grid=(1,),
            in_specs=[any_spec, any_spec, any_spec, any_spec],
            out_specs=any_spec,
            scratch_shapes=[
                pltpu.VMEM((GU_SLOTS, 2, kc, f), F32),
                pltpu.VMEM((2, fcs[0], d), F32),
                pltpu.VMEM((2, d // kc // 2, item_rows, kc), xs.dtype),
                pltpu.VMEM((item_rows, f), F32),
                pltpu.VMEM((item_rows, f), F32),
                pltpu.VMEM((item_rows, f), BF16),
                pltpu.VMEM((item_rows, d), F32),
                pltpu.VMEM((sub, d), F32),
                pltpu.SemaphoreType.DMA((GU_SLOTS,)),
                pltpu.SemaphoreType.DMA((2,)),
                pltpu.SemaphoreType.DMA((2,)),
                pltpu.SemaphoreType.DMA(()),
                pltpu.SemaphoreType.DMA(()),
            ],
        ),
        out_shape=jax.ShapeDtypeStruct((rows, d), F32),
        compiler_params=_cparams(("arbitrary",), 60 * 1024 * 1024),
        name="moe_experts",
    )(item_e, item_row, item_nsub, n_items, pend, xs, w_gate, w_up, w_down)


def _combine_kernel(dest_ref, x_ref, info_ref, g_ref, ys_ref, yp_ref, ys_out_ref, rbuf, sem, *, tm, n_prompt_tiles):
    i = pl.program_id(0)
    n = pl.num_programs(0)
    slot = i % 2

    def row_copy(sl, k, r, d):
        return pltpu.make_async_copy(ys_ref.at[pl.ds(d, 1), :], rbuf.at[sl, k, pl.ds(r, 1), :], sem.at[sl])

    def issue(step, sl):
        def body(r, c):
            a = (step * tm + r) * TOP_K
            for k in range(TOP_K):
                row_copy(sl, k, r, dest_ref[a + k]).start(priority=k % 2)
            return c
        lax.fori_loop(0, tm, body, 0, unroll=8)

    @pl.when(i == 0)
    def _():
        issue(0, 0)

    @pl.when(i + 1 < n)
    def _():
        issue(i + 1, 1 - slot)

    for k in range(TOP_K):
        pltpu.make_async_copy(ys_ref.at[pl.ds(0, tm), :], rbuf.at[slot, k], sem.at[slot]).wait()

    info = info_ref[...]
    lane_i = lax.broadcasted_iota(jnp.int32, info.shape, 1)
    w1 = jnp.sum(jnp.where(lane_i == 4, info, 0.0), axis=-1, keepdims=True)
    w2 = jnp.sum(jnp.where(lane_i == 5, info, 0.0), axis=-1, keepdims=True)
    y = x_ref[...] + (rbuf[slot, 0] * w1 + rbuf[slot, 1] * w2)
    out = _rms(y, g_ref[...])

    @pl.when(i < n_prompt_tiles)
    def _():
        yp_ref[...] = out

    @pl.when(i >= n_prompt_tiles)
    def _():
        ys_out_ref[...] = out


def _combine(dest, x1, info, final_g, ys, *, tm, n_prompt, n_sample):
    n, d = x1.shape
    npt = n_prompt // tm
    grid_spec = pltpu.PrefetchScalarGridSpec(
        num_scalar_prefetch=1, grid=(n // tm,),
        in_specs=[
            pl.BlockSpec((tm, d), lambda i, ds: (i, 0)),
            pl.BlockSpec((tm, LANES), lambda i, ds: (i, 0)),
            pl.BlockSpec((1, d), lambda i, ds: (0, 0)),
            pl.BlockSpec(memory_space=pl.ANY),
        ],
        out_specs=[
            pl.BlockSpec((tm, d), lambda i, ds: (jnp.minimum(i, npt - 1), 0)),
            pl.BlockSpec((tm, d), lambda i, ds: (jnp.maximum(i - npt, 0), 0)),
        ],
        scratch_shapes=[pltpu.VMEM((2, TOP_K, tm, d), F32), pltpu.SemaphoreType.DMA((2,))],
    )
    return pl.pallas_call(
        functools.partial(_combine_kernel, tm=tm, n_prompt_tiles=npt), grid_spec=grid_spec,
        out_shape=[jax.ShapeDtypeStruct((n_prompt, d), F32), jax.ShapeDtypeStruct((n_sample, d), F32)],
        compiler_params=_cparams(("arbitrary",)),
        name="moe_combine",
    )(dest, x1, info, final_g, ys)


def _rope_tables(pos):
    half = ROPE_DIM // 2
    inv_freq = ROPE_BASE ** (-jnp.arange(half, dtype=F32) / half)
    ang = pos.astype(F32)[:, None] * inv_freq[None, :]
    cos, sin = jnp.cos(ang), jnp.sin(ang)
    z = jnp.zeros_like(cos)
    z2 = jnp.concatenate([z, z], axis=1)
    return (jnp.concatenate([cos, cos, z2], axis=1), jnp.concatenate([-sin, z, z2], axis=1),
            jnp.concatenate([z, sin, z2], axis=1))


def kernel(x_prompt, x_sample, cache_kv_latent, cache_k_rope, state_conv, page_table, attn_norm_g, w_in, q_norm_g,
           w_q_up, kv_norm_g, w_kv_up, conv_w, conv_b, conv_ln_g, conv_ln_b, w_out, ffn_norm_g, w_router_group,
           b_router_group, w_router_expert, b_router_expert, w_exp_gate, w_exp_up, w_exp_down, final_norm_g):
    batch, seq, d = x_prompt.shape
    nb, t_dec, _ = x_sample.shape
    depth, q_lora, n_heads, qk_dim = w_q_up.shape
    kv_lora = w_kv_up.shape[1]
    v_dim = w_kv_up.shape[3] - NOPE_DIM
    conv_ch = conv_w.shape[2]
    width = conv_w.shape[1]
    n_exp = w_exp_gate.shape[1]
    n_grp = w_router_group.shape[2]
    per_grp = n_exp // n_grp
    page = cache_kv_latent.shape[2]
    past = page_table.shape[1] * page
    assert depth == 1 and t_dec == 1 and qk_dim == NOPE_DIM + ROPE_DIM and v_dim == NOPE_DIM
    scale = float(qk_dim) ** -0.5
    n_p = batch * seq
    n_tok = n_p + nb

    w_in_p = w_in[0].astype(BF16)
    wq_p = jnp.pad(w_q_up[0], ((0, 0), (0, 0), (0, HEAD_PAD - qk_dim))).reshape(q_lora, n_heads * HEAD_PAD)
    wq_p = wq_p.astype(BF16)
    wkv_p = w_kv_up[0].reshape(kv_lora, n_heads * HEAD_PAD).astype(BF16)
    w_out_b = w_out[0].astype(BF16)
    w_r = jnp.concatenate([w_router_expert[0], w_router_group[0],
                           jnp.zeros((d, LANES - n_exp - n_grp), F32)], axis=1).astype(BF16)
    b_r = jnp.concatenate([b_router_expert[0], b_router_group[0], jnp.zeros((LANES - n_exp - n_grp,), F32)])[None]
    row2 = lambda v: v.reshape(1, -1)

    tm_p = 512
    rc_p, ra_p, rb_p = _rope_tables(jnp.arange(seq))
    rc_s, ra_s, rb_s = _rope_tables(jnp.full((nb,), past, jnp.int32))
    dims = dict(n_heads=n_heads, q_lora=q_lora, kv_lora=kv_lora, conv_ch=conv_ch)
    xp2 = x_prompt.reshape(n_p, d)
    xs2 = x_sample.reshape(nb, d)
    q_p, c_p, kr_p, u_p, kv_p, krb_p = _in_proj(
        xp2, row2(attn_norm_g[0]), w_in_p, row2(q_norm_g[0]), wq_p, row2(kv_norm_g[0]), wkv_p, rc_p, ra_p, rb_p,
        tm=tm_p, prompt=True, **dims)
    qs_s, c_s, kr_s, u_s = _in_proj(
        xs2, row2(attn_norm_g[0]), w_in_p, row2(q_norm_g[0]), wq_p, row2(kv_norm_g[0]), wkv_p, rc_s, ra_s, rb_s,
        tm=nb, prompt=False, **dims)

    o_att_p = _attn_prompt(q_p, kv_p, krb_p, batch=batch, seq=seq, n_heads=n_heads, v_dim=v_dim, scale=scale)
    cw, cb, cg, cbeta = conv_w[0], row2(conv_b[0]), row2(conv_ln_g[0]), row2(conv_ln_b[0])
    o_conv_p = _conv_prompt(u_p, cw, cb, cg, cbeta, batch=batch, seq=seq)

    o_lat = _attn_sample(page_table, qs_s.reshape(nb, n_heads, QS_HEAD), c_s.reshape(nb, 1, kv_lora),
                         kr_s.reshape(nb, 1, ROPE_DIM), cache_kv_latent[0],
                         jnp.transpose(cache_k_rope[0], (0, 2, 1)), scale=scale)
    o_att_s = _v_up(o_lat.reshape(nb, n_heads * kv_lora), wkv_p, n_heads=n_heads, kv_lora=kv_lora, v_dim=v_dim)
    state_t = jnp.transpose(state_conv[0], (1, 0, 2))
    o_conv_s, new_state_t = _conv_sample(state_t, u_s, cw, cb, cg, cbeta)

    x1, info, hp, counts = _out_proj_route(o_att_p, o_conv_p, xp2, o_att_s, o_conv_s, xs2, w_out_b,
                                           row2(ffn_norm_g[0]), w_r, b_r, tm=512, n_exp=n_exp, n_grp=n_grp,
                                           per_grp=per_grp)
    i32 = jnp.int32
    sub, item_rows = MOE_SUB, MOE_SUB * MOE_ITEM_SUBS
    n_assign = n_tok * TOP_K
    rows_alloc = (-(-n_assign // sub) + n_exp) * sub
    max_items = n_assign // item_rows + n_exp + 1
    cnt = counts[0, :n_exp].astype(i32)
    padded = (cnt + sub - 1) // sub * sub
    pad_end = jnp.cumsum(padded)
    pad_start = pad_end - padded
    e_flat = info[:, 0:TOP_K].astype(i32).reshape(-1)
    rank = info[:, TOP_K:2 * TOP_K].astype(i32).reshape(-1)
    is_e = e_flat[:, None] == jnp.arange(n_exp, dtype=i32)[None, :]
    dest = rank + jnp.sum(jnp.where(is_e, pad_start[None, :], 0), axis=1)
    items_per_e = (padded + item_rows - 1) // item_rows
    it_end = jnp.cumsum(items_per_e)
    it_start = it_end - items_per_e
    w_ids = jnp.arange(max_items, dtype=i32)
    item_e = jnp.minimum(jnp.sum((it_end[None, :] <= w_ids[:, None]).astype(i32), axis=1), n_exp - 1)
    item_i = w_ids - it_start[item_e]
    item_row = jnp.clip(pad_start[item_e] + item_rows * item_i, 0, rows_alloc - item_rows).astype(i32)
    item_nsub = jnp.clip((padded[item_e] - item_rows * item_i) // sub, 1, MOE_ITEM_SUBS).astype(i32)
    n_items = it_end[-1:].astype(i32)
    pend = pad_end[-1:].astype(i32)

    xs = _dispatch(dest, (pad_start + cnt).astype(i32), (padded - cnt).astype(i32), pend, hp, tm=TOK_TILE,
                   rows=rows_alloc, sub=sub)
    ys = _experts(item_e, item_row, item_nsub, n_items, pend, xs, w_exp_gate[0], w_exp_up[0], w_exp_down[0],
                  sub=sub, n_sub_max=MOE_ITEM_SUBS)
    y_p, y_s = _combine(dest, x1, info, row2(final_norm_g), ys, tm=CMB_TILE, n_prompt=n_p, n_sample=nb)

    hist = width - 1
    new_conv_p = u_p.reshape(batch, seq, conv_ch)[:, seq - hist:]
    new_conv_s = jnp.transpose(new_state_t, (1, 0, 2))
    return (y_p.reshape(batch, seq, d), y_s.reshape(nb, t_dec, d),
            c_p.reshape(1, batch, seq, kv_lora), kr_p.reshape(1, batch, seq, ROPE_DIM), new_conv_p[None],
            c_s.reshape(1, nb, t_dec, kv_lora), kr_s.reshape(1, nb, t_dec, ROPE_DIM), new_conv_s[None])
```
